```python
import math
import jax, jax.numpy as jnp
from jax import lax
import numpy as np

D_MODEL = 2048
BATCH = 2
SEQ = 8192
DEPTH = 4

N_MIXERS = 3
PLE_DIM = 256
D_FF = 4 * D_MODEL
CHUNK = 64
EPS = 1e-6

GLA_HEADS = 4
GLA_DK = D_MODEL // 2 // GLA_HEADS
GLA_DV = D_MODEL // GLA_HEADS
GLA_GATE_RANK = 16
GLA_GATE_NORM = 16.0
HGRN_EXPAND = 128
HGRN_HEADS = D_MODEL // HGRN_EXPAND
HGRN_DK = HGRN_EXPAND
HGRN_DV = D_MODEL // HGRN_HEADS
SSM_EXPAND = 2
SSM_DINNER = SSM_EXPAND * D_MODEL
SSM_HEADDIM = 64
SSM_HEADS = SSM_DINNER // SSM_HEADDIM
SSM_GROUPS = 8
SSM_STATE = 128
SSM_CONV = 4
SSM_CONV_DIM = SSM_DINNER + 2 * SSM_GROUPS * SSM_STATE

N_GLA = len(range(0, DEPTH, N_MIXERS))
N_HGRN = len(range(1, DEPTH, N_MIXERS))
N_SSM = len(range(2, DEPTH, N_MIXERS))

kernel_name = 'hybrid_gla_hgrn2_mamba2_trunk'


def rmsnorm(x, w):
    xf = x.astype(jnp.float32)
    y = xf * lax.rsqrt(jnp.mean(xf * xf, axis=-1, keepdims=True) + EPS)
    return (y * w.astype(jnp.float32)).astype(x.dtype)


def to_chunks(t):
    b, l = t.shape[0], t.shape[1]
    t = t.reshape((b, l // CHUNK, CHUNK) + t.shape[2:])
    return jnp.moveaxis(t, 1, 0)


def from_chunks(t):
    t = jnp.moveaxis(t, 0, 1)
    return t.reshape((t.shape[0], t.shape[1] * t.shape[2]) + t.shape[3:])


def chunk_gla(q, k, v, g):
    bsz, _, h, dk = q.shape
    dv = v.shape[-1]
    tri = jnp.tril(jnp.ones((CHUNK, CHUNK), dtype=bool))

    def step(s, inp):
        qc, kc, vc, gc = inp
        b = jnp.cumsum(gc, axis=1)
        b_last = b[:, -1]
        o_inter = jnp.einsum('bthd,bhdv->bthv', qc * jnp.exp(b), s)
        rel = b[:, :, None] - b[:, None, :]
        decay = jnp.exp(jnp.where(tri[None, :, :, None, None], rel, -jnp.inf))
        att = jnp.einsum('bthd,bshd,btshd->bhts', qc, kc, decay)
        o_intra = jnp.einsum('bhts,bshv->bthv', att, vc)
        k_dec = kc * jnp.exp(b_last[:, None] - b)
        s = jnp.exp(b_last)[..., None] * s + jnp.einsum('bshd,bshv->bhdv', k_dec, vc)
        return s, o_inter + o_intra

    s0 = jnp.zeros((bsz, h, dk, dv), jnp.float32)
    _, o = lax.scan(step, s0, (to_chunks(q), to_chunks(k), to_chunks(v), to_chunks(g)))
    return from_chunks(o)


def chunk_ssd(x, dt, a, bm, cm):
    bsz, l, h, p = x.shape
    g, n = bm.shape[2], bm.shape[3]
    hpg = h // g
    tri = jnp.tril(jnp.ones((CHUNK, CHUNK), dtype=bool))
    xdt = (x * dt[..., None]).reshape(bsz, l, g, hpg, p)
    la = (dt * a).reshape(bsz, l, g, hpg)

    def step(s, inp):
        xc, lac, bc, cc = inp
        b = jnp.cumsum(lac, axis=1)
        b_last = b[:, -1]
        y_inter = jnp.einsum('btgn,btgh,bghnp->btghp', cc, jnp.exp(b), s)
        rel = b[:, :, None] - b[:, None, :]
        decay = jnp.exp(jnp.where(tri[None, :, :, None, None], rel, -jnp.inf))
        cb = jnp.einsum('btgn,bsgn->btsg', cc, bc)
        y_intra = jnp.einsum('btsg,btsgh,bsghp->btghp', cb, decay, xc)
        w_end = jnp.exp(b_last[:, None] - b)
        s = jnp.exp(b_last)[..., None, None] * s + jnp.einsum('bsgn,bsgh,bsghp->bghnp', bc, w_end, xc)
        return s, y_inter + y_intra

    s0 = jnp.zeros((bsz, g, hpg, n, p), jnp.float32)
    _, y = lax.scan(step, s0, (to_chunks(xdt), to_chunks(la), to_chunks(bm), to_chunks(cm)))
    return from_chunks(y).reshape(bsz, l, h, p)


def gla_mixer(u, w_in, w_gk2, b_gk, gn_w, w_out):
    bsz, l, _ = u.shape
    f32 = jnp.float32
    kd, vd = GLA_HEADS * GLA_DK, GLA_HEADS * GLA_DV
    q, k, v, og, gk_lr = jnp.split(u @ w_in, [kd, 2 * kd, 2 * kd + vd, 2 * kd + 2 * vd], axis=-1)
    gk = jax.nn.log_sigmoid((gk_lr @ w_gk2 + b_gk).astype(f32)) / GLA_GATE_NORM
    q = q.reshape(bsz, l, GLA_HEADS, GLA_DK).astype(f32) * (GLA_DK ** -0.5)
    k = k.reshape(bsz, l, GLA_HEADS, GLA_DK).astype(f32)
    v = v.reshape(bsz, l, GLA_HEADS, GLA_DV).astype(f32)
    gk = gk.reshape(bsz, l, GLA_HEADS, GLA_DK)
    o = chunk_gla(q, k, v, gk)
    o = rmsnorm(o, gn_w) * jax.nn.silu(og.reshape(bsz, l, GLA_HEADS, GLA_DV).astype(f32))
    return o.reshape(bsz, l, vd).astype(u.dtype) @ w_out


def hgrn2_mixer(u, lb, w_in, gn_w, w_out):
    bsz, l, _ = u.shape
    f32 = jnp.float32
    fd, vd = HGRN_HEADS * HGRN_DK, HGRN_HEADS * HGRN_DV
    q, fz, i, og = jnp.split(u @ w_in, [fd, 2 * fd, 2 * fd + vd], axis=-1)
    f = lb + (1.0 - lb) * jax.nn.sigmoid(fz.astype(f32))
    k = (1.0 - f).reshape(bsz, l, HGRN_HEADS, HGRN_DK)
    g = jnp.log(f).reshape(bsz, l, HGRN_HEADS, HGRN_DK)
    q = jax.nn.silu(q.astype(f32)).reshape(bsz, l, HGRN_HEADS, HGRN_DK) * (HGRN_DK ** -0.5)
    v = i.astype(f32).reshape(bsz, l, HGRN_HEADS, HGRN_DV)
    o = chunk_gla(q, k, v, g)
    o = rmsnorm(o, gn_w) * jax.nn.sigmoid(og.reshape(bsz, l, HGRN_HEADS, HGRN_DV).astype(f32))
    return o.reshape(bsz, l, vd).astype(u.dtype) @ w_out


def mamba2_mixer(u, w_in, conv_w, conv_b, dt_bias, a_log, d_skip, norm_w, w_out):
    bsz, l, _ = u.shape
    f32 = jnp.float32
    z, xbc, dt = jnp.split(u @ w_in, [SSM_DINNER, SSM_DINNER + SSM_CONV_DIM], axis=-1)
    xbc = lax.conv_general_dilated(
        xbc, conv_w[:, None, :], window_strides=(1,), padding=[(SSM_CONV - 1, 0)],
        dimension_numbers=('NWC', 'WIO', 'NWC'), feature_group_count=SSM_CONV_DIM) + conv_b
    xbc = jax.nn.silu(xbc)
    xs, bm, cm = jnp.split(xbc, [SSM_DINNER, SSM_DINNER + SSM_GROUPS * SSM_STATE], axis=-1)
    dt = jax.nn.softplus(dt.astype(f32) + dt_bias.astype(f32))
    a = -jnp.exp(a_log.astype(f32))
    x4 = xs.astype(f32).reshape(bsz, l, SSM_HEADS, SSM_HEADDIM)
    bm = bm.astype(f32).reshape(bsz, l, SSM_GROUPS, SSM_STATE)
    cm = cm.astype(f32).reshape(bsz, l, SSM_GROUPS, SSM_STATE)
    y = chunk_ssd(x4, dt, a, bm, cm) + d_skip.astype(f32)[:, None] * x4
    y = y.reshape(bsz, l, SSM_DINNER) * jax.nn.silu(z.astype(f32))
    y = y.reshape(bsz, l, SSM_GROUPS, SSM_DINNER // SSM_GROUPS)
    y = rmsnorm(y, norm_w.reshape(SSM_GROUPS, SSM_DINNER // SSM_GROUPS)).reshape(bsz, l, SSM_DINNER)
    return y.astype(u.dtype) @ w_out


def sq_relu_mlp(u, w_up, w_down):
    return jnp.square(jax.nn.relu(u @ w_up)) @ w_down


def setup_inputs(seed: int = 0) -> dict:
    key = jax.random.key(seed)
    ks = jax.random.split(key, 32)
    f32 = jnp.float32

    def nrm(k, shape, scale):
        return jax.random.normal(k, shape, f32) * scale

    def gain(k, shape):
        return 1.0 + 0.02 * jax.random.normal(k, shape, f32)

    res = (2.0 * DEPTH) ** -0.5
    gla_in = 2 * GLA_HEADS * GLA_DK + 2 * GLA_HEADS * GLA_DV + GLA_GATE_RANK
    hgrn_in = 2 * HGRN_HEADS * HGRN_DK + 2 * HGRN_HEADS * HGRN_DV
    ssm_in = 2 * SSM_DINNER + 2 * SSM_GROUPS * SSM_STATE + SSM_HEADS
    dt0 = jnp.exp(jax.random.uniform(ks[26], (N_SSM, SSM_HEADS), f32, math.log(1e-3), math.log(1e-1)))
    return {
        'x': nrm(ks[0], (BATCH, SEQ, D_MODEL), 1.0),
        'p': nrm(ks[1], (DEPTH, BATCH, SEQ, PLE_DIM), 1.0),
        'norm_mix': gain(ks[2], (DEPTH, D_MODEL)),
        'norm_mlp': gain(ks[3], (DEPTH, D_MODEL)),
        'norm_ple': gain(ks[4], (DEPTH, D_MODEL)),
        'norm_final': gain(ks[5], (D_MODEL,)),
        'w_up': nrm(ks[6], (DEPTH, D_MODEL, D_FF), D_MODEL ** -0.5),
        'w_down': nrm(ks[7], (DEPTH, D_FF, D_MODEL), res * D_FF ** -0.5),
        'w_ple_proj': nrm(ks[8], (DEPTH, PLE_DIM, D_MODEL), res * PLE_DIM ** -0.5),
        'w_ple_gate': nrm(ks[9], (DEPTH, D_MODEL, D_MODEL), D_MODEL ** -0.5),
        'gla_w_in': nrm(ks[10], (N_GLA, D_MODEL, gla_in), D_MODEL ** -0.5),
        'gla_w_gk2': nrm(ks[11], (N_GLA, GLA_GATE_RANK, GLA_HEADS * GLA_DK), GLA_GATE_RANK ** -0.5),
        'gla_b_gk': nrm(ks[12], (N_GLA, GLA_HEADS * GLA_DK), 0.01),
        'gla_gn': gain(ks[13], (N_GLA, GLA_DV)),
        'gla_w_out': nrm(ks[14], (N_GLA, GLA_HEADS * GLA_DV, D_MODEL), res * (GLA_HEADS * GLA_DV) ** -0.5),
        'hgrn_lb_logits': nrm(ks[15], (DEPTH, HGRN_HEADS * HGRN_DK), 0.5),
        'hgrn_w_in': nrm(ks[16], (N_HGRN, D_MODEL, hgrn_in), D_MODEL ** -0.5),
        'hgrn_gn': gain(ks[17], (N_HGRN, HGRN_DV)),
        'hgrn_w_out': nrm(ks[18], (N_HGRN, HGRN_HEADS * HGRN_DV, D_MODEL), res * (HGRN_HEADS * HGRN_DV) ** -0.5),
        'ssm_w_in': nrm(ks[19], (N_SSM, D_MODEL, ssm_in), D_MODEL ** -0.5),
        'ssm_conv_w': nrm(ks[20], (N_SSM, SSM_CONV, SSM_CONV_DIM), SSM_CONV ** -0.5),
        'ssm_conv_b': nrm(ks[21], (N_SSM, SSM_CONV_DIM), 0.01),
        'ssm_dt_bias': dt0 + jnp.log(-jnp.expm1(-dt0)),
        'ssm_a_log': jnp.log(jax.random.uniform(ks[22], (N_SSM, SSM_HEADS), f32, 1.0, 16.0)),
        'ssm_d': gain(ks[23], (N_SSM, SSM_HEADS)),
        'ssm_norm': gain(ks[24], (N_SSM, SSM_DINNER)),
        'ssm_w_out': nrm(ks[25], (N_SSM, SSM_DINNER, D_MODEL), res * SSM_DINNER ** -0.5),
    }


def reference(x, p, norm_mix, norm_mlp, norm_ple, norm_final, w_up, w_down, w_ple_proj, w_ple_gate,
              gla_w_in, gla_w_gk2, gla_b_gk, gla_gn, gla_w_out,
              hgrn_lb_logits, hgrn_w_in, hgrn_gn, hgrn_w_out,
              ssm_w_in, ssm_conv_w, ssm_conv_b, ssm_dt_bias, ssm_a_log, ssm_d, ssm_norm, ssm_w_out):
    gamma = jnp.cumsum(jax.nn.softmax(hgrn_lb_logits.astype(jnp.float32), axis=0), axis=0)
    lower_bounds = gamma - gamma[0]
    h = x
    for i in range(DEPTH):
        kind, j = i % N_MIXERS, i // N_MIXERS
        u = rmsnorm(h, norm_mix[i])
        if kind == 0:
            mix = gla_mixer(u, gla_w_in[j], gla_w_gk2[j], gla_b_gk[j], gla_gn[j], gla_w_out[j])
        elif kind == 1:
            mix = hgrn2_mixer(u, lower_bounds[i], hgrn_w_in[j], hgrn_gn[j], hgrn_w_out[j])
        else:
            mix = mamba2_mixer(u, ssm_w_in[j], ssm_conv_w[j], ssm_conv_b[j], ssm_dt_bias[j],
                               ssm_a_log[j], ssm_d[j], ssm_norm[j], ssm_w_out[j])
        h = h + mix.astype(h.dtype)
        h = h + sq_relu_mlp(rmsnorm(h, norm_mlp[i]), w_up[i], w_down[i])
        gate = jax.nn.sigmoid(rmsnorm(h, norm_ple[i]) @ w_ple_gate[i])
        h = h + gate * (p[i] @ w_ple_proj[i])
    return rmsnorm(h, norm_final)
```

```python
import functools
import math

import numpy as np
import jax
import jax.numpy as jnp
from jax import lax
from jax.experimental import pallas as pl
from jax.experimental.pallas import tpu as pltpu

F32 = jnp.float32
BF16 = jnp.bfloat16

D_MODEL = 2048
DEPTH = 4
N_MIXERS = 3
PLE_DIM = 256
D_FF = 4 * D_MODEL
EPS = 1e-6

GLA_HEADS = 4
GLA_DK = 256
GLA_DV = 512
GLA_GATE_RANK = 16
GLA_GATE_NORM = 16.0
HGRN_HEADS = 16
HGRN_DK = 128
HGRN_DV = 128
SSM_DINNER = 4096
SSM_HEADDIM = 64
SSM_HEADS = 64
SSM_GROUPS = 8
SSM_STATE = 128
SSM_CONV = 4
SSM_CONV_DIM = SSM_DINNER + 2 * SSM_GROUPS * SSM_STATE

LANES = 128
VMEM_LIMIT_BYTES = 56 * 1024 * 1024

GLA_CHUNK = 64
GLA_TOKENS_PER_STEP = 512
SSD_CHUNK = 128
CONV_HALO = 8

GLA_IN_COLS = 6400
SSM_IN_COLS = 10752
N_LEVELS = 6


def _cparams(sem):
    return pltpu.CompilerParams(dimension_semantics=sem, vmem_limit_bytes=VMEM_LIMIT_BYTES)


def _rms(x, w):
    ms = jnp.mean(x * x, axis=-1, keepdims=True)
    return x * lax.rsqrt(ms + EPS) * w


def _split3(x):
    hi = x.astype(BF16)
    r1 = x - hi.astype(F32)
    mid = r1.astype(BF16)
    lo = (r1 - mid.astype(F32)).astype(BF16)
    return hi, mid, lo


def _norm_matmul_kernel(h_ref, nw_ref, w_ref, o_ref, u_ref):
    @pl.when(pl.program_id(1) == 0)
    def _():
        u_ref[...] = _rms(h_ref[...], nw_ref[...]).astype(BF16)

    o_ref[...] = jnp.dot(u_ref[...], w_ref[...], preferred_element_type=F32).astype(o_ref.dtype)


def _norm_matmul(h, nw, w, *, tm, tn):
    t, d = h.shape
    n = w.shape[1]
    return pl.pallas_call(
        _norm_matmul_kernel,
        grid=(t // tm, n // tn),
        in_specs=[
            pl.BlockSpec((tm, d), lambda i, j: (i, 0)),
            pl.BlockSpec((1, d), lambda i, j: (0, 0)),
            pl.BlockSpec((d, tn), lambda i, j: (0, j)),
        ],
        out_specs=pl.BlockSpec((tm, tn), lambda i, j: (i, j)),
        out_shape=jax.ShapeDtypeStruct((t, n), F32),
        scratch_shapes=[pltpu.VMEM((tm, d), BF16)],
        compiler_params=_cparams(("parallel", "arbitrary")),
        name="norm_matmul",
    )(h, nw, w)


def _matmul_res_kernel(a_ref, w_ref, h_ref, o_ref):
    o_ref[...] = h_ref[...] + jnp.dot(a_ref[...], w_ref[...], preferred_element_type=F32)


def _matmul_res(a, w, h, *, tm, tn):
    t, k = a.shape
    n = w.shape[1]
    return pl.pallas_call(
        _matmul_res_kernel,
        grid=(t // tm, n // tn),
        in_specs=[
            pl.BlockSpec((tm, k), lambda i, j: (i, 0)),
            pl.BlockSpec((k, tn), lambda i, j: (0, j)),
            pl.BlockSpec((tm, tn), lambda i, j: (i, j)),
        ],
        out_specs=pl.BlockSpec((tm, tn), lambda i, j: (i, j)),
        out_shape=jax.ShapeDtypeStruct((t, n), F32),
        compiler_params=_cparams(("parallel", "arbitrary")),
        name="matmul_res",
    )(a, w, h)


def _mlp_kernel(h_ref, nw_ref, wu_ref, wd_ref, o_ref, u_ref):
    @pl.when(pl.program_id(1) == 0)
    def _():
        x = h_ref[...]
        u_ref[...] = _rms(x, nw_ref[...]).astype(BF16)
        o_ref[...] = x

    a = jnp.dot(u_ref[...], wu_ref[...], preferred_element_type=F32)
    a = jnp.square(jnp.maximum(a, 0.0)).astype(BF16)
    o_ref[...] += jnp.dot(a, wd_ref[...], preferred_element_type=F32)


def _mlp(h, nw, wu, wd, *, tm, tf):
    t, d = h.shape
    ff = wu.shape[1]
    return pl.pallas_call(
        _mlp_kernel,
        grid=(t // tm, ff // tf),
        in_specs=[
            pl.BlockSpec((tm, d), lambda i, f: (i, 0)),
            pl.BlockSpec((1, d), lambda i, f: (0, 0)),
            pl.BlockSpec((d, tf), lambda i, f: (0, f)),
            pl.BlockSpec((tf, d), lambda i, f: (f, 0)),
        ],
        out_specs=pl.BlockSpec((tm, d), lambda i, f: (i, 0)),
        out_shape=jax.ShapeDtypeStruct((t, d), F32),
        scratch_shapes=[pltpu.VMEM((tm, d), BF16)],
        compiler_params=_cparams(("parallel", "arbitrary")),
        name="mlp",
    )(h, nw, wu, wd)


def _ple_kernel(h_ref, nw_ref, wg_ref, p_ref, wp_ref, nf_ref, o_ref, *, final_norm, tn):
    x = h_ref[...]
    u = _rms(x, nw_ref[...]).astype(BF16)
    pb = p_ref[...].astype(BF16)
    d = x.shape[1]
    for n0 in range(0, d, tn):
        cs = slice(n0, n0 + tn)
        gate = jax.nn.sigmoid(jnp.dot(u, wg_ref[:, cs], preferred_element_type=F32))
        proj = jnp.dot(pb, wp_ref[:, cs], preferred_element_type=F32)
        o_ref[:, cs] = h_ref[:, cs] + gate * proj
    if final_norm:
        o_ref[...] = _rms(o_ref[...], nf_ref[...])


def _ple(h, nw, wg, p, wp, nf, *, final_norm, tm, tn):
    t, d = h.shape
    pd = p.shape[1]
    return pl.pallas_call(
        functools.partial(_ple_kernel, final_norm=final_norm, tn=tn),
        grid=(t // tm,),
        in_specs=[
            pl.BlockSpec((tm, d), lambda i: (i, 0)),
            pl.BlockSpec((1, d), lambda i: (0, 0)),
            pl.BlockSpec((d, d), lambda i: (0, 0)),
            pl.BlockSpec((tm, pd), lambda i: (i, 0)),
            pl.BlockSpec((pd, d), lambda i: (0, 0)),
            pl.BlockSpec((1, d), lambda i: (0, 0)),
        ],
        out_specs=pl.BlockSpec((tm, d), lambda i: (i, 0)),
        out_shape=jax.ShapeDtypeStruct((t, d), F32),
        compiler_params=_cparams(("parallel",)),
        name="ple",
    )(h, nw, wg, p, wp, nf)


def _gla_arg_matrix():
    c = GLA_CHUNK
    m = np.zeros((2 + N_LEVELS, c, c), np.float32)
    t = np.arange(c)[:, None]
    r = np.arange(c)[None, :]
    m[0] = (r <= t)
    m[1] = (r > t)
    for lvl in range(N_LEVELS):
        n = c >> (lvl + 1)
        ref = (t // (2 * n)) * (2 * n) + n - 1
        lower = t > ref
        m[2 + lvl] = np.where(lower, (r > ref) & (r <= t), (r > t) & (r <= ref))
    m = m.reshape((2 + N_LEVELS) * c, c)
    return np.concatenate([m, m, m, np.zeros_like(m)], axis=1)


def _gla_masks():
    c = GLA_CHUNK
    t = np.arange(c)[:, None]
    s = np.arange(c)[None, :]
    masks = np.zeros((1 + N_LEVELS, c, c), np.float32)
    masks[0] = (t == s)
    for lvl in range(N_LEVELS):
        n = c >> (lvl + 1)
        same = (t // (2 * n)) == (s // (2 * n))
        masks[1 + lvl] = same & ((t % (2 * n)) >= n) & ((s % (2 * n)) < n)
    return masks


_NT = (((1,), (1,)), ((), ()))
_TN = (((0,), (0,)), ((), ()))


def _gla_core_kernel(*refs, mode, layer, scale):
    if mode == "gla":
        (q_ref, k_ref, v_ref, og_ref, glr_ref, wgk_ref, bgk_ref, gn_ref, mall_ref, masks_ref,
         o_ref, st_ref) = refs
    else:
        (q_ref, k_ref, v_ref, og_ref, lbl_ref, gn_ref, mall_ref, masks_ref, o_ref, st_ref) = refs

    @pl.when(pl.program_id(2) == 0)
    def _():
        st_ref[...] = jnp.zeros_like(st_ref)

    if mode == "hgrn":
        lg = lbl_ref[...]
        e = jnp.exp(lg - jnp.max(lg, axis=0, keepdims=True))
        sm = e / jnp.sum(e, axis=0, keepdims=True)
        lb = jnp.zeros_like(sm[0:1])
        for r in range(1, layer + 1):
            lb = lb + sm[r:r + 1]

    c = GLA_CHUNK
    n_chunks = q_ref.shape[0] // c

    def chunk(ci, carry):
        rows = pl.ds(pl.multiple_of(ci * c, c), c)
        vc = v_ref[rows, :]
        og = og_ref[rows, :]
        if mode == "gla":
            qc = q_ref[rows, :] * scale
            kc = k_ref[rows, :]
            z = jnp.dot(glr_ref[rows, :].astype(BF16), wgk_ref[...], preferred_element_type=F32)
            z = z + bgk_ref[...]
            gc = (jnp.minimum(z, 0.0) - jnp.log1p(jnp.exp(-jnp.abs(z)))) * (1.0 / GLA_GATE_NORM)
            gate = og * jax.nn.sigmoid(og)
        else:
            qq = q_ref[rows, :]
            qc = qq * jax.nn.sigmoid(qq) * scale
            f = lb + (1.0 - lb) * jax.nn.sigmoid(k_ref[rows, :])
            kc = 1.0 - f
            gc = jnp.log(f)
            gate = jax.nn.sigmoid(og)

        g_hi, g_mid, g_lo = _split3(gc)
        gs = jnp.concatenate([g_hi, g_mid, g_lo, jnp.zeros_like(g_hi)], axis=0)
        ex = jnp.exp(jnp.dot(mall_ref[...], gs, preferred_element_type=F32))

        st = st_ref[...]
        q_in = (qc * ex[0:c]).astype(BF16)
        o = lax.dot_general(q_in, st.astype(BF16), _NT, preferred_element_type=F32)
        k_dec = (kc * ex[c:2 * c]).astype(BF16)

        att = masks_ref[0] * lax.dot_general(qc.astype(BF16), kc.astype(BF16), _NT,
                                             preferred_element_type=F32)
        for lvl in range(N_LEVELS):
            el = ex[(2 + lvl) * c:(3 + lvl) * c]
            ql = (qc * el).astype(BF16)
            kl = (kc * el).astype(BF16)
            att = att + masks_ref[1 + lvl] * lax.dot_general(ql, kl, _NT,
                                                             preferred_element_type=F32)
        vb = vc.astype(BF16)
        o = o + jnp.dot(att.astype(BF16), vb, preferred_element_type=F32)

        e_last = ex[c - 1:c]
        st_ref[...] = st * e_last + lax.dot_general(vb, k_dec, _TN, preferred_element_type=F32)

        o_ref[rows, :] = (_rms(o, gn_ref[...]) * gate).astype(o_ref.dtype)
        return carry

    lax.fori_loop(0, n_chunks, chunk, 0)


def _gla_core(proj, *, mode, layer, batch, seq, heads, dk, dv, col_q, col_k, col_v, col_og,
              extra, gn):
    tl = GLA_TOKENS_PER_STEP
    nl = seq // tl
    t = batch * seq
    mall = jnp.asarray(_gla_arg_matrix(), BF16)
    masks = jnp.asarray(_gla_masks(), F32)

    def rowblk(b, h, l):
        return b * nl + l

    in_specs = [
        pl.BlockSpec((tl, dk), lambda b, h, l: (rowblk(b, h, l), col_q // dk + h)),
        pl.BlockSpec((tl, dk), lambda b, h, l: (rowblk(b, h, l), col_k // dk + h)),
        pl.BlockSpec((tl, dv), lambda b, h, l: (rowblk(b, h, l), col_v // dv + h)),
        pl.BlockSpec((tl, dv), lambda b, h, l: (rowblk(b, h, l), col_og // dv + h)),
    ]
    args = [proj, proj, proj, proj]
    if mode == "gla":
        col_glr, wgk, bgk = extra
        in_specs += [
            pl.BlockSpec((tl, LANES), lambda b, h, l: (rowblk(b, h, l), col_glr // LANES)),
            pl.BlockSpec((LANES, dk), lambda b, h, l: (0, h)),
            pl.BlockSpec((1, dk), lambda b, h, l: (0, h)),
        ]
        args += [proj, wgk, bgk]
    else:
        (lbl,) = extra
        in_specs += [pl.BlockSpec((DEPTH, dk), lambda b, h, l: (0, h))]
        args += [lbl]
    in_specs += [
        pl.BlockSpec((1, dv), lambda b, h, l: (0, 0)),
        pl.BlockSpec(mall.shape, lambda b, h, l: (0, 0)),
        pl.BlockSpec(masks.shape, lambda b, h, l: (0, 0, 0)),
    ]
    args += [gn, mall, masks]
    return pl.pallas_call(
        functools.partial(_gla_core_kernel, mode=mode, layer=layer, scale=dk ** -0.5),
        grid=(batch, heads, nl),
        in_specs=in_specs,
        out_specs=pl.BlockSpec((tl, dv), lambda b, h, l: (rowblk(b, h, l), h)),
        out_shape=jax.ShapeDtypeStruct((t, heads * dv), BF16),
        scratch_shapes=[pltpu.VMEM((dv, dk), F32)],
        compiler_params=_cparams(("parallel", "parallel", "arbitrary")),
        name=mode + "_core",
    )(*args)


N_PAIRS = SSM_HEADS // 2
PAIRS_PER_GROUP = N_PAIRS // SSM_GROUPS


def _ssd_kernel(z_ref, x_ref, bm_ref, cm_ref, dt_ref, cw_ref, cb_ref, dtb_ref, alog_ref, dsk_ref,
                nw_ref, tri3_ref, o_ref,
                ext_ref, xs_ref, bs_ref, cs_ref, st_ref, y_ref, bt_ref, dtt_ref, wdt_ref, elb_ref):
    c = SSD_CHUNK
    halo = CONV_HALO
    l = pl.program_id(1)

    @pl.when(l == 0)
    def _():
        st_ref[...] = jnp.zeros_like(st_ref)
        ext_ref[0:halo, :] = jnp.zeros((halo, ext_ref.shape[1]), F32)

    @pl.when(l > 0)
    def _():
        ext_ref[0:halo, :] = ext_ref[c:c + halo, :]

    ext_ref[halo:halo + c, 0:SSM_DINNER] = x_ref[...]
    ext_ref[halo:halo + c, SSM_DINNER:SSM_DINNER + 1024] = bm_ref[...]
    ext_ref[halo:halo + c, SSM_DINNER + 1024:SSM_CONV_DIM] = cm_ref[...]

    for slab in range(SSM_CONV_DIM // LANES):
        cs = slice(slab * LANES, (slab + 1) * LANES)
        acc = cb_ref[:, cs] + cw_ref[0:1, cs] * ext_ref[halo - 3:halo - 3 + c, cs]
        for j in range(1, SSM_CONV):
            acc = acc + cw_ref[j:j + 1, cs] * ext_ref[halo - 3 + j:halo - 3 + j + c, cs]
        act = acc * jax.nn.sigmoid(acc)
        if slab < N_PAIRS:
            xs_ref[slab] = act
        elif slab < N_PAIRS + SSM_GROUPS:
            bs_ref[slab - N_PAIRS] = act
        else:
            cs_ref[slab - N_PAIRS - SSM_GROUPS] = act

    lane = lax.broadcasted_iota(jnp.int32, (1, LANES), 1)
    dtr = dt_ref[...] + dtb_ref[...]
    dt = jnp.maximum(dtr, 0.0) + jnp.log1p(jnp.exp(-jnp.abs(dtr)))
    a = jnp.where(lane < SSM_HEADS, -jnp.exp(alog_ref[...]), 0.0)
    la_hi, la_mid, la_lo = _split3(dt * a)
    b = jnp.dot(tri3_ref[...], jnp.concatenate([la_hi, la_mid, la_lo], axis=0),
                preferred_element_type=F32)
    b_last = b[c - 1:c]
    wd = jnp.exp(b_last - b) * dt
    bt = b.T
    bt_ref[...] = bt
    dtt_ref[...] = dt.T
    wdt_ref[...] = wd.T
    elb_ref[...] = jnp.broadcast_to(jnp.exp(bt[:, c - 1:c]), (LANES, LANES))

    row = lax.broadcasted_iota(jnp.int32, (c, c), 0)
    col = lax.broadcasted_iota(jnp.int32, (c, c), 1)
    causal = col <= row
    lo = lax.broadcasted_iota(jnp.int32, (c, LANES), 1) < SSM_HEADDIM

    def group(g, carry):
        bg = bs_ref[g]
        cg = cs_ref[g]
        cbm = lax.dot_general(cg.astype(BF16), bg.astype(BF16), _NT, preferred_element_type=F32)
        bgt = bg.T
        for j in range(PAIRS_PER_GROUP):
            p = g * PAIRS_PER_GROUP + j
            xp = xs_ref[p]
            sp = st_ref[p]
            x_lo = jnp.where(lo, xp, 0.0).astype(BF16)
            x_hi = jnp.where(lo, 0.0, xp).astype(BF16)
            s_lo = jnp.where(lo, sp, 0.0).astype(BF16)
            s_hi = jnp.where(lo, 0.0, sp).astype(BF16)
            lhs_y = []
            lhs_ce = []
            lhs_s = []
            for k in range(2):
                h = 2 * p + k
                bh = jnp.broadcast_to(bt_ref[pl.ds(h, 1), :], (c, c)).T
                rel = bh - bt_ref[pl.ds(h, 1), :]
                dec = jnp.where(causal, jnp.exp(jnp.minimum(rel, 0.0)), 0.0)
                lhs_y.append((cbm * dec * dtt_ref[pl.ds(h, 1), :]).astype(BF16))
                lhs_ce.append((cg * jnp.exp(bh)).astype(BF16))
                lhs_s.append((bgt * wdt_ref[pl.ds(h, 1), :]).astype(BF16))
            y = jnp.dot(jnp.concatenate(lhs_y + lhs_ce, axis=1),
                        jnp.concatenate([x_lo, x_hi, s_lo, s_hi], axis=0),
                        preferred_element_type=F32)
            su = jnp.dot(jnp.concatenate(lhs_s, axis=1),
                         jnp.concatenate([x_lo, x_hi], axis=0),
                         preferred_element_type=F32)
            el = jnp.where(lane < SSM_HEADDIM, elb_ref[pl.ds(2 * p, 1), :],
                           elb_ref[pl.ds(2 * p + 1, 1), :])
            st_ref[p] = sp * el + su
            y_ref[p] = y + dsk_ref[p] * xp
        return carry

    lax.fori_loop(0, SSM_GROUPS, group, 0)

    gw = PAIRS_PER_GROUP * LANES
    for g in range(SSM_GROUPS):
        cs = slice(g * gw, (g + 1) * gw)
        yg = jnp.concatenate([y_ref[g * PAIRS_PER_GROUP + j] for j in range(PAIRS_PER_GROUP)],
                             axis=1)
        zg = z_ref[:, cs]
        yg = yg * (zg * jax.nn.sigmoid(zg))
        o_ref[:, cs] = _rms(yg, nw_ref[:, cs]).astype(o_ref.dtype)


def _ssd_core(proj, conv_w, conv_b, dt_bias, a_log, d_skip, norm_w, *, batch, seq):
    c = SSD_CHUNK
    nl = seq // c
    t = batch * seq
    tri = np.tril(np.ones((c, c), np.float32))
    tri3 = jnp.asarray(np.concatenate([tri, tri, tri], axis=1), BF16)
    pad = LANES - SSM_HEADS
    dtb = jnp.pad(dt_bias.astype(F32), (0, pad)).reshape(1, LANES)
    alog = jnp.pad(a_log.astype(F32), (0, pad)).reshape(1, LANES)
    dsk = jnp.repeat(d_skip.astype(F32), SSM_HEADDIM).reshape(N_PAIRS, 1, LANES)

    def rb(b, l):
        return b * nl + l

    full2 = lambda b, l: (0, 0)
    in_specs = [
        pl.BlockSpec((c, SSM_DINNER), lambda b, l: (rb(b, l), 0)),
        pl.BlockSpec((c, SSM_DINNER), lambda b, l: (rb(b, l), 1)),
        pl.BlockSpec((c, 1024), lambda b, l: (rb(b, l), 2 * SSM_DINNER // 1024)),
        pl.BlockSpec((c, 1024), lambda b, l: (rb(b, l), 2 * SSM_DINNER // 1024 + 1)),
        pl.BlockSpec((c, LANES), lambda b, l: (rb(b, l), (SSM_DINNER + SSM_CONV_DIM) // LANES)),
        pl.BlockSpec((SSM_CONV, SSM_CONV_DIM), full2),
        pl.BlockSpec((1, SSM_CONV_DIM), full2),
        pl.BlockSpec((1, LANES), full2),
        pl.BlockSpec((1, LANES), full2),
        pl.BlockSpec((N_PAIRS, 1, LANES), lambda b, l: (0, 0, 0)),
        pl.BlockSpec((1, SSM_DINNER), full2),
        pl.BlockSpec(tri3.shape, full2),
    ]
    scratch = [
        pltpu.VMEM((CONV_HALO + c, SSM_CONV_DIM), F32),
        pltpu.VMEM((N_PAIRS, c, LANES), F32),
        pltpu.VMEM((SSM_GROUPS, c, SSM_STATE), F32),
        pltpu.VMEM((SSM_GROUPS, c, SSM_STATE), F32),
        pltpu.VMEM((N_PAIRS, SSM_STATE, LANES), F32),
        pltpu.VMEM((N_PAIRS, c, LANES), F32),
        pltpu.VMEM((LANES, c), F32),
        pltpu.VMEM((LANES, c), F32),
        pltpu.VMEM((LANES, c), F32),
        pltpu.VMEM((LANES, LANES), F32),
    ]
    return pl.pallas_call(
        _ssd_kernel,
        grid=(batch, nl),
        in_specs=in_specs,
        out_specs=pl.BlockSpec((c, SSM_DINNER), lambda b, l: (rb(b, l), 0)),
        out_shape=jax.ShapeDtypeStruct((t, SSM_DINNER), BF16),
        scratch_shapes=scratch,
        compiler_params=_cparams(("parallel", "arbitrary")),
        name="ssd_core",
    )(proj, proj, proj, proj, proj, conv_w.astype(F32), conv_b.astype(F32).reshape(1, -1),
      dtb, alog, dsk, norm_w.astype(F32).reshape(1, -1), tri3)


def _pad_cols(w, n):
    return jnp.pad(w, ((0, 0), (0, n - w.shape[1])))


def kernel(x, p, norm_mix, norm_mlp, norm_ple, norm_final, w_up, w_down, w_ple_proj, w_ple_gate,
           gla_w_in, gla_w_gk2, gla_b_gk, gla_gn, gla_w_out,
           hgrn_lb_logits, hgrn_w_in, hgrn_gn, hgrn_w_out,
           ssm_w_in, ssm_conv_w, ssm_conv_b, ssm_dt_bias, ssm_a_log, ssm_d, ssm_norm, ssm_w_out):
    batch, seq, d = x.shape
    t = batch * seq
    h = x.reshape(t, d)
    pf = p.reshape(DEPTH, t, PLE_DIM)
    row = lambda v: v.astype(F32).reshape(1, -1)

    for i in range(DEPTH):
        kind, j = i % N_MIXERS, i // N_MIXERS
        nw = row(norm_mix[i])
        if kind == 0:
            w_in = _pad_cols(gla_w_in[j], GLA_IN_COLS).astype(BF16)
            proj = _norm_matmul(h, nw, w_in, tm=1024, tn=1280)
            kd = GLA_HEADS * GLA_DK
            vd = GLA_HEADS * GLA_DV
            wgk = jnp.pad(gla_w_gk2[j], ((0, LANES - GLA_GATE_RANK), (0, 0))).astype(BF16)
            o = _gla_core(proj, mode="gla", layer=i, batch=batch, seq=seq, heads=GLA_HEADS,
                          dk=GLA_DK, dv=GLA_DV, col_q=0, col_k=kd, col_v=2 * kd,
                          col_og=2 * kd + vd,
                          extra=(2 * kd + 2 * vd, wgk, row(gla_b_gk[j])), gn=row(gla_gn[j]))
            w_out = gla_w_out[j].astype(BF16)
        elif kind == 1:
            w_in = hgrn_w_in[j].astype(BF16)
            proj = _norm_matmul(h, nw, w_in, tm=1024, tn=1024)
            fd = HGRN_HEADS * HGRN_DK
            vd = HGRN_HEADS * HGRN_DV
            o = _gla_core(proj, mode="hgrn", layer=i, batch=batch, seq=seq, heads=HGRN_HEADS,
                          dk=HGRN_DK, dv=HGRN_DV, col_q=0, col_k=fd, col_v=2 * fd,
                          col_og=2 * fd + vd,
                          extra=(hgrn_lb_logits.astype(F32),), gn=row(hgrn_gn[j]))
            w_out = hgrn_w_out[j].astype(BF16)
        else:
            w_in = _pad_cols(ssm_w_in[j], SSM_IN_COLS).astype(BF16)
            proj = _norm_matmul(h, nw, w_in, tm=512, tn=1792)
            o = _ssd_core(proj, ssm_conv_w[j], ssm_conv_b[j], ssm_dt_bias[j], ssm_a_log[j],
                          ssm_d[j], ssm_norm[j], batch=batch, seq=seq)
            w_out = ssm_w_out[j].astype(BF16)
        h = _matmul_res(o, w_out, h, tm=1024, tn=512)
        h = _mlp(h, row(norm_mlp[i]), w_up[i].astype(BF16), w_down[i].astype(BF16), tm=1024, tf=512)
        h = _ple(h, row(norm_ple[i]), w_ple_gate[i].astype(BF16), pf[i], w_ple_proj[i].astype(BF16),
                 row(norm_final), final_norm=(i == DEPTH - 1), tm=512, tn=512)
    return h.reshape(batch, seq, d)
```

```python
import functools
import math

import numpy as np
import jax
import jax.numpy as jnp
from jax import lax
from jax.experimental import pallas as pl
from jax.experimental.pallas import tpu as pltpu

F32 = jnp.float32
BF16 = jnp.bfloat16

D_MODEL = 2048
DEPTH = 4
N_MIXERS = 3
PLE_DIM = 256
D_FF = 4 * D_MODEL
EPS = 1e-6

GLA_HEADS = 4
GLA_DK = 256
GLA_DV = 512
GLA_GATE_RANK = 16
GLA_GATE_NORM = 16.0
HGRN_HEADS = 16
HGRN_DK = 128
HGRN_DV = 128
SSM_DINNER = 4096
SSM_HEADDIM = 64
SSM_HEADS = 64
SSM_GROUPS = 8
SSM_STATE = 128
SSM_CONV = 4
SSM_CONV_DIM = SSM_DINNER + 2 * SSM_GROUPS * SSM_STATE

LANES = 128
VMEM_LIMIT_BYTES = 56 * 1024 * 1024

GLA_CHUNK = 64
GLA_TOKENS_PER_STEP = 512
SSD_CHUNK = 128
CONV_HALO = 8

GLA_IN_COLS = 6400
SSM_IN_COLS = 10752
N_LEVELS = 6
LOG2E = math.log2(math.e)


def _cparams(sem):
    return pltpu.CompilerParams(dimension_semantics=sem, vmem_limit_bytes=VMEM_LIMIT_BYTES)


def _rms(x, w):
    ms = jnp.mean(x * x, axis=-1, keepdims=True)
    return x * lax.rsqrt(ms + EPS) * w


def _split3(x):
    hi = x.astype(BF16)
    r1 = x - hi.astype(F32)
    mid = r1.astype(BF16)
    lo = (r1 - mid.astype(F32)).astype(BF16)
    return hi, mid, lo


def _norm_matmul_kernel(h_ref, nw_ref, w_ref, o_ref, u_ref):
    @pl.when(pl.program_id(1) == 0)
    def _():
        u_ref[...] = _rms(h_ref[...], nw_ref[...]).astype(BF16)

    o_ref[...] = jnp.dot(u_ref[...], w_ref[...], preferred_element_type=F32).astype(o_ref.dtype)


def _norm_matmul(h, nw, w, layer, *, tm, tn):
    t, d = h.shape
    n = w.shape[2]
    return pl.pallas_call(
        _norm_matmul_kernel,
        grid=(t // tm, n // tn),
        in_specs=[
            pl.BlockSpec((tm, d), lambda i, j: (i, 0)),
            pl.BlockSpec((1, d), lambda i, j: (0, 0)),
            pl.BlockSpec((None, d, tn), lambda i, j: (layer, 0, j)),
        ],
        out_specs=pl.BlockSpec((tm, tn), lambda i, j: (i, j)),
        out_shape=jax.ShapeDtypeStruct((t, n), F32),
        scratch_shapes=[pltpu.VMEM((tm, d), BF16)],
        compiler_params=_cparams(("parallel", "arbitrary")),
        name="norm_matmul",
    )(h, nw, w)


def _matmul_res_kernel(a_ref, w_ref, h_ref, o_ref):
    o_ref[...] = h_ref[...] + jnp.dot(a_ref[...], w_ref[...], preferred_element_type=F32)


def _matmul_res(a, w, layer, h, *, tm):
    t, k = a.shape
    n = w.shape[2]
    return pl.pallas_call(
        _matmul_res_kernel,
        grid=(t // tm,),
        in_specs=[
            pl.BlockSpec((tm, k), lambda i: (i, 0)),
            pl.BlockSpec((None, k, n), lambda i: (layer, 0, 0), pipeline_mode=pl.Buffered(1)),
            pl.BlockSpec((tm, n), lambda i: (i, 0)),
        ],
        out_specs=pl.BlockSpec((tm, n), lambda i: (i, 0)),
        out_shape=jax.ShapeDtypeStruct((t, n), F32),
        compiler_params=_cparams(("parallel",)),
        name="matmul_res",
    )(a, w, h)


def _mlp_kernel(h_ref, nw_ref, wu_ref, wd_ref, o_ref, u_ref):
    @pl.when(pl.program_id(1) == 0)
    def _():
        x = h_ref[...]
        u_ref[...] = _rms(x, nw_ref[...]).astype(BF16)
        o_ref[...] = x

    a = jnp.dot(u_ref[...], wu_ref[...], preferred_element_type=F32)
    a = jnp.square(jnp.maximum(a, 0.0)).astype(BF16)
    o_ref[...] += jnp.dot(a, wd_ref[...], preferred_element_type=F32)


def _mlp(h, nw, wu, wd, layer, *, tm, tf):
    t, d = h.shape
    ff = wu.shape[2]
    return pl.pallas_call(
        _mlp_kernel,
        grid=(t // tm, ff // tf),
        in_specs=[
            pl.BlockSpec((tm, d), lambda i, f: (i, 0), pipeline_mode=pl.Buffered(1)),
            pl.BlockSpec((1, d), lambda i, f: (0, 0)),
            pl.BlockSpec((None, d, tf), lambda i, f: (layer, 0, f)),
            pl.BlockSpec((None, tf, d), lambda i, f: (layer, f, 0)),
        ],
        out_specs=pl.BlockSpec((tm, d), lambda i, f: (i, 0)),
        out_shape=jax.ShapeDtypeStruct((t, d), F32),
        scratch_shapes=[pltpu.VMEM((tm, d), BF16)],
        compiler_params=_cparams(("parallel", "arbitrary")),
        name="mlp",
    )(h, nw, wu, wd)


def _ple_kernel(h_ref, nw_ref, wg_ref, p_ref, wp_ref, nf_ref, o_ref, *, final_norm, tn):
    x = h_ref[...]
    u = _rms(x, nw_ref[...]).astype(BF16)
    pb = p_ref[...].astype(BF16)
    d = x.shape[1]
    for n0 in range(0, d, tn):
        cs = slice(n0, n0 + tn)
        gate = jax.nn.sigmoid(jnp.dot(u, wg_ref[:, cs], preferred_element_type=F32))
        proj = jnp.dot(pb, wp_ref[:, cs], preferred_element_type=F32)
        o_ref[:, cs] = h_ref[:, cs] + gate * proj
    if final_norm:
        o_ref[...] = _rms(o_ref[...], nf_ref[...])


def _ple(h, nw, wg, p, wp, nf, layer, *, final_norm, tm, tn):
    t, d = h.shape
    pd = p.shape[2]
    return pl.pallas_call(
        functools.partial(_ple_kernel, final_norm=final_norm, tn=tn),
        grid=(t // tm,),
        in_specs=[
            pl.BlockSpec((tm, d), lambda i: (i, 0)),
            pl.BlockSpec((1, d), lambda i: (0, 0)),
            pl.BlockSpec((None, d, d), lambda i: (layer, 0, 0)),
            pl.BlockSpec((None, tm, pd), lambda i: (layer, i, 0)),
            pl.BlockSpec((None, pd, d), lambda i: (layer, 0, 0)),
            pl.BlockSpec((1, d), lambda i: (0, 0)),
        ],
        out_specs=pl.BlockSpec((tm, d), lambda i: (i, 0)),
        out_shape=jax.ShapeDtypeStruct((t, d), F32),
        compiler_params=_cparams(("parallel",)),
        name="ple",
    )(h, nw, wg, p, wp, nf)


def _gla_arg_matrix():
    c = GLA_CHUNK
    m = np.zeros((2 + N_LEVELS, c, c), np.float32)
    t = np.arange(c)[:, None]
    r = np.arange(c)[None, :]
    m[0] = (r <= t)
    m[1] = (r > t)
    for lvl in range(N_LEVELS):
        n = c >> (lvl + 1)
        ref = (t // (2 * n)) * (2 * n) + n - 1
        lower = t > ref
        m[2 + lvl] = np.where(lower, (r > ref) & (r <= t), (r > t) & (r <= ref))
    m = m.reshape((2 + N_LEVELS) * c, c)
    return np.concatenate([m, m, m, np.zeros_like(m)], axis=1)


def _gla_masks():
    c = GLA_CHUNK
    t = np.arange(c)[:, None]
    s = np.arange(c)[None, :]
    masks = np.zeros((1 + N_LEVELS, c, c), np.float32)
    masks[0] = (t == s)
    for lvl in range(N_LEVELS):
        n = c >> (lvl + 1)
        same = (t // (2 * n)) == (s // (2 * n))
        masks[1 + lvl] = same & ((t % (2 * n)) >= n) & ((s % (2 * n)) < n)
    return masks


_NT = (((1,), (1,)), ((), ()))
_TN = (((0,), (0,)), ((), ()))


def _gla_core_kernel(*refs, mode, layer, scale, heads, dk, dv):
    if mode == "gla":
        (q_ref, k_ref, v_ref, og_ref, glr_ref, wgk_ref, bgk_ref, gn_ref, mall_ref, masks_ref,
         o_ref, st_ref) = refs
    else:
        (q_ref, k_ref, v_ref, og_ref, lbl_ref, gn_ref, mall_ref, masks_ref, o_ref, st_ref) = refs

    @pl.when(pl.program_id(2) == 0)
    def _():
        st_ref[...] = jnp.zeros_like(st_ref)

    if mode == "hgrn":
        lg = lbl_ref[...]
        e = jnp.exp(lg - jnp.max(lg, axis=0, keepdims=True))
        sm = e / jnp.sum(e, axis=0, keepdims=True)
        lb_all = jnp.zeros_like(sm[0:1])
        for r in range(1, layer + 1):
            lb_all = lb_all + sm[r:r + 1]

    c = GLA_CHUNK
    n_chunks = q_ref.shape[0] // c

    def chunk(ci, carry):
        rows = pl.ds(pl.multiple_of(ci * c, c), c)
        if mode == "gla":
            z_all = jnp.dot(glr_ref[rows, :].astype(BF16), wgk_ref[...],
                            preferred_element_type=F32) + bgk_ref[...]
        for g in range(heads):
            ks = slice(g * dk, (g + 1) * dk)
            vs = slice(g * dv, (g + 1) * dv)
            vc = v_ref[rows, vs]
            og = og_ref[rows, vs]
            if mode == "gla":
                qc = q_ref[rows, ks] * scale
                kc = k_ref[rows, ks]
                z = z_all[:, ks]
                gc = (jnp.minimum(z, 0.0) - jnp.log1p(jnp.exp(-jnp.abs(z)))) * (LOG2E / GLA_GATE_NORM)
                gate = og * jax.nn.sigmoid(og)
            else:
                qq = q_ref[rows, ks]
                qc = qq * jax.nn.sigmoid(qq) * scale
                lb = lb_all[:, ks]
                f = lb + (1.0 - lb) * jax.nn.sigmoid(k_ref[rows, ks])
                kc = 1.0 - f
                gc = jnp.log(f) * LOG2E
                gate = jax.nn.sigmoid(og)

            g_hi, g_mid, g_lo = _split3(gc)
            gs = jnp.concatenate([g_hi, g_mid, g_lo, jnp.zeros_like(g_hi)], axis=0)
            ex = jnp.exp2(jnp.dot(mall_ref[...], gs, preferred_element_type=F32))

            st = st_ref[g]
            q_in = (qc * ex[0:c]).astype(BF16)
            o = lax.dot_general(q_in, st.astype(BF16), _NT, preferred_element_type=F32)
            k_dec = (kc * ex[c:2 * c]).astype(BF16)

            att = masks_ref[0] * lax.dot_general(qc.astype(BF16), kc.astype(BF16), _NT,
                                                 preferred_element_type=F32)
            for lvl in range(N_LEVELS):
                el = ex[(2 + lvl) * c:(3 + lvl) * c]
                ql = (qc * el).astype(BF16)
                kl = (kc * el).astype(BF16)
                att = att + masks_ref[1 + lvl] * lax.dot_general(ql, kl, _NT,
                                                                 preferred_element_type=F32)
            vb = vc.astype(BF16)
            o = o + jnp.dot(att.astype(BF16), vb, preferred_element_type=F32)

            e_last = ex[c - 1:c]
            st_ref[g] = st * e_last + lax.dot_general(vb, k_dec, _TN, preferred_element_type=F32)

            o_ref[rows, vs] = (_rms(o, gn_ref[...]) * gate).astype(o_ref.dtype)
        return carry

    lax.fori_loop(0, n_chunks, chunk, 0)


def _gla_core(proj, *, mode, layer, batch, seq, heads, hps, dk, dv, col_q, col_k, col_v, col_og,
              extra, gn):
    tl = GLA_TOKENS_PER_STEP
    nl = seq // tl
    t = batch * seq
    wk = hps * dk
    wv = hps * dv
    mall = jnp.asarray(_gla_arg_matrix(), BF16)
    masks = jnp.asarray(_gla_masks(), F32)

    def rowblk(b, h, l):
        return b * nl + l

    in_specs = [
        pl.BlockSpec((tl, wk), lambda b, h, l: (rowblk(b, h, l), col_q // wk + h)),
        pl.BlockSpec((tl, wk), lambda b, h, l: (rowblk(b, h, l), col_k // wk + h)),
        pl.BlockSpec((tl, wv), lambda b, h, l: (rowblk(b, h, l), col_v // wv + h)),
        pl.BlockSpec((tl, wv), lambda b, h, l: (rowblk(b, h, l), col_og // wv + h)),
    ]
    args = [proj, proj, proj, proj]
    if mode == "gla":
        col_glr, wgk, bgk = extra
        in_specs += [
            pl.BlockSpec((tl, LANES), lambda b, h, l: (rowblk(b, h, l), col_glr // LANES)),
            pl.BlockSpec((LANES, wk), lambda b, h, l: (0, h)),
            pl.BlockSpec((1, wk), lambda b, h, l: (0, h)),
        ]
        args += [proj, wgk, bgk]
    else:
        (lbl,) = extra
        in_specs += [pl.BlockSpec((DEPTH, wk), lambda b, h, l: (0, h))]
        args += [lbl]
    in_specs += [
        pl.BlockSpec((1, dv), lambda b, h, l: (0, 0)),
        pl.BlockSpec(mall.shape, lambda b, h, l: (0, 0)),
        pl.BlockSpec(masks.shape, lambda b, h, l: (0, 0, 0)),
    ]
    args += [gn, mall, masks]
    return pl.pallas_call(
        functools.partial(_gla_core_kernel, mode=mode, layer=layer, scale=dk ** -0.5, heads=hps,
                          dk=dk, dv=dv),
        grid=(batch, heads // hps, nl),
        in_specs=in_specs,
        out_specs=pl.BlockSpec((tl, wv), lambda b, h, l: (rowblk(b, h, l), h)),
        out_shape=jax.ShapeDtypeStruct((t, heads * dv), BF16),
        scratch_shapes=[pltpu.VMEM((hps, dv, dk), F32)],
        compiler_params=_cparams(("parallel", "parallel", "arbitrary")),
        name=mode + "_core",
    )(*args)


N_PAIRS = SSM_HEADS // 2
PAIRS_PER_GROUP = N_PAIRS // SSM_GROUPS


def _ssd_kernel(z_ref, x_ref, bm_ref, cm_ref, dt_ref, cw_ref, cb_ref, dtb_ref, alog_ref, dsk_ref,
                nw_ref, tri3_ref, o_ref,
                ext_ref, xs_ref, bs_ref, cs_ref, st_ref, y_ref, bt_ref, dtt_ref, wdt_ref, elb_ref):
    c = SSD_CHUNK
    halo = CONV_HALO
    l = pl.program_id(1)

    @pl.when(l == 0)
    def _():
        st_ref[...] = jnp.zeros_like(st_ref)
        ext_ref[0:halo, :] = jnp.zeros((halo, ext_ref.shape[1]), F32)

    @pl.when(l > 0)
    def _():
        ext_ref[0:halo, :] = ext_ref[c:c + halo, :]

    ext_ref[halo:halo + c, 0:SSM_DINNER] = x_ref[...]
    ext_ref[halo:halo + c, SSM_DINNER:SSM_DINNER + 1024] = bm_ref[...]
    ext_ref[halo:halo + c, SSM_DINNER + 1024:SSM_CONV_DIM] = cm_ref[...]

    for slab in range(SSM_CONV_DIM // LANES):
        cs = slice(slab * LANES, (slab + 1) * LANES)
        acc = cb_ref[:, cs] + cw_ref[0:1, cs] * ext_ref[halo - 3:halo - 3 + c, cs]
        for j in range(1, SSM_CONV):
            acc = acc + cw_ref[j:j + 1, cs] * ext_ref[halo - 3 + j:halo - 3 + j + c, cs]
        act = acc * jax.nn.sigmoid(acc)
        if slab < N_PAIRS:
            xs_ref[slab] = act
        elif slab < N_PAIRS + SSM_GROUPS:
            bs_ref[slab - N_PAIRS] = act
        else:
            cs_ref[slab - N_PAIRS - SSM_GROUPS] = act

    lane = lax.broadcasted_iota(jnp.int32, (1, LANES), 1)
    dtr = dt_ref[...] + dtb_ref[...]
    dt = jnp.maximum(dtr, 0.0) + jnp.log1p(jnp.exp(-jnp.abs(dtr)))
    a = jnp.where(lane < SSM_HEADS, -jnp.exp(alog_ref[...]), 0.0)
    la_hi, la_mid, la_lo = _split3(dt * (a * LOG2E))
    b = jnp.dot(tri3_ref[...], jnp.concatenate([la_hi, la_mid, la_lo], axis=0),
                preferred_element_type=F32)
    b_last = b[c - 1:c]
    wd = jnp.exp2(b_last - b) * dt
    bt = b.T
    bt_ref[...] = bt
    dtt_ref[...] = dt.T
    wdt_ref[...] = wd.T
    elb_ref[...] = jnp.broadcast_to(jnp.exp2(bt[:, c - 1:c]), (LANES, LANES))

    row = lax.broadcasted_iota(jnp.int32, (c, c), 0)
    col = lax.broadcasted_iota(jnp.int32, (c, c), 1)
    causal = col <= row
    lo = lax.broadcasted_iota(jnp.int32, (c, LANES), 1) < SSM_HEADDIM

    def group(g, carry):
        bg = bs_ref[g]
        cg = cs_ref[g]
        cbm = lax.dot_general(cg.astype(BF16), bg.astype(BF16), _NT, preferred_element_type=F32)
        bgt = bg.T
        for j in range(PAIRS_PER_GROUP):
            p = g * PAIRS_PER_GROUP + j
            xp = xs_ref[p]
            sp = st_ref[p]
            x_lo = jnp.where(lo, xp, 0.0).astype(BF16)
            x_hi = jnp.where(lo, 0.0, xp).astype(BF16)
            s_lo = jnp.where(lo, sp, 0.0).astype(BF16)
            s_hi = jnp.where(lo, 0.0, sp).astype(BF16)
            lhs_y = []
            lhs_ce = []
            lhs_s = []
            for k in range(2):
                h = 2 * p + k
                bh = jnp.broadcast_to(bt_ref[pl.ds(h, 1), :], (c, c)).T
                rel = bh - bt_ref[pl.ds(h, 1), :]
                dec = jnp.where(causal, jnp.exp2(jnp.minimum(rel, 0.0)), 0.0)
                lhs_y.append((cbm * dec * dtt_ref[pl.ds(h, 1), :]).astype(BF16))
                lhs_ce.append((cg * jnp.exp2(bh)).astype(BF16))
                lhs_s.append((bgt * wdt_ref[pl.ds(h, 1), :]).astype(BF16))
            y = jnp.dot(jnp.concatenate(lhs_y + lhs_ce, axis=1),
                        jnp.concatenate([x_lo, x_hi, s_lo, s_hi], axis=0),
                        preferred_element_type=F32)
            su = jnp.dot(jnp.concatenate(lhs_s, axis=1),
                         jnp.concatenate([x_lo, x_hi], axis=0),
                         preferred_element_type=F32)
            el = jnp.where(lane < SSM_HEADDIM, elb_ref[pl.ds(2 * p, 1), :],
                           elb_ref[pl.ds(2 * p + 1, 1), :])
            st_ref[p] = sp * el + su
            y_ref[p] = y + dsk_ref[p] * xp
        return carry

    lax.fori_loop(0, SSM_GROUPS, group, 0)

    gw = PAIRS_PER_GROUP * LANES
    for g in range(SSM_GROUPS):
        cs = slice(g * gw, (g + 1) * gw)
        yg = jnp.concatenate([y_ref[g * PAIRS_PER_GROUP + j] for j in range(PAIRS_PER_GROUP)],
                             axis=1)
        zg = z_ref[:, cs]
        yg = yg * (zg * jax.nn.sigmoid(zg))
        o_ref[:, cs] = _rms(yg, nw_ref[:, cs]).astype(o_ref.dtype)


def _ssd_core(proj, conv_w, conv_b, dt_bias, a_log, d_skip, norm_w, *, batch, seq):
    c = SSD_CHUNK
    nl = seq // c
    t = batch * seq
    tri = np.tril(np.ones((c, c), np.float32))
    tri3 = jnp.asarray(np.concatenate([tri, tri, tri], axis=1), BF16)
    pad = LANES - SSM_HEADS
    dtb = jnp.pad(dt_bias.astype(F32), (0, pad)).reshape(1, LANES)
    alog = jnp.pad(a_log.astype(F32), (0, pad)).reshape(1, LANES)
    dsk = jnp.repeat(d_skip.astype(F32), SSM_HEADDIM).reshape(N_PAIRS, 1, LANES)

    def rb(b, l):
        return b * nl + l

    full2 = lambda b, l: (0, 0)
    in_specs = [
        pl.BlockSpec((c, SSM_DINNER), lambda b, l: (rb(b, l), 0)),
        pl.BlockSpec((c, SSM_DINNER), lambda b, l: (rb(b, l), 1)),
        pl.BlockSpec((c, 1024), lambda b, l: (rb(b, l), 2 * SSM_DINNER // 1024)),
        pl.BlockSpec((c, 1024), lambda b, l: (rb(b, l), 2 * SSM_DINNER // 1024 + 1)),
        pl.BlockSpec((c, LANES), lambda b, l: (rb(b, l), (SSM_DINNER + SSM_CONV_DIM) // LANES)),
        pl.BlockSpec((SSM_CONV, SSM_CONV_DIM), full2),
        pl.BlockSpec((1, SSM_CONV_DIM), full2),
        pl.BlockSpec((1, LANES), full2),
        pl.BlockSpec((1, LANES), full2),
        pl.BlockSpec((N_PAIRS, 1, LANES), lambda b, l: (0, 0, 0)),
        pl.BlockSpec((1, SSM_DINNER), full2),
        pl.BlockSpec(tri3.shape, full2),
    ]
    scratch = [
        pltpu.VMEM((CONV_HALO + c, SSM_CONV_DIM), F32),
        pltpu.VMEM((N_PAIRS, c, LANES), F32),
        pltpu.VMEM((SSM_GROUPS, c, SSM_STATE), F32),
        pltpu.VMEM((SSM_GROUPS, c, SSM_STATE), F32),
        pltpu.VMEM((N_PAIRS, SSM_STATE, LANES), F32),
        pltpu.VMEM((N_PAIRS, c, LANES), F32),
        pltpu.VMEM((LANES, c), F32),
        pltpu.VMEM((LANES, c), F32),
        pltpu.VMEM((LANES, c), F32),
        pltpu.VMEM((LANES, LANES), F32),
    ]
    return pl.pallas_call(
        _ssd_kernel,
        grid=(batch, nl),
        in_specs=in_specs,
        out_specs=pl.BlockSpec((c, SSM_DINNER), lambda b, l: (rb(b, l), 0)),
        out_shape=jax.ShapeDtypeStruct((t, SSM_DINNER), BF16),
        scratch_shapes=scratch,
        compiler_params=_cparams(("parallel", "arbitrary")),
        name="ssd_core",
    )(proj, proj, proj, proj, proj, conv_w.astype(F32), conv_b.astype(F32).reshape(1, -1),
      dtb, alog, dsk, norm_w.astype(F32).reshape(1, -1), tri3)


def _pad_last(w, n):
    return jnp.pad(w, [(0, 0)] * (w.ndim - 1) + [(0, n - w.shape[-1])])


def kernel(x, p, norm_mix, norm_mlp, norm_ple, norm_final, w_up, w_down, w_ple_proj, w_ple_gate,
           gla_w_in, gla_w_gk2, gla_b_gk, gla_gn, gla_w_out,
           hgrn_lb_logits, hgrn_w_in, hgrn_gn, hgrn_w_out,
           ssm_w_in, ssm_conv_w, ssm_conv_b, ssm_dt_bias, ssm_a_log, ssm_d, ssm_norm, ssm_w_out):
    batch, seq, d = x.shape
    t = batch * seq
    h = x.reshape(t, d)
    pf = p.reshape(DEPTH, t, PLE_DIM)
    row = lambda v: v.astype(F32).reshape(1, -1)

    gla_w_in_b = _pad_last(gla_w_in, GLA_IN_COLS).astype(BF16)
    hgrn_w_in_b = hgrn_w_in.astype(BF16)
    ssm_w_in_b = _pad_last(ssm_w_in, SSM_IN_COLS).astype(BF16)
    gla_w_out_b = gla_w_out.astype(BF16)
    hgrn_w_out_b = hgrn_w_out.astype(BF16)
    ssm_w_out_b = ssm_w_out.astype(BF16)
    w_up_b = w_up.astype(BF16)
    w_down_b = w_down.astype(BF16)
    w_gate_b = w_ple_gate.astype(BF16)
    w_proj_b = w_ple_proj.astype(BF16)

    for i in range(DEPTH):
        kind, j = i % N_MIXERS, i // N_MIXERS
        nw = row(norm_mix[i])
        if kind == 0:
            proj = _norm_matmul(h, nw, gla_w_in_b, j, tm=1024, tn=1280)
            kd = GLA_HEADS * GLA_DK
            vd = GLA_HEADS * GLA_DV
            wgk = jnp.pad(gla_w_gk2[j], ((0, LANES - GLA_GATE_RANK), (0, 0))).astype(BF16)
            o = _gla_core(proj, mode="gla", layer=i, batch=batch, seq=seq, heads=GLA_HEADS,
                          hps=GLA_HEADS, dk=GLA_DK, dv=GLA_DV, col_q=0, col_k=kd, col_v=2 * kd,
                          col_og=2 * kd + vd,
                          extra=(2 * kd + 2 * vd, wgk, row(gla_b_gk[j])), gn=row(gla_gn[j]))
            h = _matmul_res(o, gla_w_out_b, j, h, tm=512)
        elif kind == 1:
            proj = _norm_matmul(h, nw, hgrn_w_in_b, j, tm=1024, tn=1024)
            fd = HGRN_HEADS * HGRN_DK
            vd = HGRN_HEADS * HGRN_DV
            o = _gla_core(proj, mode="hgrn", layer=i, batch=batch, seq=seq, heads=HGRN_HEADS,
                          hps=HGRN_HEADS // 2, dk=HGRN_DK, dv=HGRN_DV, col_q=0, col_k=fd,
                          col_v=2 * fd, col_og=2 * fd + vd,
                          extra=(hgrn_lb_logits.astype(F32),), gn=row(hgrn_gn[j]))
            h = _matmul_res(o, hgrn_w_out_b, j, h, tm=512)
        else:
            proj = _norm_matmul(h, nw, ssm_w_in_b, j, tm=512, tn=1792)
            o = _ssd_core(proj, ssm_conv_w[j], ssm_conv_b[j], ssm_dt_bias[j], ssm_a_log[j],
                          ssm_d[j], ssm_norm[j], batch=batch, seq=seq)
            h = _matmul_res(o, ssm_w_out_b, j, h, tm=512)
        h = _mlp(h, row(norm_mlp[i]), w_up_b, w_down_b, i, tm=1024, tf=1024)
        h = _ple(h, row(norm_ple[i]), w_gate_b, pf, w_proj_b, row(norm_final), i,
                 final_norm=(i == DEPTH - 1), tm=512, tn=512)
    return h.reshape(batch, seq, d)
```

```python
import functools
import math

import numpy as np
import jax
import jax.numpy as jnp
from jax import lax
from jax.experimental import pallas as pl
from jax.experimental.pallas import tpu as pltpu

F32 = jnp.float32
BF16 = jnp.bfloat16

D_MODEL = 2048
DEPTH = 4
N_MIXERS = 3
PLE_DIM = 256
D_FF = 4 * D_MODEL
EPS = 1e-6

GLA_HEADS = 4
GLA_DK = 256
GLA_DV = 512
GLA_GATE_RANK = 16
GLA_GATE_NORM = 16.0
HGRN_HEADS = 16
HGRN_DK = 128
HGRN_DV = 128
SSM_DINNER = 4096
SSM_HEADDIM = 64
SSM_HEADS = 64
SSM_GROUPS = 8
SSM_STATE = 128
SSM_CONV = 4
SSM_CONV_DIM = SSM_DINNER + 2 * SSM_GROUPS * SSM_STATE

LANES = 128
VMEM_LIMIT_BYTES = 56 * 1024 * 1024

GLA_CHUNK = 64
GLA_TOKENS_PER_STEP = 512
SSD_CHUNK = 128
CONV_HALO = 8

GLA_IN_COLS = 6400
SSM_IN_COLS = 10752
N_LEVELS = 6
LOG2E = math.log2(math.e)
MASKED_LOG2 = -1e30


def _cparams(sem):
    return pltpu.CompilerParams(dimension_semantics=sem, vmem_limit_bytes=VMEM_LIMIT_BYTES)


def _rms(x, w):
    ms = jnp.mean(x * x, axis=-1, keepdims=True)
    return x * lax.rsqrt(ms + EPS) * w


def _split3(x):
    hi = x.astype(BF16)
    r1 = x - hi.astype(F32)
    mid = r1.astype(BF16)
    lo = (r1 - mid.astype(F32)).astype(BF16)
    return hi, mid, lo


def _norm_matmul_kernel(h_ref, nw_ref, w_ref, o_ref, u_ref):
    @pl.when(pl.program_id(1) == 0)
    def _():
        u_ref[...] = _rms(h_ref[...], nw_ref[...]).astype(BF16)

    o_ref[...] = jnp.dot(u_ref[...], w_ref[...], preferred_element_type=F32).astype(o_ref.dtype)


def _norm_matmul(h, nw, w, layer, *, tm, tn):
    t, d = h.shape
    n = w.shape[2]
    return pl.pallas_call(
        _norm_matmul_kernel,
        grid=(t // tm, n // tn),
        in_specs=[
            pl.BlockSpec((tm, d), lambda i, j: (i, 0)),
            pl.BlockSpec((1, d), lambda i, j: (0, 0)),
            pl.BlockSpec((None, d, tn), lambda i, j: (layer, 0, j)),
        ],
        out_specs=pl.BlockSpec((tm, tn), lambda i, j: (i, j)),
        out_shape=jax.ShapeDtypeStruct((t, n), F32),
        scratch_shapes=[pltpu.VMEM((tm, d), BF16)],
        compiler_params=_cparams(("parallel", "arbitrary")),
        name="norm_matmul",
    )(h, nw, w)


def _matmul_res_kernel(a_ref, w_ref, h_ref, o_ref):
    o_ref[...] = h_ref[...] + jnp.dot(a_ref[...], w_ref[...], preferred_element_type=F32)


def _matmul_res(a, w, layer, h, *, tm):
    t, k = a.shape
    n = w.shape[2]
    return pl.pallas_call(
        _matmul_res_kernel,
        grid=(t // tm,),
        in_specs=[
            pl.BlockSpec((tm, k), lambda i: (i, 0)),
            pl.BlockSpec((None, k, n), lambda i: (layer, 0, 0), pipeline_mode=pl.Buffered(1)),
            pl.BlockSpec((tm, n), lambda i: (i, 0)),
        ],
        out_specs=pl.BlockSpec((tm, n), lambda i: (i, 0)),
        out_shape=jax.ShapeDtypeStruct((t, n), F32),
        compiler_params=_cparams(("parallel",)),
        name="matmul_res",
    )(a, w, h)


def _mlp_kernel(h_ref, nw_ref, wu_ref, wd_ref, o_ref, u_ref):
    @pl.when(pl.program_id(1) == 0)
    def _():
        x = h_ref[...]
        u_ref[...] = _rms(x, nw_ref[...]).astype(BF16)
        o_ref[...] = x

    a = jnp.dot(u_ref[...], wu_ref[...], preferred_element_type=F32)
    a = jnp.square(jnp.maximum(a, 0.0)).astype(BF16)
    o_ref[...] += jnp.dot(a, wd_ref[...], preferred_element_type=F32)


def _mlp(h, nw, wu, wd, layer, *, tm, tf):
    t, d = h.shape
    ff = wu.shape[2]
    return pl.pallas_call(
        _mlp_kernel,
        grid=(t // tm, ff // tf),
        in_specs=[
            pl.BlockSpec((tm, d), lambda i, f: (i, 0)),
            pl.BlockSpec((1, d), lambda i, f: (0, 0)),
            pl.BlockSpec((None, d, tf), lambda i, f: (layer, 0, f)),
            pl.BlockSpec((None, tf, d), lambda i, f: (layer, f, 0)),
        ],
        out_specs=pl.BlockSpec((tm, d), lambda i, f: (i, 0)),
        out_shape=jax.ShapeDtypeStruct((t, d), F32),
        scratch_shapes=[pltpu.VMEM((tm, d), BF16)],
        compiler_params=_cparams(("parallel", "arbitrary")),
        name="mlp",
    )(h, nw, wu, wd)


def _ple_kernel(h_ref, nw_ref, wg_ref, p_ref, wp_ref, nf_ref, o_ref, *, final_norm, tn):
    x = h_ref[...]
    u = _rms(x, nw_ref[...]).astype(BF16)
    pb = p_ref[...].astype(BF16)
    d = x.shape[1]
    for n0 in range(0, d, tn):
        cs = slice(n0, n0 + tn)
        gate = jax.nn.sigmoid(jnp.dot(u, wg_ref[:, cs], preferred_element_type=F32))
        proj = jnp.dot(pb, wp_ref[:, cs], preferred_element_type=F32)
        o_ref[:, cs] = h_ref[:, cs] + gate * proj
    if final_norm:
        o_ref[...] = _rms(o_ref[...], nf_ref[...])


def _ple(h, nw, wg, p, wp, nf, layer, *, final_norm, tm, tn):
    t, d = h.shape
    pd = p.shape[2]
    return pl.pallas_call(
        functools.partial(_ple_kernel, final_norm=final_norm, tn=tn),
        grid=(t // tm,),
        in_specs=[
            pl.BlockSpec((tm, d), lambda i: (i, 0)),
            pl.BlockSpec((1, d), lambda i: (0, 0)),
            pl.BlockSpec((None, d, d), lambda i: (layer, 0, 0)),
            pl.BlockSpec((None, tm, pd), lambda i: (layer, i, 0)),
            pl.BlockSpec((None, pd, d), lambda i: (layer, 0, 0)),
            pl.BlockSpec((1, d), lambda i: (0, 0)),
        ],
        out_specs=pl.BlockSpec((tm, d), lambda i: (i, 0)),
        out_shape=jax.ShapeDtypeStruct((t, d), F32),
        compiler_params=_cparams(("parallel",)),
        name="ple",
    )(h, nw, wg, p, wp, nf)


def _gla_arg_matrix():
    c = GLA_CHUNK
    m = np.zeros((2 + N_LEVELS, c, c), np.float32)
    t = np.arange(c)[:, None]
    r = np.arange(c)[None, :]
    m[0] = (r <= t)
    m[1] = (r > t)
    for lvl in range(N_LEVELS):
        n = c >> (lvl + 1)
        ref = (t // (2 * n)) * (2 * n) + n - 1
        lower = t > ref
        m[2 + lvl] = np.where(lower, (r > ref) & (r <= t), (r > t) & (r <= ref))
    m = m.reshape((2 + N_LEVELS) * c, c)
    return np.concatenate([m, m, m, np.zeros_like(m)], axis=1)


def _gla_masks(pack):
    c = GLA_CHUNK
    t = np.arange(c)[:, None]
    s = np.arange(c)[None, :]
    masks = np.zeros((1 + N_LEVELS, c, c), np.float32)
    masks[0] = (t == s)
    for lvl in range(N_LEVELS):
        n = c >> (lvl + 1)
        same = (t // (2 * n)) == (s // (2 * n))
        masks[1 + lvl] = same & ((t % (2 * n)) >= n) & ((s % (2 * n)) < n)
    return np.tile(masks, (1, 1, pack))


def _block_diag_rows(x, pack, width):
    if pack == 1:
        return x
    head = lax.broadcasted_iota(jnp.int32, x.shape, 1) // width
    return jnp.concatenate([jnp.where(head == p, x, jnp.zeros_like(x)) for p in range(pack)], axis=0)


_NT = (((1,), (1,)), ((), ()))
_TN = (((0,), (0,)), ((), ()))


def _gla_core_kernel(*refs, mode, layer, scale, packs, pack, dk, dv):
    wk = pack * dk
    wv = pack * dv
    if mode == "gla":
        (q_ref, k_ref, v_ref, og_ref, glr_ref, wgk_ref, bgk_ref, gn_ref, mall_ref, masks_ref,
         o_ref, st_ref) = refs
    else:
        (q_ref, k_ref, v_ref, og_ref, lbl_ref, gn_ref, mall_ref, masks_ref, o_ref, st_ref) = refs

    @pl.when(pl.program_id(2) == 0)
    def _():
        st_ref[...] = jnp.zeros_like(st_ref)

    if mode == "hgrn":
        lg = lbl_ref[...]
        e = jnp.exp(lg - jnp.max(lg, axis=0, keepdims=True))
        sm = e / jnp.sum(e, axis=0, keepdims=True)
        lb_all = jnp.zeros_like(sm[0:1])
        for r in range(1, layer + 1):
            lb_all = lb_all + sm[r:r + 1]

    c = GLA_CHUNK
    n_chunks = q_ref.shape[0] // c
    if pack > 1:
        st_diag = (lax.broadcasted_iota(jnp.int32, (wv, wk), 0) // dv
                   == lax.broadcasted_iota(jnp.int32, (wv, wk), 1) // dk)

    def chunk(ci, carry):
        rows = pl.ds(pl.multiple_of(ci * c, c), c)
        if mode == "gla":
            z_all = jnp.dot(glr_ref[rows, :].astype(BF16), wgk_ref[...],
                            preferred_element_type=F32) + bgk_ref[...]
        for g in range(packs):
            ks = slice(g * wk, (g + 1) * wk)
            vs = slice(g * wv, (g + 1) * wv)
            vc = v_ref[rows, vs]
            og = og_ref[rows, vs]
            if mode == "gla":
                qc = q_ref[rows, ks] * scale
                kc = k_ref[rows, ks]
                z = z_all[:, ks]
                gc = (jnp.minimum(z, 0.0) - jnp.log1p(jnp.exp(-jnp.abs(z)))) * (LOG2E / GLA_GATE_NORM)
                gate = og * jax.nn.sigmoid(og)
            else:
                qq = q_ref[rows, ks]
                qc = qq * jax.nn.sigmoid(qq) * scale
                lb = lb_all[:, ks]
                f = lb + (1.0 - lb) * jax.nn.sigmoid(k_ref[rows, ks])
                kc = 1.0 - f
                gc = jnp.log(f) * LOG2E
                gate = jax.nn.sigmoid(og)

            g_hi, g_mid, g_lo = _split3(gc)
            gs = jnp.concatenate([g_hi, g_mid, g_lo, jnp.zeros_like(g_hi)], axis=0)
            ex = jnp.exp2(jnp.dot(mall_ref[...], gs, preferred_element_type=F32))

            st = st_ref[g]
            q_in = (qc * ex[0:c]).astype(BF16)
            o = lax.dot_general(q_in, st.astype(BF16), _NT, preferred_element_type=F32)
            k_dec = (kc * ex[c:2 * c]).astype(BF16)

            att = masks_ref[0] * lax.dot_general(
                qc.astype(BF16), _block_diag_rows(kc.astype(BF16), pack, dk), _NT,
                preferred_element_type=F32)
            for lvl in range(N_LEVELS):
                el = ex[(2 + lvl) * c:(3 + lvl) * c]
                ql = (qc * el).astype(BF16)
                kl = _block_diag_rows((kc * el).astype(BF16), pack, dk)
                att = att + masks_ref[1 + lvl] * lax.dot_general(ql, kl, _NT,
                                                                 preferred_element_type=F32)
            vb = vc.astype(BF16)
            o = o + jnp.dot(att.astype(BF16), _block_diag_rows(vb, pack, dv),
                            preferred_element_type=F32)

            e_last = ex[c - 1:c]
            upd = lax.dot_general(vb, k_dec, _TN, preferred_element_type=F32)
            if pack > 1:
                upd = jnp.where(st_diag, upd, 0.0)
            st_ref[g] = st * e_last + upd

            for p in range(pack):
                hs = slice(p * dv, (p + 1) * dv)
                o_ref[rows, g * wv + p * dv:g * wv + (p + 1) * dv] = (
                    _rms(o[:, hs], gn_ref[...]) * gate[:, hs]).astype(o_ref.dtype)
        return carry

    lax.fori_loop(0, n_chunks, chunk, 0)


def _gla_core(proj, *, mode, layer, batch, seq, heads, hps, pack, dk, dv, col_q, col_k, col_v,
              col_og, extra, gn):
    tl = GLA_TOKENS_PER_STEP
    nl = seq // tl
    t = batch * seq
    wk = hps * dk
    wv = hps * dv
    mall = jnp.asarray(_gla_arg_matrix(), BF16)
    masks = jnp.asarray(_gla_masks(pack), F32)

    def rowblk(b, h, l):
        return b * nl + l

    in_specs = [
        pl.BlockSpec((tl, wk), lambda b, h, l: (rowblk(b, h, l), col_q // wk + h)),
        pl.BlockSpec((tl, wk), lambda b, h, l: (rowblk(b, h, l), col_k // wk + h)),
        pl.BlockSpec((tl, wv), lambda b, h, l: (rowblk(b, h, l), col_v // wv + h)),
        pl.BlockSpec((tl, wv), lambda b, h, l: (rowblk(b, h, l), col_og // wv + h)),
    ]
    args = [proj, proj, proj, proj]
    if mode == "gla":
        col_glr, wgk, bgk = extra
        in_specs += [
            pl.BlockSpec((tl, LANES), lambda b, h, l: (rowblk(b, h, l), col_glr // LANES)),
            pl.BlockSpec((LANES, wk), lambda b, h, l: (0, h)),
            pl.BlockSpec((1, wk), lambda b, h, l: (0, h)),
        ]
        args += [proj, wgk, bgk]
    else:
        (lbl,) = extra
        in_specs += [pl.BlockSpec((DEPTH, wk), lambda b, h, l: (0, h))]
        args += [lbl]
    in_specs += [
        pl.BlockSpec((1, dv), lambda b, h, l: (0, 0)),
        pl.BlockSpec(mall.shape, lambda b, h, l: (0, 0)),
        pl.BlockSpec(masks.shape, lambda b, h, l: (0, 0, 0)),
    ]
    args += [gn, mall, masks]
    return pl.pallas_call(
        functools.partial(_gla_core_kernel, mode=mode, layer=layer, scale=dk ** -0.5,
                          packs=hps // pack, pack=pack, dk=dk, dv=dv),
        grid=(batch, heads // hps, nl),
        in_specs=in_specs,
        out_specs=pl.BlockSpec((tl, wv), lambda b, h, l: (rowblk(b, h, l), h)),
        out_shape=jax.ShapeDtypeStruct((t, heads * dv), BF16),
        scratch_shapes=[pltpu.VMEM((hps // pack, pack * dv, pack * dk), F32)],
        compiler_params=_cparams(("parallel", "parallel", "arbitrary")),
        name=mode + "_core",
    )(*args)


N_PAIRS = SSM_HEADS // 2
PAIRS_PER_GROUP = N_PAIRS // SSM_GROUPS


def _ssd_kernel(z_ref, x_ref, bm_ref, cm_ref, dt_ref, cw_ref, cb_ref, dtb_ref, alog_ref, dsk_ref,
                nw_ref, tri3_ref, o_ref,
                ext_ref, xs_ref, bs_ref, cs_ref, st_ref, y_ref, bt_ref, dtt_ref, wdt_ref, elb_ref):
    c = SSD_CHUNK
    halo = CONV_HALO
    l = pl.program_id(1)

    @pl.when(l == 0)
    def _():
        st_ref[...] = jnp.zeros_like(st_ref)
        ext_ref[0:halo, :] = jnp.zeros((halo, ext_ref.shape[1]), F32)

    @pl.when(l > 0)
    def _():
        ext_ref[0:halo, :] = ext_ref[c:c + halo, :]

    ext_ref[halo:halo + c, 0:SSM_DINNER] = x_ref[...]
    ext_ref[halo:halo + c, SSM_DINNER:SSM_DINNER + 1024] = bm_ref[...]
    ext_ref[halo:halo + c, SSM_DINNER + 1024:SSM_CONV_DIM] = cm_ref[...]

    for slab in range(SSM_CONV_DIM // LANES):
        cs = slice(slab * LANES, (slab + 1) * LANES)
        acc = cb_ref[:, cs] + cw_ref[0:1, cs] * ext_ref[halo - 3:halo - 3 + c, cs]
        for j in range(1, SSM_CONV):
            acc = acc + cw_ref[j:j + 1, cs] * ext_ref[halo - 3 + j:halo - 3 + j + c, cs]
        act = acc * jax.nn.sigmoid(acc)
        if slab < N_PAIRS:
            xs_ref[slab] = act
        elif slab < N_PAIRS + SSM_GROUPS:
            bs_ref[slab - N_PAIRS] = act
        else:
            cs_ref[slab - N_PAIRS - SSM_GROUPS] = act

    lane = lax.broadcasted_iota(jnp.int32, (1, LANES), 1)
    dtr = dt_ref[...] + dtb_ref[...]
    dt = jnp.maximum(dtr, 0.0) + jnp.log1p(jnp.exp(-jnp.abs(dtr)))
    a = jnp.where(lane < SSM_HEADS, -jnp.exp(alog_ref[...]), 0.0)
    la_hi, la_mid, la_lo = _split3(dt * (a * LOG2E))
    b = jnp.dot(tri3_ref[...], jnp.concatenate([la_hi, la_mid, la_lo], axis=0),
                preferred_element_type=F32)
    b_last = b[c - 1:c]
    wd = jnp.exp2(b_last - b) * dt
    bt = b.T
    bt_ref[...] = bt
    dtt_ref[...] = dt.T
    wdt_ref[...] = wd.T
    elb_ref[...] = jnp.broadcast_to(jnp.exp2(bt[:, c - 1:c]), (LANES, LANES))

    row = lax.broadcasted_iota(jnp.int32, (c, c), 0)
    col = lax.broadcasted_iota(jnp.int32, (c, c), 1)
    causal = col <= row
    lo = lax.broadcasted_iota(jnp.int32, (c, LANES), 1) < SSM_HEADDIM

    def group(g, carry):
        bg = bs_ref[g]
        cg = cs_ref[g]
        cgb = cg.astype(BF16)
        cbm = lax.dot_general(cgb, bg.astype(BF16), _NT, preferred_element_type=F32)
        bgt = bg.T
        for j in range(PAIRS_PER_GROUP):
            p = g * PAIRS_PER_GROUP + j
            xp = xs_ref[p]
            sp = st_ref[p]
            x_lo = jnp.where(lo, xp, 0.0).astype(BF16)
            x_hi = jnp.where(lo, 0.0, xp).astype(BF16)
            x_bd = jnp.concatenate([x_lo, x_hi], axis=0)
            lhs_y = []
            lhs_s = []
            bhs = []
            for k in range(2):
                h = 2 * p + k
                bh = jnp.broadcast_to(bt_ref[pl.ds(h, 1), :], (c, c)).T
                rel = bh - bt_ref[pl.ds(h, 1), :]
                dec = jnp.exp2(jnp.where(causal, rel, MASKED_LOG2))
                lhs_y.append((cbm * dec * dtt_ref[pl.ds(h, 1), :]).astype(BF16))
                lhs_s.append((bgt * wdt_ref[pl.ds(h, 1), :]).astype(BF16))
                bhs.append(bh)
            y = jnp.exp2(jnp.where(lo, bhs[0], bhs[1])) * jnp.dot(
                cgb, sp.astype(BF16), preferred_element_type=F32)
            y = y + jnp.dot(jnp.concatenate(lhs_y, axis=1), x_bd, preferred_element_type=F32)
            su = jnp.dot(jnp.concatenate(lhs_s, axis=1), x_bd,
                         preferred_element_type=F32)
            el = jnp.where(lane < SSM_HEADDIM, elb_ref[pl.ds(2 * p, 1), :],
                           elb_ref[pl.ds(2 * p + 1, 1), :])
            st_ref[p] = sp * el + su
            y_ref[p] = y + dsk_ref[p] * xp
        return carry

    lax.fori_loop(0, SSM_GROUPS, group, 0)

    gw = PAIRS_PER_GROUP * LANES
    for g in range(SSM_GROUPS):
        cs = slice(g * gw, (g + 1) * gw)
        yg = jnp.concatenate([y_ref[g * PAIRS_PER_GROUP + j] for j in range(PAIRS_PER_GROUP)],
                             axis=1)
        zg = z_ref[:, cs]
        yg = yg * (zg * jax.nn.sigmoid(zg))
        o_ref[:, cs] = _rms(yg, nw_ref[:, cs]).astype(o_ref.dtype)


def _ssd_core(proj, conv_w, conv_b, dt_bias, a_log, d_skip, norm_w, *, batch, seq):
    c = SSD_CHUNK
    nl = seq // c
    t = batch * seq
    tri = np.tril(np.ones((c, c), np.float32))
    tri3 = jnp.asarray(np.concatenate([tri, tri, tri], axis=1), BF16)
    pad = LANES - SSM_HEADS
    dtb = jnp.pad(dt_bias.astype(F32), (0, pad)).reshape(1, LANES)
    alog = jnp.pad(a_log.astype(F32), (0, pad)).reshape(1, LANES)
    dsk = jnp.repeat(d_skip.astype(F32), SSM_HEADDIM).reshape(N_PAIRS, 1, LANES)

    def rb(b, l):
        return b * nl + l

    full2 = lambda b, l: (0, 0)
    in_specs = [
        pl.BlockSpec((c, SSM_DINNER), lambda b, l: (rb(b, l), 0)),
        pl.BlockSpec((c, SSM_DINNER), lambda b, l: (rb(b, l), 1)),
        pl.BlockSpec((c, 1024), lambda b, l: (rb(b, l), 2 * SSM_DINNER // 1024)),
        pl.BlockSpec((c, 1024), lambda b, l: (rb(b, l), 2 * SSM_DINNER // 1024 + 1)),
        pl.BlockSpec((c, LANES), lambda b, l: (rb(b, l), (SSM_DINNER + SSM_CONV_DIM) // LANES)),
        pl.BlockSpec((SSM_CONV, SSM_CONV_DIM), full2),
        pl.BlockSpec((1, SSM_CONV_DIM), full2),
        pl.BlockSpec((1, LANES), full2),
        pl.BlockSpec((1, LANES), full2),
        pl.BlockSpec((N_PAIRS, 1, LANES), lambda b, l: (0, 0, 0)),
        pl.BlockSpec((1, SSM_DINNER), full2),
        pl.BlockSpec(tri3.shape, full2),
    ]
    scratch = [
        pltpu.VMEM((CONV_HALO + c, SSM_CONV_DIM), F32),
        pltpu.VMEM((N_PAIRS, c, LANES), F32),
        pltpu.VMEM((SSM_GROUPS, c, SSM_STATE), F32),
        pltpu.VMEM((SSM_GROUPS, c, SSM_STATE), F32),
        pltpu.VMEM((N_PAIRS, SSM_STATE, LANES), F32),
        pltpu.VMEM((N_PAIRS, c, LANES), F32),
        pltpu.VMEM((LANES, c), F32),
        pltpu.VMEM((LANES, c), F32),
        pltpu.VMEM((LANES, c), F32),
        pltpu.VMEM((LANES, LANES), F32),
    ]
    return pl.pallas_call(
        _ssd_kernel,
        grid=(batch, nl),
        in_specs=in_specs,
        out_specs=pl.BlockSpec((c, SSM_DINNER), lambda b, l: (rb(b, l), 0)),
        out_shape=jax.ShapeDtypeStruct((t, SSM_DINNER), BF16),
        scratch_shapes=scratch,
        compiler_params=_cparams(("parallel", "arbitrary")),
        name="ssd_core",
    )(proj, proj, proj, proj, proj, conv_w.astype(F32), conv_b.astype(F32).reshape(1, -1),
      dtb, alog, dsk, norm_w.astype(F32).reshape(1, -1), tri3)


def _pad_last(w, n):
    return jnp.pad(w, [(0, 0)] * (w.ndim - 1) + [(0, n - w.shape[-1])])


def kernel(x, p, norm_mix, norm_mlp, norm_ple, norm_final, w_up, w_down, w_ple_proj, w_ple_gate,
           gla_w_in, gla_w_gk2, gla_b_gk, gla_gn, gla_w_out,
           hgrn_lb_logits, hgrn_w_in, hgrn_gn, hgrn_w_out,
           ssm_w_in, ssm_conv_w, ssm_conv_b, ssm_dt_bias, ssm_a_log, ssm_d, ssm_norm, ssm_w_out):
    batch, seq, d = x.shape
    t = batch * seq
    h = x.reshape(t, d)
    pf = p.reshape(DEPTH, t, PLE_DIM)
    row = lambda v: v.astype(F32).reshape(1, -1)

    gla_w_in_b = _pad_last(gla_w_in, GLA_IN_COLS).astype(BF16)
    hgrn_w_in_b = hgrn_w_in.astype(BF16)
    ssm_w_in_b = _pad_last(ssm_w_in, SSM_IN_COLS).astype(BF16)
    gla_w_out_b = gla_w_out.astype(BF16)
    hgrn_w_out_b = hgrn_w_out.astype(BF16)
    ssm_w_out_b = ssm_w_out.astype(BF16)
    w_up_b = w_up.astype(BF16)
    w_down_b = w_down.astype(BF16)
    w_gate_b = w_ple_gate.astype(BF16)
    w_proj_b = w_ple_proj.astype(BF16)

    for i in range(DEPTH):
        kind, j = i % N_MIXERS, i // N_MIXERS
        nw = row(norm_mix[i])
        if kind == 0:
            proj = _norm_matmul(h, nw, gla_w_in_b, j, tm=1024, tn=1280)
            kd = GLA_HEADS * GLA_DK
            vd = GLA_HEADS * GLA_DV
            wgk = jnp.pad(gla_w_gk2[j], ((0, LANES - GLA_GATE_RANK), (0, 0))).astype(BF16)
            o = _gla_core(proj, mode="gla", layer=i, batch=batch, seq=seq, heads=GLA_HEADS,
                          hps=GLA_HEADS, pack=1, dk=GLA_DK, dv=GLA_DV, col_q=0, col_k=kd, col_v=2 * kd,
                          col_og=2 * kd + vd,
                          extra=(2 * kd + 2 * vd, wgk, row(gla_b_gk[j])), gn=row(gla_gn[j]))
            h = _matmul_res(o, gla_w_out_b, j, h, tm=512)
        elif kind == 1:
            proj = _norm_matmul(h, nw, hgrn_w_in_b, j, tm=1024, tn=1024)
            fd = HGRN_HEADS * HGRN_DK
            vd = HGRN_HEADS * HGRN_DV
            o = _gla_core(proj, mode="hgrn", layer=i, batch=batch, seq=seq, heads=HGRN_HEADS,
                          hps=HGRN_HEADS // 2, pack=2, dk=HGRN_DK, dv=HGRN_DV, col_q=0, col_k=fd,
                          col_v=2 * fd, col_og=2 * fd + vd,
                          extra=(hgrn_lb_logits.astype(F32),), gn=row(hgrn_gn[j]))
            h = _matmul_res(o, hgrn_w_out_b, j, h, tm=512)
        else:
            proj = _norm_matmul(h, nw, ssm_w_in_b, j, tm=512, tn=1792)
            o = _ssd_core(proj, ssm_conv_w[j], ssm_conv_b[j], ssm_dt_bias[j], ssm_a_log[j],
                          ssm_d[j], ssm_norm[j], batch=batch, seq=seq)
            h = _matmul_res(o, ssm_w_out_b, j, h, tm=512)
        h = _mlp(h, row(norm_mlp[i]), w_up_b, w_down_b, i, tm=1024, tf=512)
        h = _ple(h, row(norm_ple[i]), w_gate_b, pf, w_proj_b, row(norm_final), i,
                 final_norm=(i == DEPTH - 1), tm=512, tn=512)
    return h.reshape(batch, seq, d)
```

```python
import functools
import math

import numpy as np
import jax
import jax.numpy as jnp
from jax import lax
from jax.experimental import pallas as pl
from jax.experimental.pallas import tpu as pltpu

F32 = jnp.float32
BF16 = jnp.bfloat16

D_MODEL = 2048
DEPTH = 4
N_MIXERS = 3
PLE_DIM = 256
D_FF = 4 * D_MODEL
EPS = 1e-6

GLA_HEADS = 4
GLA_DK = 256
GLA_DV = 512
GLA_GATE_RANK = 16
GLA_GATE_NORM = 16.0
HGRN_HEADS = 16
HGRN_DK = 128
HGRN_DV = 128
SSM_DINNER = 4096
SSM_HEADDIM = 64
SSM_HEADS = 64
SSM_GROUPS = 8
SSM_STATE = 128
SSM_CONV = 4
SSM_CONV_DIM = SSM_DINNER + 2 * SSM_GROUPS * SSM_STATE

LANES = 128
VMEM_LIMIT_BYTES = 56 * 1024 * 1024

GLA_CHUNK = 64
GLA_TOKENS_PER_STEP = 512
SSD_CHUNK = 128
CONV_HALO = 8

GLA_IN_COLS = 6400
SSM_IN_COLS = 10752
N_LEVELS = 6
LOG2E = math.log2(math.e)
MASKED_LOG2 = -1e30


def _cparams(sem):
    return pltpu.CompilerParams(dimension_semantics=sem, vmem_limit_bytes=VMEM_LIMIT_BYTES)


def _rms(x, w):
    ms = jnp.mean(x * x, axis=-1, keepdims=True)
    return x * lax.rsqrt(ms + EPS) * w


def _split3(x):
    hi = x.astype(BF16)
    r1 = x - hi.astype(F32)
    mid = r1.astype(BF16)
    lo = (r1 - mid.astype(F32)).astype(BF16)
    return hi, mid, lo


def _norm_matmul_kernel(h_ref, nw_ref, w_ref, o_ref, u_ref):
    @pl.when(pl.program_id(1) == 0)
    def _():
        u_ref[...] = _rms(h_ref[...], nw_ref[...]).astype(BF16)

    o_ref[...] = jnp.dot(u_ref[...], w_ref[...], preferred_element_type=F32).astype(o_ref.dtype)


def _norm_matmul(h, nw, w, layer, *, tm, tn):
    t, d = h.shape
    n = w.shape[2]
    return pl.pallas_call(
        _norm_matmul_kernel,
        grid=(t // tm, n // tn),
        in_specs=[
            pl.BlockSpec((tm, d), lambda i, j: (i, 0)),
            pl.BlockSpec((1, d), lambda i, j: (0, 0)),
            pl.BlockSpec((None, d, tn), lambda i, j: (layer, 0, j)),
        ],
        out_specs=pl.BlockSpec((tm, tn), lambda i, j: (i, j)),
        out_shape=jax.ShapeDtypeStruct((t, n), F32),
        scratch_shapes=[pltpu.VMEM((tm, d), BF16)],
        compiler_params=_cparams(("parallel", "arbitrary")),
        name="norm_matmul",
    )(h, nw, w)


def _matmul_res_kernel(a_ref, w_ref, h_ref, o_ref):
    o_ref[...] = h_ref[...] + jnp.dot(a_ref[...], w_ref[...], preferred_element_type=F32)


def _matmul_res(a, w, layer, h, *, tm):
    t, k = a.shape
    n = w.shape[2]
    return pl.pallas_call(
        _matmul_res_kernel,
        grid=(t // tm,),
        in_specs=[
            pl.BlockSpec((tm, k), lambda i: (i, 0)),
            pl.BlockSpec((None, k, n), lambda i: (layer, 0, 0), pipeline_mode=pl.Buffered(1)),
            pl.BlockSpec((tm, n), lambda i: (i, 0)),
        ],
        out_specs=pl.BlockSpec((tm, n), lambda i: (i, 0)),
        out_shape=jax.ShapeDtypeStruct((t, n), F32),
        compiler_params=_cparams(("parallel",)),
        name="matmul_res",
    )(a, w, h)


def _mlp_kernel(h_ref, nw_ref, wu_ref, wd_ref, o_ref, u_ref):
    @pl.when(pl.program_id(1) == 0)
    def _():
        x = h_ref[...]
        u_ref[...] = _rms(x, nw_ref[...]).astype(BF16)
        o_ref[...] = x

    a = jnp.dot(u_ref[...], wu_ref[...], preferred_element_type=F32)
    a = jnp.square(jnp.maximum(a, 0.0)).astype(BF16)
    o_ref[...] += jnp.dot(a, wd_ref[...], preferred_element_type=F32)


def _mlp(h, nw, wu, wd, layer, *, tm, tf):
    t, d = h.shape
    ff = wu.shape[2]
    return pl.pallas_call(
        _mlp_kernel,
        grid=(t // tm, ff // tf),
        in_specs=[
            pl.BlockSpec((tm, d), lambda i, f: (i, 0)),
            pl.BlockSpec((1, d), lambda i, f: (0, 0)),
            pl.BlockSpec((None, d, tf), lambda i, f: (layer, 0, f)),
            pl.BlockSpec((None, tf, d), lambda i, f: (layer, f, 0)),
        ],
        out_specs=pl.BlockSpec((tm, d), lambda i, f: (i, 0)),
        out_shape=jax.ShapeDtypeStruct((t, d), F32),
        scratch_shapes=[pltpu.VMEM((tm, d), BF16)],
        compiler_params=_cparams(("parallel", "arbitrary")),
        name="mlp",
    )(h, nw, wu, wd)


def _ple_kernel(h_ref, nw_ref, wg_ref, p_ref, wp_ref, nf_ref, o_ref, *, final_norm, tn):
    x = h_ref[...]
    u = _rms(x, nw_ref[...]).astype(BF16)
    pb = p_ref[...].astype(BF16)
    d = x.shape[1]
    for n0 in range(0, d, tn):
        cs = slice(n0, n0 + tn)
        gate = jax.nn.sigmoid(jnp.dot(u, wg_ref[:, cs], preferred_element_type=F32))
        proj = jnp.dot(pb, wp_ref[:, cs], preferred_element_type=F32)
        o_ref[:, cs] = h_ref[:, cs] + gate * proj
    if final_norm:
        o_ref[...] = _rms(o_ref[...], nf_ref[...])


def _ple(h, nw, wg, p, wp, nf, layer, *, final_norm, tm, tn):
    t, d = h.shape
    pd = p.shape[2]
    return pl.pallas_call(
        functools.partial(_ple_kernel, final_norm=final_norm, tn=tn),
        grid=(t // tm,),
        in_specs=[
            pl.BlockSpec((tm, d), lambda i: (i, 0)),
            pl.BlockSpec((1, d), lambda i: (0, 0)),
            pl.BlockSpec((None, d, d), lambda i: (layer, 0, 0)),
            pl.BlockSpec((None, tm, pd), lambda i: (layer, i, 0)),
            pl.BlockSpec((None, pd, d), lambda i: (layer, 0, 0)),
            pl.BlockSpec((1, d), lambda i: (0, 0)),
        ],
        out_specs=pl.BlockSpec((tm, d), lambda i: (i, 0)),
        out_shape=jax.ShapeDtypeStruct((t, d), F32),
        compiler_params=_cparams(("parallel",)),
        name="ple",
    )(h, nw, wg, p, wp, nf)


MATMUL_LEVELS = (3, 4)


def _gla_arg_matrix():
    c = GLA_CHUNK
    t = np.arange(c)[:, None]
    r = np.arange(c)[None, :]
    blocks = [(r <= t)]
    for lvl in MATMUL_LEVELS:
        n = c >> (lvl + 1)
        ref = (t // (2 * n)) * (2 * n) + n - 1
        lower = t > ref
        blocks.append(np.where(lower, (r > ref) & (r <= t), (r > t) & (r <= ref)))
    m = np.concatenate(blocks, axis=0).astype(np.float32)
    return np.concatenate([m, m, m, np.zeros_like(m)], axis=1)


def _coarse_level_args(b, n):
    out = []
    for blk in range(GLA_CHUNK // (2 * n)):
        r0 = blk * 2 * n
        ref = b[r0 + n - 1:r0 + n]
        out.append(ref - b[r0:r0 + n])
        out.append(b[r0 + n:r0 + 2 * n] - ref)
    return jnp.concatenate(out, axis=0)


def _gla_masks(pack):
    c = GLA_CHUNK
    t = np.arange(c)[:, None]
    s = np.arange(c)[None, :]
    masks = np.zeros((1 + N_LEVELS, c, c), np.float32)
    masks[0] = (t == s)
    for lvl in range(N_LEVELS):
        n = c >> (lvl + 1)
        same = (t // (2 * n)) == (s // (2 * n))
        masks[1 + lvl] = same & ((t % (2 * n)) >= n) & ((s % (2 * n)) < n)
    return np.tile(masks, (1, 1, pack))


def _block_diag_rows(x, pack, width):
    if pack == 1:
        return x
    head = lax.broadcasted_iota(jnp.int32, x.shape, 1) // width
    return jnp.concatenate([jnp.where(head == p, x, jnp.zeros_like(x)) for p in range(pack)], axis=0)


_NT = (((1,), (1,)), ((), ()))
_TN = (((0,), (0,)), ((), ()))


def _gla_core_kernel(*refs, mode, layer, scale, packs, pack, dk, dv):
    wk = pack * dk
    wv = pack * dv
    if mode == "gla":
        (q_ref, k_ref, v_ref, og_ref, glr_ref, wgk_ref, bgk_ref, gn_ref, mall_ref, masks_ref,
         o_ref, st_ref) = refs
    else:
        (q_ref, k_ref, v_ref, og_ref, lbl_ref, gn_ref, mall_ref, masks_ref, o_ref, st_ref) = refs

    @pl.when(pl.program_id(2) == 0)
    def _():
        st_ref[...] = jnp.zeros_like(st_ref)

    if mode == "hgrn":
        lg = lbl_ref[...]
        e = jnp.exp(lg - jnp.max(lg, axis=0, keepdims=True))
        sm = e / jnp.sum(e, axis=0, keepdims=True)
        lb_all = jnp.zeros_like(sm[0:1])
        for r in range(1, layer + 1):
            lb_all = lb_all + sm[r:r + 1]

    c = GLA_CHUNK
    n_chunks = q_ref.shape[0] // c
    odd_row = lax.broadcasted_iota(jnp.int32, (c, wk), 0) % 2 == 1
    if pack > 1:
        st_diag = (lax.broadcasted_iota(jnp.int32, (wv, wk), 0) // dv
                   == lax.broadcasted_iota(jnp.int32, (wv, wk), 1) // dk)

    def chunk(ci, carry):
        rows = pl.ds(pl.multiple_of(ci * c, c), c)
        if mode == "gla":
            z_all = jnp.dot(glr_ref[rows, :].astype(BF16), wgk_ref[...],
                            preferred_element_type=F32) + bgk_ref[...]
        for g in range(packs):
            ks = slice(g * wk, (g + 1) * wk)
            vs = slice(g * wv, (g + 1) * wv)
            vc = v_ref[rows, vs]
            og = og_ref[rows, vs]
            if mode == "gla":
                qc = q_ref[rows, ks] * scale
                kc = k_ref[rows, ks]
                z = z_all[:, ks]
                gc = (jnp.minimum(z, 0.0) - jnp.log1p(jnp.exp(-jnp.abs(z)))) * (LOG2E / GLA_GATE_NORM)
                gate = og * jax.nn.sigmoid(og)
            else:
                qq = q_ref[rows, ks]
                qc = qq * jax.nn.sigmoid(qq) * scale
                lb = lb_all[:, ks]
                f = lb + (1.0 - lb) * jax.nn.sigmoid(k_ref[rows, ks])
                kc = 1.0 - f
                gc = jnp.log(f) * LOG2E
                gate = jax.nn.sigmoid(og)

            g_hi, g_mid, g_lo = _split3(gc)
            gs = jnp.concatenate([g_hi, g_mid, g_lo, jnp.zeros_like(g_hi)], axis=0)
            pre = jnp.dot(mall_ref[...], gs, preferred_element_type=F32)
            b = pre[0:c]
            lvl_args = [_coarse_level_args(b, c >> (lvl + 1)) for lvl in range(MATMUL_LEVELS[0])]
            lvl_args += [pre[c:2 * c], pre[2 * c:3 * c], jnp.where(odd_row, gc, 0.0)]
            e_b = jnp.exp2(b)

            st = st_ref[g]
            q_in = (qc * e_b).astype(BF16)
            o = lax.dot_general(q_in, st.astype(BF16), _NT, preferred_element_type=F32)
            k_dec = (kc * jnp.exp2(b[c - 1:c] - b)).astype(BF16)

            att = masks_ref[0] * lax.dot_general(
                qc.astype(BF16), _block_diag_rows(kc.astype(BF16), pack, dk), _NT,
                preferred_element_type=F32)
            for lvl in range(N_LEVELS):
                el = jnp.exp2(lvl_args[lvl])
                ql = (qc * el).astype(BF16)
                kl = _block_diag_rows((kc * el).astype(BF16), pack, dk)
                att = att + masks_ref[1 + lvl] * lax.dot_general(ql, kl, _NT,
                                                                 preferred_element_type=F32)
            vb = vc.astype(BF16)
            o = o + jnp.dot(att.astype(BF16), _block_diag_rows(vb, pack, dv),
                            preferred_element_type=F32)

            e_last = e_b[c - 1:c]
            upd = lax.dot_general(vb, k_dec, _TN, preferred_element_type=F32)
            if pack > 1:
                upd = jnp.where(st_diag, upd, 0.0)
            st_ref[g] = st * e_last + upd

            for p in range(pack):
                hs = slice(p * dv, (p + 1) * dv)
                o_ref[rows, g * wv + p * dv:g * wv + (p + 1) * dv] = (
                    _rms(o[:, hs], gn_ref[...]) * gate[:, hs]).astype(o_ref.dtype)
        return carry

    lax.fori_loop(0, n_chunks, chunk, 0, unroll=2)


def _gla_core(proj, *, mode, layer, batch, seq, heads, hps, pack, dk, dv, col_q, col_k, col_v,
              col_og, extra, gn):
    tl = GLA_TOKENS_PER_STEP
    nl = seq // tl
    t = batch * seq
    wk = hps * dk
    wv = hps * dv
    mall = jnp.asarray(_gla_arg_matrix(), BF16)
    masks = jnp.asarray(_gla_masks(pack), F32)

    def rowblk(b, h, l):
        return b * nl + l

    in_specs = [
        pl.BlockSpec((tl, wk), lambda b, h, l: (rowblk(b, h, l), col_q // wk + h)),
        pl.BlockSpec((tl, wk), lambda b, h, l: (rowblk(b, h, l), col_k // wk + h)),
        pl.BlockSpec((tl, wv), lambda b, h, l: (rowblk(b, h, l), col_v // wv + h)),
        pl.BlockSpec((tl, wv), lambda b, h, l: (rowblk(b, h, l), col_og // wv + h)),
    ]
    args = [proj, proj, proj, proj]
    if mode == "gla":
        col_glr, wgk, bgk = extra
        in_specs += [
            pl.BlockSpec((tl, LANES), lambda b, h, l: (rowblk(b, h, l), col_glr // LANES)),
            pl.BlockSpec((LANES, wk), lambda b, h, l: (0, h)),
            pl.BlockSpec((1, wk), lambda b, h, l: (0, h)),
        ]
        args += [proj, wgk, bgk]
    else:
        (lbl,) = extra
        in_specs += [pl.BlockSpec((DEPTH, wk), lambda b, h, l: (0, h))]
        args += [lbl]
    in_specs += [
        pl.BlockSpec((1, dv), lambda b, h, l: (0, 0)),
        pl.BlockSpec(mall.shape, lambda b, h, l: (0, 0)),
        pl.BlockSpec(masks.shape, lambda b, h, l: (0, 0, 0)),
    ]
    args += [gn, mall, masks]
    return pl.pallas_call(
        functools.partial(_gla_core_kernel, mode=mode, layer=layer, scale=dk ** -0.5,
                          packs=hps // pack, pack=pack, dk=dk, dv=dv),
        grid=(batch, heads // hps, nl),
        in_specs=in_specs,
        out_specs=pl.BlockSpec((tl, wv), lambda b, h, l: (rowblk(b, h, l), h)),
        out_shape=jax.ShapeDtypeStruct((t, heads * dv), BF16),
        scratch_shapes=[pltpu.VMEM((hps // pack, pack * dv, pack * dk), F32)],
        compiler_params=_cparams(("parallel", "parallel", "arbitrary")),
        name=mode + "_core",
    )(*args)


N_PAIRS = SSM_HEADS // 2
PAIRS_PER_GROUP = N_PAIRS // SSM_GROUPS


def _ssd_kernel(z_ref, x_ref, bm_ref, cm_ref, dt_ref, cw_ref, cb_ref, dtb_ref, alog_ref, dsk_ref,
                nw_ref, tri3_ref, o_ref,
                ext_ref, xs_ref, bs_ref, cs_ref, st_ref, y_ref, bt_ref, dtt_ref, wdt_ref, elb_ref):
    c = SSD_CHUNK
    halo = CONV_HALO
    l = pl.program_id(1)

    @pl.when(l == 0)
    def _():
        st_ref[...] = jnp.zeros_like(st_ref)
        ext_ref[0:halo, :] = jnp.zeros((halo, ext_ref.shape[1]), F32)

    @pl.when(l > 0)
    def _():
        ext_ref[0:halo, :] = ext_ref[c:c + halo, :]

    ext_ref[halo:halo + c, 0:SSM_DINNER] = x_ref[...]
    ext_ref[halo:halo + c, SSM_DINNER:SSM_DINNER + 1024] = bm_ref[...]
    ext_ref[halo:halo + c, SSM_DINNER + 1024:SSM_CONV_DIM] = cm_ref[...]

    for slab in range(SSM_CONV_DIM // LANES):
        cs = slice(slab * LANES, (slab + 1) * LANES)
        acc = cb_ref[:, cs] + cw_ref[0:1, cs] * ext_ref[halo - 3:halo - 3 + c, cs]
        for j in range(1, SSM_CONV):
            acc = acc + cw_ref[j:j + 1, cs] * ext_ref[halo - 3 + j:halo - 3 + j + c, cs]
        act = acc * jax.nn.sigmoid(acc)
        if slab < N_PAIRS:
            xs_ref[slab] = act
        elif slab < N_PAIRS + SSM_GROUPS:
            bs_ref[slab - N_PAIRS] = act
        else:
            cs_ref[slab - N_PAIRS - SSM_GROUPS] = act

    lane = lax.broadcasted_iota(jnp.int32, (1, LANES), 1)
    dtr = dt_ref[...] + dtb_ref[...]
    dt = jnp.maximum(dtr, 0.0) + jnp.log1p(jnp.exp(-jnp.abs(dtr)))
    a = jnp.where(lane < SSM_HEADS, -jnp.exp(alog_ref[...]), 0.0)
    la_hi, la_mid, la_lo = _split3(dt * (a * LOG2E))
    b = jnp.dot(tri3_ref[...], jnp.concatenate([la_hi, la_mid, la_lo], axis=0),
                preferred_element_type=F32)
    b_last = b[c - 1:c]
    wd = jnp.exp2(b_last - b) * dt
    bt = b.T
    bt_ref[...] = bt
    dtt_ref[...] = dt.T
    wdt_ref[...] = wd.T
    elb_ref[...] = jnp.broadcast_to(jnp.exp2(bt[:, c - 1:c]), (LANES, LANES))

    row = lax.broadcasted_iota(jnp.int32, (c, c), 0)
    col = lax.broadcasted_iota(jnp.int32, (c, c), 1)
    causal = col <= row
    lo = lax.broadcasted_iota(jnp.int32, (c, LANES), 1) < SSM_HEADDIM

    def group(g, carry):
        bg = bs_ref[g]
        cg = cs_ref[g]
        cgb = cg.astype(BF16)
        cbm = lax.dot_general(cgb, bg.astype(BF16), _NT, preferred_element_type=F32)
        bgt = bg.T
        for j in range(PAIRS_PER_GROUP):
            p = g * PAIRS_PER_GROUP + j
            xp = xs_ref[p]
            sp = st_ref[p]
            x_lo = jnp.where(lo, xp, 0.0).astype(BF16)
            x_hi = jnp.where(lo, 0.0, xp).astype(BF16)
            x_bd = jnp.concatenate([x_lo, x_hi], axis=0)
            lhs_y = []
            lhs_s = []
            bhs = []
            for k in range(2):
                h = 2 * p + k
                bh = jnp.broadcast_to(bt_ref[pl.ds(h, 1), :], (c, c)).T
                rel = bh - bt_ref[pl.ds(h, 1), :]
                dec = jnp.exp2(jnp.where(causal, rel, MASKED_LOG2))
                lhs_y.append((cbm * dec * dtt_ref[pl.ds(h, 1), :]).astype(BF16))
                lhs_s.append((bgt * wdt_ref[pl.ds(h, 1), :]).astype(BF16))
                bhs.append(bh)
            y = jnp.exp2(jnp.where(lo, bhs[0], bhs[1])) * jnp.dot(
                cgb, sp.astype(BF16), preferred_element_type=F32)
            y = y + jnp.dot(jnp.concatenate(lhs_y, axis=1), x_bd, preferred_element_type=F32)
            su = jnp.dot(jnp.concatenate(lhs_s, axis=1), x_bd,
                         preferred_element_type=F32)
            el = jnp.where(lane < SSM_HEADDIM, elb_ref[pl.ds(2 * p, 1), :],
                           elb_ref[pl.ds(2 * p + 1, 1), :])
            st_ref[p] = sp * el + su
            y_ref[p] = y + dsk_ref[p] * xp
        return carry

    lax.fori_loop(0, SSM_GROUPS, group, 0)

    gw = PAIRS_PER_GROUP * LANES
    for g in range(SSM_GROUPS):
        cs = slice(g * gw, (g + 1) * gw)
        yg = jnp.concatenate([y_ref[g * PAIRS_PER_GROUP + j] for j in range(PAIRS_PER_GROUP)],
                             axis=1)
        zg = z_ref[:, cs]
        yg = yg * (zg * jax.nn.sigmoid(zg))
        o_ref[:, cs] = _rms(yg, nw_ref[:, cs]).astype(o_ref.dtype)


def _ssd_core(proj, conv_w, conv_b, dt_bias, a_log, d_skip, norm_w, *, batch, seq):
    c = SSD_CHUNK
    nl = seq // c
    t = batch * seq
    tri = np.tril(np.ones((c, c), np.float32))
    tri3 = jnp.asarray(np.concatenate([tri, tri, tri], axis=1), BF16)
    pad = LANES - SSM_HEADS
    dtb = jnp.pad(dt_bias.astype(F32), (0, pad)).reshape(1, LANES)
    alog = jnp.pad(a_log.astype(F32), (0, pad)).reshape(1, LANES)
    dsk = jnp.repeat(d_skip.astype(F32), SSM_HEADDIM).reshape(N_PAIRS, 1, LANES)

    def rb(b, l):
        return b * nl + l

    full2 = lambda b, l: (0, 0)
    in_specs = [
        pl.BlockSpec((c, SSM_DINNER), lambda b, l: (rb(b, l), 0)),
        pl.BlockSpec((c, SSM_DINNER), lambda b, l: (rb(b, l), 1)),
        pl.BlockSpec((c, 1024), lambda b, l: (rb(b, l), 2 * SSM_DINNER // 1024)),
        pl.BlockSpec((c, 1024), lambda b, l: (rb(b, l), 2 * SSM_DINNER // 1024 + 1)),
        pl.BlockSpec((c, LANES), lambda b, l: (rb(b, l), (SSM_DINNER + SSM_CONV_DIM) // LANES)),
        pl.BlockSpec((SSM_CONV, SSM_CONV_DIM), full2),
        pl.BlockSpec((1, SSM_CONV_DIM), full2),
        pl.BlockSpec((1, LANES), full2),
        pl.BlockSpec((1, LANES), full2),
        pl.BlockSpec((N_PAIRS, 1, LANES), lambda b, l: (0, 0, 0)),
        pl.BlockSpec((1, SSM_DINNER), full2),
        pl.BlockSpec(tri3.shape, full2),
    ]
    scratch = [
        pltpu.VMEM((CONV_HALO + c, SSM_CONV_DIM), F32),
        pltpu.VMEM((N_PAIRS, c, LANES), F32),
        pltpu.VMEM((SSM_GROUPS, c, SSM_STATE), F32),
        pltpu.VMEM((SSM_GROUPS, c, SSM_STATE), F32),
        pltpu.VMEM((N_PAIRS, SSM_STATE, LANES), F32),
        pltpu.VMEM((N_PAIRS, c, LANES), F32),
        pltpu.VMEM((LANES, c), F32),
        pltpu.VMEM((LANES, c), F32),
        pltpu.VMEM((LANES, c), F32),
        pltpu.VMEM((LANES, LANES), F32),
    ]
    return pl.pallas_call(
        _ssd_kernel,
        grid=(batch, nl),
        in_specs=in_specs,
        out_specs=pl.BlockSpec((c, SSM_DINNER), lambda b, l: (rb(b, l), 0)),
        out_shape=jax.ShapeDtypeStruct((t, SSM_DINNER), BF16),
        scratch_shapes=scratch,
        compiler_params=_cparams(("parallel", "arbitrary")),
        name="ssd_core",
    )(proj, proj, proj, proj, proj, conv_w.astype(F32), conv_b.astype(F32).reshape(1, -1),
      dtb, alog, dsk, norm_w.astype(F32).reshape(1, -1), tri3)


def _pad_last(w, n):
    return jnp.pad(w, [(0, 0)] * (w.ndim - 1) + [(0, n - w.shape[-1])])


def kernel(x, p, norm_mix, norm_mlp, norm_ple, norm_final, w_up, w_down, w_ple_proj, w_ple_gate,
           gla_w_in, gla_w_gk2, gla_b_gk, gla_gn, gla_w_out,
           hgrn_lb_logits, hgrn_w_in, hgrn_gn, hgrn_w_out,
           ssm_w_in, ssm_conv_w, ssm_conv_b, ssm_dt_bias, ssm_a_log, ssm_d, ssm_norm, ssm_w_out):
    batch, seq, d = x.shape
    t = batch * seq
    h = x.reshape(t, d)
    pf = p.reshape(DEPTH, t, PLE_DIM)
    row = lambda v: v.astype(F32).reshape(1, -1)

    gla_w_in_b = _pad_last(gla_w_in, GLA_IN_COLS).astype(BF16)
    hgrn_w_in_b = hgrn_w_in.astype(BF16)
    ssm_w_in_b = _pad_last(ssm_w_in, SSM_IN_COLS).astype(BF16)
    gla_w_out_b = gla_w_out.astype(BF16)
    hgrn_w_out_b = hgrn_w_out.astype(BF16)
    ssm_w_out_b = ssm_w_out.astype(BF16)
    w_up_b = w_up.astype(BF16)
    w_down_b = w_down.astype(BF16)
    w_gate_b = w_ple_gate.astype(BF16)
    w_proj_b = w_ple_proj.astype(BF16)

    for i in range(DEPTH):
        kind, j = i % N_MIXERS, i // N_MIXERS
        nw = row(norm_mix[i])
        if kind == 0:
            proj = _norm_matmul(h, nw, gla_w_in_b, j, tm=1024, tn=1280)
            kd = GLA_HEADS * GLA_DK
            vd = GLA_HEADS * GLA_DV
            wgk = jnp.pad(gla_w_gk2[j], ((0, LANES - GLA_GATE_RANK), (0, 0))).astype(BF16)
            o = _gla_core(proj, mode="gla", layer=i, batch=batch, seq=seq, heads=GLA_HEADS,
                          hps=GLA_HEADS, pack=1, dk=GLA_DK, dv=GLA_DV, col_q=0, col_k=kd, col_v=2 * kd,
                          col_og=2 * kd + vd,
                          extra=(2 * kd + 2 * vd, wgk, row(gla_b_gk[j])), gn=row(gla_gn[j]))
            h = _matmul_res(o, gla_w_out_b, j, h, tm=512)
        elif kind == 1:
            proj = _norm_matmul(h, nw, hgrn_w_in_b, j, tm=1024, tn=1024)
            fd = HGRN_HEADS * HGRN_DK
            vd = HGRN_HEADS * HGRN_DV
            o = _gla_core(proj, mode="hgrn", layer=i, batch=batch, seq=seq, heads=HGRN_HEADS,
                          hps=HGRN_HEADS // 2, pack=2, dk=HGRN_DK, dv=HGRN_DV, col_q=0, col_k=fd,
                          col_v=2 * fd, col_og=2 * fd + vd,
                          extra=(hgrn_lb_logits.astype(F32),), gn=row(hgrn_gn[j]))
            h = _matmul_res(o, hgrn_w_out_b, j, h, tm=512)
        else:
            proj = _norm_matmul(h, nw, ssm_w_in_b, j, tm=1024, tn=1792)
            o = _ssd_core(proj, ssm_conv_w[j], ssm_conv_b[j], ssm_dt_bias[j], ssm_a_log[j],
                          ssm_d[j], ssm_norm[j], batch=batch, seq=seq)
            h = _matmul_res(o, ssm_w_out_b, j, h, tm=512)
        h = _mlp(h, row(norm_mlp[i]), w_up_b, w_down_b, i, tm=1024, tf=512)
        h = _ple(h, row(norm_ple[i]), w_gate_b, pf, w_proj_b, row(norm_final), i,
                 final_norm=(i == DEPTH - 1), tm=512, tn=512)
    return h.reshape(batch, seq, d)
```

```python
import functools
import math

import numpy as np
import jax
import jax.numpy as jnp
from jax import lax
from jax.experimental import pallas as pl
from jax.experimental.pallas import tpu as pltpu

F32 = jnp.float32
BF16 = jnp.bfloat16

D_MODEL = 2048
DEPTH = 4
N_MIXERS = 3
PLE_DIM = 256
D_FF = 4 * D_MODEL
EPS = 1e-6

GLA_HEADS = 4
GLA_DK = 256
GLA_DV = 512
GLA_GATE_RANK = 16
GLA_GATE_NORM = 16.0
HGRN_HEADS = 16
HGRN_DK = 128
HGRN_DV = 128
SSM_DINNER = 4096
SSM_HEADDIM = 64
SSM_HEADS = 64
SSM_GROUPS = 8
SSM_STATE = 128
SSM_CONV = 4
SSM_CONV_DIM = SSM_DINNER + 2 * SSM_GROUPS * SSM_STATE

LANES = 128
VMEM_LIMIT_BYTES = 56 * 1024 * 1024

GLA_CHUNK = 64
GLA_TOKENS_PER_STEP = 512
SSD_CHUNK = 128
CONV_HALO = 8

GLA_IN_COLS = 6400
SSM_IN_COLS = 10752
N_LEVELS = 6
LOG2E = math.log2(math.e)
MASKED_LOG2 = -1e30


def _cparams(sem):
    return pltpu.CompilerParams(dimension_semantics=sem, vmem_limit_bytes=VMEM_LIMIT_BYTES)


def _rms(x, w):
    ms = jnp.mean(x * x, axis=-1, keepdims=True)
    return x * lax.rsqrt(ms + EPS) * w


def _split3(x):
    hi = x.astype(BF16)
    r1 = x - hi.astype(F32)
    mid = r1.astype(BF16)
    lo = (r1 - mid.astype(F32)).astype(BF16)
    return hi, mid, lo


def _norm_matmul_kernel(h_ref, nw_ref, w_ref, o_ref, u_ref):
    @pl.when(pl.program_id(1) == 0)
    def _():
        u_ref[...] = _rms(h_ref[...], nw_ref[...]).astype(BF16)

    o_ref[...] = jnp.dot(u_ref[...], w_ref[...], preferred_element_type=F32).astype(o_ref.dtype)


def _norm_matmul(h, nw, w, layer, *, tm, tn):
    t, d = h.shape
    n = w.shape[2]
    return pl.pallas_call(
        _norm_matmul_kernel,
        grid=(t // tm, n // tn),
        in_specs=[
            pl.BlockSpec((tm, d), lambda i, j: (i, 0)),
            pl.BlockSpec((1, d), lambda i, j: (0, 0)),
            pl.BlockSpec((None, d, tn), lambda i, j: (layer, 0, j)),
        ],
        out_specs=pl.BlockSpec((tm, tn), lambda i, j: (i, j)),
        out_shape=jax.ShapeDtypeStruct((t, n), F32),
        scratch_shapes=[pltpu.VMEM((tm, d), BF16)],
        compiler_params=_cparams(("parallel", "arbitrary")),
        name="norm_matmul",
    )(h, nw, w)


def _matmul_res_kernel(a_ref, w_ref, h_ref, o_ref):
    o_ref[...] = h_ref[...] + jnp.dot(a_ref[...], w_ref[...], preferred_element_type=F32)


def _matmul_res(a, w, layer, h, *, tm):
    t, k = a.shape
    n = w.shape[2]
    return pl.pallas_call(
        _matmul_res_kernel,
        grid=(t // tm,),
        in_specs=[
            pl.BlockSpec((tm, k), lambda i: (i, 0)),
            pl.BlockSpec((None, k, n), lambda i: (layer, 0, 0), pipeline_mode=pl.Buffered(1)),
            pl.BlockSpec((tm, n), lambda i: (i, 0)),
        ],
        out_specs=pl.BlockSpec((tm, n), lambda i: (i, 0)),
        out_shape=jax.ShapeDtypeStruct((t, n), F32),
        compiler_params=_cparams(("parallel",)),
        name="matmul_res",
    )(a, w, h)


def _mlp_kernel(h_ref, nw_ref, wu_ref, wd_ref, o_ref, u_ref):
    @pl.when(pl.program_id(1) == 0)
    def _():
        x = h_ref[...]
        u_ref[...] = _rms(x, nw_ref[...]).astype(BF16)
        o_ref[...] = x

    a = jnp.dot(u_ref[...], wu_ref[...], preferred_element_type=F32)
    a = jnp.square(jnp.maximum(a, 0.0)).astype(BF16)
    o_ref[...] += jnp.dot(a, wd_ref[...], preferred_element_type=F32)


def _mlp(h, nw, wu, wd, layer, *, tm, tf):
    t, d = h.shape
    ff = wu.shape[2]
    return pl.pallas_call(
        _mlp_kernel,
        grid=(t // tm, ff // tf),
        in_specs=[
            pl.BlockSpec((tm, d), lambda i, f: (i, 0)),
            pl.BlockSpec((1, d), lambda i, f: (0, 0)),
            pl.BlockSpec((None, d, tf), lambda i, f: (layer, 0, f)),
            pl.BlockSpec((None, tf, d), lambda i, f: (layer, f, 0)),
        ],
        out_specs=pl.BlockSpec((tm, d), lambda i, f: (i, 0)),
        out_shape=jax.ShapeDtypeStruct((t, d), F32),
        scratch_shapes=[pltpu.VMEM((tm, d), BF16)],
        compiler_params=_cparams(("parallel", "arbitrary")),
        name="mlp",
    )(h, nw, wu, wd)


def _ple_kernel(h_ref, nw_ref, wg_ref, p_ref, wp_ref, nf_ref, o_ref, *, final_norm, tn):
    x = h_ref[...]
    u = _rms(x, nw_ref[...]).astype(BF16)
    pb = p_ref[...].astype(BF16)
    d = x.shape[1]
    for n0 in range(0, d, tn):
        cs = slice(n0, n0 + tn)
        gate = jax.nn.sigmoid(jnp.dot(u, wg_ref[:, cs], preferred_element_type=F32))
        proj = jnp.dot(pb, wp_ref[:, cs], preferred_element_type=F32)
        o_ref[:, cs] = h_ref[:, cs] + gate * proj
    if final_norm:
        o_ref[...] = _rms(o_ref[...], nf_ref[...])


def _ple(h, nw, wg, p, wp, nf, layer, *, final_norm, tm, tn):
    t, d = h.shape
    pd = p.shape[2]
    return pl.pallas_call(
        functools.partial(_ple_kernel, final_norm=final_norm, tn=tn),
        grid=(t // tm,),
        in_specs=[
            pl.BlockSpec((tm, d), lambda i: (i, 0)),
            pl.BlockSpec((1, d), lambda i: (0, 0)),
            pl.BlockSpec((None, d, d), lambda i: (layer, 0, 0)),
            pl.BlockSpec((None, tm, pd), lambda i: (layer, i, 0)),
            pl.BlockSpec((None, pd, d), lambda i: (layer, 0, 0)),
            pl.BlockSpec((1, d), lambda i: (0, 0)),
        ],
        out_specs=pl.BlockSpec((tm, d), lambda i: (i, 0)),
        out_shape=jax.ShapeDtypeStruct((t, d), F32),
        compiler_params=_cparams(("parallel",)),
        name="ple",
    )(h, nw, wg, p, wp, nf)


MATMUL_LEVELS = (3, 4)


def _gla_arg_matrix():
    c = GLA_CHUNK
    t = np.arange(c)[:, None]
    r = np.arange(c)[None, :]
    blocks = [(r <= t)]
    for lvl in MATMUL_LEVELS:
        n = c >> (lvl + 1)
        ref = (t // (2 * n)) * (2 * n) + n - 1
        lower = t > ref
        blocks.append(np.where(lower, (r > ref) & (r <= t), (r > t) & (r <= ref)))
    m = np.concatenate(blocks, axis=0).astype(np.float32)
    return np.concatenate([m, m, m, np.zeros_like(m)], axis=1)


def _coarse_level_args(b, n):
    out = []
    for blk in range(GLA_CHUNK // (2 * n)):
        r0 = blk * 2 * n
        ref = b[r0 + n - 1:r0 + n]
        out.append(ref - b[r0:r0 + n])
        out.append(b[r0 + n:r0 + 2 * n] - ref)
    return jnp.concatenate(out, axis=0)


def _gla_masks(pack):
    c = GLA_CHUNK
    t = np.arange(c)[:, None]
    s = np.arange(c)[None, :]
    masks = np.zeros((1 + N_LEVELS, c, c), np.float32)
    masks[0] = (t == s)
    for lvl in range(N_LEVELS):
        n = c >> (lvl + 1)
        same = (t // (2 * n)) == (s // (2 * n))
        masks[1 + lvl] = same & ((t % (2 * n)) >= n) & ((s % (2 * n)) < n)
    return np.tile(masks, (1, 1, pack))


def _block_diag_rows(x, pack, width):
    if pack == 1:
        return x
    head = lax.broadcasted_iota(jnp.int32, x.shape, 1) // width
    return jnp.concatenate([jnp.where(head == p, x, jnp.zeros_like(x)) for p in range(pack)], axis=0)


_NT = (((1,), (1,)), ((), ()))
_TN = (((0,), (0,)), ((), ()))


def _gla_core_kernel(*refs, mode, layer, scale, packs, pack, dk, dv):
    wk = pack * dk
    wv = pack * dv
    if mode == "gla":
        (q_ref, k_ref, v_ref, og_ref, glr_ref, wgk_ref, bgk_ref, gn_ref, mall_ref, masks_ref,
         o_ref, st_ref) = refs
    else:
        (q_ref, k_ref, v_ref, og_ref, lbl_ref, gn_ref, mall_ref, masks_ref, o_ref, st_ref) = refs

    @pl.when(pl.program_id(2) == 0)
    def _():
        st_ref[...] = jnp.zeros_like(st_ref)

    if mode == "hgrn":
        lg = lbl_ref[...]
        e = jnp.exp(lg - jnp.max(lg, axis=0, keepdims=True))
        sm = e / jnp.sum(e, axis=0, keepdims=True)
        lb_all = jnp.zeros_like(sm[0:1])
        for r in range(1, layer + 1):
            lb_all = lb_all + sm[r:r + 1]

    c = GLA_CHUNK
    n_chunks = q_ref.shape[0] // c
    odd_row = lax.broadcasted_iota(jnp.int32, (c, wk), 0) % 2 == 1
    if pack > 1:
        st_diag = (lax.broadcasted_iota(jnp.int32, (wv, wk), 0) // dv
                   == lax.broadcasted_iota(jnp.int32, (wv, wk), 1) // dk)

    def chunk(ci, carry):
        rows = pl.ds(pl.multiple_of(ci * c, c), c)
        if mode == "gla":
            z_all = jnp.dot(glr_ref[rows, :].astype(BF16), wgk_ref[...],
                            preferred_element_type=F32) + bgk_ref[...]
        for g in range(packs):
            ks = slice(g * wk, (g + 1) * wk)
            vs = slice(g * wv, (g + 1) * wv)
            vc = v_ref[rows, vs]
            og = og_ref[rows, vs]
            if mode == "gla":
                qc = q_ref[rows, ks] * scale
                kc = k_ref[rows, ks]
                z = z_all[:, ks]
                gc = (jnp.minimum(z, 0.0) - jnp.log1p(jnp.exp(-jnp.abs(z)))) * (LOG2E / GLA_GATE_NORM)
                gate = og * jax.nn.sigmoid(og)
            else:
                qq = q_ref[rows, ks]
                qc = qq * jax.nn.sigmoid(qq) * scale
                lb = lb_all[:, ks]
                f = lb + (1.0 - lb) * jax.nn.sigmoid(k_ref[rows, ks])
                kc = 1.0 - f
                gc = jnp.log(f) * LOG2E
                gate = jax.nn.sigmoid(og)

            g_hi, g_mid, g_lo = _split3(gc)
            gs = jnp.concatenate([g_hi, g_mid, g_lo, jnp.zeros_like(g_hi)], axis=0)
            pre = jnp.dot(mall_ref[...], gs, preferred_element_type=F32)
            b = pre[0:c]
            lvl_args = [_coarse_level_args(b, c >> (lvl + 1)) for lvl in range(MATMUL_LEVELS[0])]
            lvl_args += [pre[c:2 * c], pre[2 * c:3 * c], jnp.where(odd_row, gc, 0.0)]
            e_b = jnp.exp2(b)

            st = st_ref[g]
            q_in = (qc * e_b).astype(BF16)
            o = lax.dot_general(q_in, st.astype(BF16), _NT, preferred_element_type=F32)
            k_dec = (kc * jnp.exp2(b[c - 1:c] - b)).astype(BF16)

            att = masks_ref[0] * lax.dot_general(
                qc.astype(BF16), _block_diag_rows(kc.astype(BF16), pack, dk), _NT,
                preferred_element_type=F32)
            for lvl in range(N_LEVELS):
                el = jnp.exp2(lvl_args[lvl])
                ql = (qc * el).astype(BF16)
                kl = _block_diag_rows((kc * el).astype(BF16), pack, dk)
                att = att + masks_ref[1 + lvl] * lax.dot_general(ql, kl, _NT,
                                                                 preferred_element_type=F32)
            vb = vc.astype(BF16)
            o = o + jnp.dot(att.astype(BF16), _block_diag_rows(vb, pack, dv),
                            preferred_element_type=F32)

            e_last = e_b[c - 1:c]
            upd = lax.dot_general(vb, k_dec, _TN, preferred_element_type=F32)
            if pack > 1:
                upd = jnp.where(st_diag, upd, 0.0)
            st_ref[g] = st * e_last + upd

            for p in range(pack):
                hs = slice(p * dv, (p + 1) * dv)
                o_ref[rows, g * wv + p * dv:g * wv + (p + 1) * dv] = (
                    _rms(o[:, hs], gn_ref[...]) * gate[:, hs]).astype(o_ref.dtype)
        return carry

    lax.fori_loop(0, n_chunks, chunk, 0, unroll=4)


def _gla_core(proj, *, mode, layer, batch, seq, heads, hps, pack, dk, dv, col_q, col_k, col_v,
              col_og, extra, gn):
    tl = GLA_TOKENS_PER_STEP
    nl = seq // tl
    t = batch * seq
    wk = hps * dk
    wv = hps * dv
    mall = jnp.asarray(_gla_arg_matrix(), BF16)
    masks = jnp.asarray(_gla_masks(pack), F32)

    def rowblk(b, h, l):
        return b * nl + l

    in_specs = [
        pl.BlockSpec((tl, wk), lambda b, h, l: (rowblk(b, h, l), col_q // wk + h)),
        pl.BlockSpec((tl, wk), lambda b, h, l: (rowblk(b, h, l), col_k // wk + h)),
        pl.BlockSpec((tl, wv), lambda b, h, l: (rowblk(b, h, l), col_v // wv + h)),
        pl.BlockSpec((tl, wv), lambda b, h, l: (rowblk(b, h, l), col_og // wv + h)),
    ]
    args = [proj, proj, proj, proj]
    if mode == "gla":
        col_glr, wgk, bgk = extra
        in_specs += [
            pl.BlockSpec((tl, LANES), lambda b, h, l: (rowblk(b, h, l), col_glr // LANES)),
            pl.BlockSpec((LANES, wk), lambda b, h, l: (0, h)),
            pl.BlockSpec((1, wk), lambda b, h, l: (0, h)),
        ]
        args += [proj, wgk, bgk]
    else:
        (lbl,) = extra
        in_specs += [pl.BlockSpec((DEPTH, wk), lambda b, h, l: (0, h))]
        args += [lbl]
    in_specs += [
        pl.BlockSpec((1, dv), lambda b, h, l: (0, 0)),
        pl.BlockSpec(mall.shape, lambda b, h, l: (0, 0)),
        pl.BlockSpec(masks.shape, lambda b, h, l: (0, 0, 0)),
    ]
    args += [gn, mall, masks]
    return pl.pallas_call(
        functools.partial(_gla_core_kernel, mode=mode, layer=layer, scale=dk ** -0.5,
                          packs=hps // pack, pack=pack, dk=dk, dv=dv),
        grid=(batch, heads // hps, nl),
        in_specs=in_specs,
        out_specs=pl.BlockSpec((tl, wv), lambda b, h, l: (rowblk(b, h, l), h)),
        out_shape=jax.ShapeDtypeStruct((t, heads * dv), BF16),
        scratch_shapes=[pltpu.VMEM((hps // pack, pack * dv, pack * dk), F32)],
        compiler_params=_cparams(("parallel", "parallel", "arbitrary")),
        name=mode + "_core",
    )(*args)


N_PAIRS = SSM_HEADS // 2
PAIRS_PER_GROUP = N_PAIRS // SSM_GROUPS


def _ssd_kernel(z_ref, x_ref, bm_ref, cm_ref, dt_ref, cw_ref, cb_ref, dtb_ref, alog_ref, dsk_ref,
                nw_ref, tri3_ref, o_ref,
                ext_ref, xs_ref, bs_ref, cs_ref, st_ref, y_ref, bt_ref, dtt_ref, wdt_ref, elb_ref):
    c = SSD_CHUNK
    halo = CONV_HALO
    l = pl.program_id(1)

    @pl.when(l == 0)
    def _():
        st_ref[...] = jnp.zeros_like(st_ref)
        ext_ref[0:halo, :] = jnp.zeros((halo, ext_ref.shape[1]), F32)

    @pl.when(l > 0)
    def _():
        ext_ref[0:halo, :] = ext_ref[c:c + halo, :]

    ext_ref[halo:halo + c, 0:SSM_DINNER] = x_ref[...]
    ext_ref[halo:halo + c, SSM_DINNER:SSM_DINNER + 1024] = bm_ref[...]
    ext_ref[halo:halo + c, SSM_DINNER + 1024:SSM_CONV_DIM] = cm_ref[...]

    for slab in range(SSM_CONV_DIM // LANES):
        cs = slice(slab * LANES, (slab + 1) * LANES)
        acc = cb_ref[:, cs] + cw_ref[0:1, cs] * ext_ref[halo - 3:halo - 3 + c, cs]
        for j in range(1, SSM_CONV):
            acc = acc + cw_ref[j:j + 1, cs] * ext_ref[halo - 3 + j:halo - 3 + j + c, cs]
        act = acc * jax.nn.sigmoid(acc)
        if slab < N_PAIRS:
            xs_ref[slab] = act
        elif slab < N_PAIRS + SSM_GROUPS:
            bs_ref[slab - N_PAIRS] = act
        else:
            cs_ref[slab - N_PAIRS - SSM_GROUPS] = act

    lane = lax.broadcasted_iota(jnp.int32, (1, LANES), 1)
    dtr = dt_ref[...] + dtb_ref[...]
    dt = jnp.maximum(dtr, 0.0) + jnp.log1p(jnp.exp(-jnp.abs(dtr)))
    a = jnp.where(lane < SSM_HEADS, -jnp.exp(alog_ref[...]), 0.0)
    la_hi, la_mid, la_lo = _split3(dt * (a * LOG2E))
    b = jnp.dot(tri3_ref[...], jnp.concatenate([la_hi, la_mid, la_lo], axis=0),
                preferred_element_type=F32)
    b_last = b[c - 1:c]
    wd = jnp.exp2(b_last - b) * dt
    bt = b.T
    bt_ref[...] = bt
    dtt_ref[...] = dt.T
    wdt_ref[...] = wd.T
    elb_ref[...] = jnp.broadcast_to(jnp.exp2(bt[:, c - 1:c]), (LANES, LANES))

    row = lax.broadcasted_iota(jnp.int32, (c, c), 0)
    col = lax.broadcasted_iota(jnp.int32, (c, c), 1)
    causal = col <= row
    lo = lax.broadcasted_iota(jnp.int32, (c, LANES), 1) < SSM_HEADDIM

    def group(g, carry):
        bg = bs_ref[g]
        cg = cs_ref[g]
        cgb = cg.astype(BF16)
        cbm = lax.dot_general(cgb, bg.astype(BF16), _NT, preferred_element_type=F32)
        bgt = bg.T
        for j in range(PAIRS_PER_GROUP):
            p = g * PAIRS_PER_GROUP + j
            xp = xs_ref[p]
            sp = st_ref[p]
            x_lo = jnp.where(lo, xp, 0.0).astype(BF16)
            x_hi = jnp.where(lo, 0.0, xp).astype(BF16)
            x_bd = jnp.concatenate([x_lo, x_hi], axis=0)
            lhs_y = []
            lhs_s = []
            bhs = []
            for k in range(2):
                h = 2 * p + k
                bh = jnp.broadcast_to(bt_ref[pl.ds(h, 1), :], (c, c)).T
                rel = bh - bt_ref[pl.ds(h, 1), :]
                dec = jnp.exp2(jnp.where(causal, rel, MASKED_LOG2))
                lhs_y.append((cbm * dec * dtt_ref[pl.ds(h, 1), :]).astype(BF16))
                lhs_s.append((bgt * wdt_ref[pl.ds(h, 1), :]).astype(BF16))
                bhs.append(bh)
            y = jnp.exp2(jnp.where(lo, bhs[0], bhs[1])) * jnp.dot(
                cgb, sp.astype(BF16), preferred_element_type=F32)
            y = y + jnp.dot(jnp.concatenate(lhs_y, axis=1), x_bd, preferred_element_type=F32)
            su = jnp.dot(jnp.concatenate(lhs_s, axis=1), x_bd,
                         preferred_element_type=F32)
            el = jnp.where(lane < SSM_HEADDIM, elb_ref[pl.ds(2 * p, 1), :],
                           elb_ref[pl.ds(2 * p + 1, 1), :])
            st_ref[p] = sp * el + su
            y_ref[p] = y + dsk_ref[p] * xp
        return carry

    lax.fori_loop(0, SSM_GROUPS, group, 0, unroll=8)

    gw = PAIRS_PER_GROUP * LANES
    for g in range(SSM_GROUPS):
        cs = slice(g * gw, (g + 1) * gw)
        yg = jnp.concatenate([y_ref[g * PAIRS_PER_GROUP + j] for j in range(PAIRS_PER_GROUP)],
                             axis=1)
        zg = z_ref[:, cs]
        yg = yg * (zg * jax.nn.sigmoid(zg))
        o_ref[:, cs] = _rms(yg, nw_ref[:, cs]).astype(o_ref.dtype)


def _ssd_core(proj, conv_w, conv_b, dt_bias, a_log, d_skip, norm_w, *, batch, seq):
    c = SSD_CHUNK
    nl = seq // c
    t = batch * seq
    tri = np.tril(np.ones((c, c), np.float32))
    tri3 = jnp.asarray(np.concatenate([tri, tri, tri], axis=1), BF16)
    pad = LANES - SSM_HEADS
    dtb = jnp.pad(dt_bias.astype(F32), (0, pad)).reshape(1, LANES)
    alog = jnp.pad(a_log.astype(F32), (0, pad)).reshape(1, LANES)
    dsk = jnp.repeat(d_skip.astype(F32), SSM_HEADDIM).reshape(N_PAIRS, 1, LANES)

    def rb(b, l):
        return b * nl + l

    full2 = lambda b, l: (0, 0)
    in_specs = [
        pl.BlockSpec((c, SSM_DINNER), lambda b, l: (rb(b, l), 0)),
        pl.BlockSpec((c, SSM_DINNER), lambda b, l: (rb(b, l), 1)),
        pl.BlockSpec((c, 1024), lambda b, l: (rb(b, l), 2 * SSM_DINNER // 1024)),
        pl.BlockSpec((c, 1024), lambda b, l: (rb(b, l), 2 * SSM_DINNER // 1024 + 1)),
        pl.BlockSpec((c, LANES), lambda b, l: (rb(b, l), (SSM_DINNER + SSM_CONV_DIM) // LANES)),
        pl.BlockSpec((SSM_CONV, SSM_CONV_DIM), full2),
        pl.BlockSpec((1, SSM_CONV_DIM), full2),
        pl.BlockSpec((1, LANES), full2),
        pl.BlockSpec((1, LANES), full2),
        pl.BlockSpec((N_PAIRS, 1, LANES), lambda b, l: (0, 0, 0)),
        pl.BlockSpec((1, SSM_DINNER), full2),
        pl.BlockSpec(tri3.shape, full2),
    ]
    scratch = [
        pltpu.VMEM((CONV_HALO + c, SSM_CONV_DIM), F32),
        pltpu.VMEM((N_PAIRS, c, LANES), F32),
        pltpu.VMEM((SSM_GROUPS, c, SSM_STATE), F32),
        pltpu.VMEM((SSM_GROUPS, c, SSM_STATE), F32),
        pltpu.VMEM((N_PAIRS, SSM_STATE, LANES), F32),
        pltpu.VMEM((N_PAIRS, c, LANES), F32),
        pltpu.VMEM((LANES, c), F32),
        pltpu.VMEM((LANES, c), F32),
        pltpu.VMEM((LANES, c), F32),
        pltpu.VMEM((LANES, LANES), F32),
    ]
    return pl.pallas_call(
        _ssd_kernel,
        grid=(batch, nl),
        in_specs=in_specs,
        out_specs=pl.BlockSpec((c, SSM_DINNER), lambda b, l: (rb(b, l), 0)),
        out_shape=jax.ShapeDtypeStruct((t, SSM_DINNER), BF16),
        scratch_shapes=scratch,
        compiler_params=_cparams(("parallel", "arbitrary")),
        name="ssd_core",
    )(proj, proj, proj, proj, proj, conv_w.astype(F32), conv_b.astype(F32).reshape(1, -1),
      dtb, alog, dsk, norm_w.astype(F32).reshape(1, -1), tri3)


def _pad_last(w, n):
    return jnp.pad(w, [(0, 0)] * (w.ndim - 1) + [(0, n - w.shape[-1])])


def kernel(x, p, norm_mix, norm_mlp, norm_ple, norm_final, w_up, w_down, w_ple_proj, w_ple_gate,
           gla_w_in, gla_w_gk2, gla_b_gk, gla_gn, gla_w_out,
           hgrn_lb_logits, hgrn_w_in, hgrn_gn, hgrn_w_out,
           ssm_w_in, ssm_conv_w, ssm_conv_b, ssm_dt_bias, ssm_a_log, ssm_d, ssm_norm, ssm_w_out):
    batch, seq, d = x.shape
    t = batch * seq
    h = x.reshape(t, d)
    pf = p.reshape(DEPTH, t, PLE_DIM)
    row = lambda v: v.astype(F32).reshape(1, -1)

    gla_w_in_b = _pad_last(gla_w_in, GLA_IN_COLS).astype(BF16)
    hgrn_w_in_b = hgrn_w_in.astype(BF16)
    ssm_w_in_b = _pad_last(ssm_w_in, SSM_IN_COLS).astype(BF16)
    gla_w_out_b = gla_w_out.astype(BF16)
    hgrn_w_out_b = hgrn_w_out.astype(BF16)
    ssm_w_out_b = ssm_w_out.astype(BF16)
    w_up_b = w_up.astype(BF16)
    w_down_b = w_down.astype(BF16)
    w_gate_b = w_ple_gate.astype(BF16)
    w_proj_b = w_ple_proj.astype(BF16)

    for i in range(DEPTH):
        kind, j = i % N_MIXERS, i // N_MIXERS
        nw = row(norm_mix[i])
        if kind == 0:
            proj = _norm_matmul(h, nw, gla_w_in_b, j, tm=1024, tn=1280)
            kd = GLA_HEADS * GLA_DK
            vd = GLA_HEADS * GLA_DV
            wgk = jnp.pad(gla_w_gk2[j], ((0, LANES - GLA_GATE_RANK), (0, 0))).astype(BF16)
            o = _gla_core(proj, mode="gla", layer=i, batch=batch, seq=seq, heads=GLA_HEADS,
                          hps=GLA_HEADS, pack=1, dk=GLA_DK, dv=GLA_DV, col_q=0, col_k=kd, col_v=2 * kd,
                          col_og=2 * kd + vd,
                          extra=(2 * kd + 2 * vd, wgk, row(gla_b_gk[j])), gn=row(gla_gn[j]))
            h = _matmul_res(o, gla_w_out_b, j, h, tm=512)
        elif kind == 1:
            proj = _norm_matmul(h, nw, hgrn_w_in_b, j, tm=1024, tn=1024)
            fd = HGRN_HEADS * HGRN_DK
            vd = HGRN_HEADS * HGRN_DV
            o = _gla_core(proj, mode="hgrn", layer=i, batch=batch, seq=seq, heads=HGRN_HEADS,
                          hps=HGRN_HEADS // 2, pack=2, dk=HGRN_DK, dv=HGRN_DV, col_q=0, col_k=fd,
                          col_v=2 * fd, col_og=2 * fd + vd,
                          extra=(hgrn_lb_logits.astype(F32),), gn=row(hgrn_gn[j]))
            h = _matmul_res(o, hgrn_w_out_b, j, h, tm=512)
        else:
            proj = _norm_matmul(h, nw, ssm_w_in_b, j, tm=1024, tn=1792)
            o = _ssd_core(proj, ssm_conv_w[j], ssm_conv_b[j], ssm_dt_bias[j], ssm_a_log[j],
                          ssm_d[j], ssm_norm[j], batch=batch, seq=seq)
            h = _matmul_res(o, ssm_w_out_b, j, h, tm=512)
        h = _mlp(h, row(norm_mlp[i]), w_up_b, w_down_b, i, tm=1024, tf=512)
        h = _ple(h, row(norm_ple[i]), w_gate_b, pf, w_proj_b, row(norm_final), i,
                 final_norm=(i == DEPTH - 1), tm=512, tn=512)
    return h.reshape(batch, seq, d)
```

```python
import functools
import math

import numpy as np
import jax
import jax.numpy as jnp
from jax import lax
from jax.experimental import pallas as pl
from jax.experimental.pallas import tpu as pltpu

F32 = jnp.float32
BF16 = jnp.bfloat16

D_MODEL = 2048
DEPTH = 4
N_MIXERS = 3
PLE_DIM = 256
D_FF = 4 * D_MODEL
EPS = 1e-6

GLA_HEADS = 4
GLA_DK = 256
GLA_DV = 512
GLA_GATE_RANK = 16
GLA_GATE_NORM = 16.0
HGRN_HEADS = 16
HGRN_DK = 128
HGRN_DV = 128
SSM_DINNER = 4096
SSM_HEADDIM = 64
SSM_HEADS = 64
SSM_GROUPS = 8
SSM_STATE = 128
SSM_CONV = 4
SSM_CONV_DIM = SSM_DINNER + 2 * SSM_GROUPS * SSM_STATE

LANES = 128
VMEM_LIMIT_BYTES = 56 * 1024 * 1024

GLA_CHUNK = 64
GLA_TOKENS_PER_STEP = 512
SSD_CHUNK = 128
CONV_HALO = 8

GLA_IN_COLS = 6400
SSM_IN_COLS = 10752
N_LEVELS = 6
LOG2E = math.log2(math.e)
MASKED_LOG2 = -1e30


def _cparams(sem):
    return pltpu.CompilerParams(dimension_semantics=sem, vmem_limit_bytes=VMEM_LIMIT_BYTES)


def _rms(x, w):
    ms = jnp.mean(x * x, axis=-1, keepdims=True)
    return x * lax.rsqrt(ms + EPS) * w


def _split3(x):
    hi = x.astype(BF16)
    r1 = x - hi.astype(F32)
    mid = r1.astype(BF16)
    lo = (r1 - mid.astype(F32)).astype(BF16)
    return hi, mid, lo


def _norm_matmul_kernel(h_ref, nw_ref, w_ref, o_ref, u_ref):
    @pl.when(pl.program_id(1) == 0)
    def _():
        u_ref[...] = _rms(h_ref[...], nw_ref[...]).astype(BF16)

    o_ref[...] = jnp.dot(u_ref[...], w_ref[...], preferred_element_type=F32).astype(o_ref.dtype)


def _norm_matmul(h, nw, w, layer, *, tm, tn):
    t, d = h.shape
    n = w.shape[2]
    return pl.pallas_call(
        _norm_matmul_kernel,
        grid=(t // tm, n // tn),
        in_specs=[
            pl.BlockSpec((tm, d), lambda i, j: (i, 0)),
            pl.BlockSpec((1, d), lambda i, j: (0, 0)),
            pl.BlockSpec((None, d, tn), lambda i, j: (layer, 0, j)),
        ],
        out_specs=pl.BlockSpec((tm, tn), lambda i, j: (i, j)),
        out_shape=jax.ShapeDtypeStruct((t, n), F32),
        scratch_shapes=[pltpu.VMEM((tm, d), BF16)],
        compiler_params=_cparams(("parallel", "arbitrary")),
        name="norm_matmul",
    )(h, nw, w)


def _matmul_res_kernel(a_ref, w_ref, h_ref, o_ref):
    o_ref[...] = h_ref[...] + jnp.dot(a_ref[...], w_ref[...], preferred_element_type=F32)


def _matmul_res(a, w, layer, h, *, tm):
    t, k = a.shape
    n = w.shape[2]
    return pl.pallas_call(
        _matmul_res_kernel,
        grid=(t // tm,),
        in_specs=[
            pl.BlockSpec((tm, k), lambda i: (i, 0)),
            pl.BlockSpec((None, k, n), lambda i: (layer, 0, 0), pipeline_mode=pl.Buffered(1)),
            pl.BlockSpec((tm, n), lambda i: (i, 0)),
        ],
        out_specs=pl.BlockSpec((tm, n), lambda i: (i, 0)),
        out_shape=jax.ShapeDtypeStruct((t, n), F32),
        compiler_params=_cparams(("parallel",)),
        name="matmul_res",
    )(a, w, h)


def _mlp_kernel(h_ref, nw_ref, wu_ref, wd_ref, o_ref, u_ref):
    @pl.when(pl.program_id(1) == 0)
    def _():
        x = h_ref[...]
        u_ref[...] = _rms(x, nw_ref[...]).astype(BF16)
        o_ref[...] = x

    a = jnp.dot(u_ref[...], wu_ref[...], preferred_element_type=F32)
    a = jnp.square(jnp.maximum(a, 0.0)).astype(BF16)
    o_ref[...] += jnp.dot(a, wd_ref[...], preferred_element_type=F32)


def _mlp(h, nw, wu, wd, layer, *, tm, tf):
    t, d = h.shape
    ff = wu.shape[2]
    return pl.pallas_call(
        _mlp_kernel,
        grid=(t // tm, ff // tf),
        in_specs=[
            pl.BlockSpec((tm, d), lambda i, f: (i, 0)),
            pl.BlockSpec((1, d), lambda i, f: (0, 0)),
            pl.BlockSpec((None, d, tf), lambda i, f: (layer, 0, f)),
            pl.BlockSpec((None, tf, d), lambda i, f: (layer, f, 0)),
        ],
        out_specs=pl.BlockSpec((tm, d), lambda i, f: (i, 0)),
        out_shape=jax.ShapeDtypeStruct((t, d), F32),
        scratch_shapes=[pltpu.VMEM((tm, d), BF16)],
        compiler_params=_cparams(("parallel", "arbitrary")),
        name="mlp",
    )(h, nw, wu, wd)


def _ple_kernel(h_ref, nw_ref, wg_ref, p_ref, wp_ref, nf_ref, o_ref, *, final_norm, tn):
    x = h_ref[...]
    u = _rms(x, nw_ref[...]).astype(BF16)
    pb = p_ref[...].astype(BF16)
    d = x.shape[1]
    for n0 in range(0, d, tn):
        cs = slice(n0, n0 + tn)
        gate = jax.nn.sigmoid(jnp.dot(u, wg_ref[:, cs], preferred_element_type=F32))
        proj = jnp.dot(pb, wp_ref[:, cs], preferred_element_type=F32)
        o_ref[:, cs] = h_ref[:, cs] + gate * proj
    if final_norm:
        o_ref[...] = _rms(o_ref[...], nf_ref[...])


def _ple(h, nw, wg, p, wp, nf, layer, *, final_norm, tm, tn):
    t, d = h.shape
    pd = p.shape[2]
    return pl.pallas_call(
        functools.partial(_ple_kernel, final_norm=final_norm, tn=tn),
        grid=(t // tm,),
        in_specs=[
            pl.BlockSpec((tm, d), lambda i: (i, 0)),
            pl.BlockSpec((1, d), lambda i: (0, 0)),
            pl.BlockSpec((None, d, d), lambda i: (layer, 0, 0)),
            pl.BlockSpec((None, tm, pd), lambda i: (layer, i, 0)),
            pl.BlockSpec((None, pd, d), lambda i: (layer, 0, 0)),
            pl.BlockSpec((1, d), lambda i: (0, 0)),
        ],
        out_specs=pl.BlockSpec((tm, d), lambda i: (i, 0)),
        out_shape=jax.ShapeDtypeStruct((t, d), F32),
        compiler_params=_cparams(("parallel",)),
        name="ple",
    )(h, nw, wg, p, wp, nf)


MATMUL_LEVELS = (3, 4)


def _gla_arg_matrix():
    c = GLA_CHUNK
    t = np.arange(c)[:, None]
    r = np.arange(c)[None, :]
    blocks = [(r <= t)]
    for lvl in MATMUL_LEVELS:
        n = c >> (lvl + 1)
        ref = (t // (2 * n)) * (2 * n) + n - 1
        lower = t > ref
        blocks.append(np.where(lower, (r > ref) & (r <= t), (r > t) & (r <= ref)))
    m = np.concatenate(blocks, axis=0).astype(np.float32)
    return np.concatenate([m, m, m, np.zeros_like(m)], axis=1)


def _coarse_level_args(b, n):
    out = []
    for blk in range(GLA_CHUNK // (2 * n)):
        r0 = blk * 2 * n
        ref = b[r0 + n - 1:r0 + n]
        out.append(ref - b[r0:r0 + n])
        out.append(b[r0 + n:r0 + 2 * n] - ref)
    return jnp.concatenate(out, axis=0)


def _gla_masks(pack):
    c = GLA_CHUNK
    t = np.arange(c)[:, None]
    s = np.arange(c)[None, :]
    masks = np.zeros((1 + N_LEVELS, c, c), np.float32)
    masks[0] = (t == s)
    for lvl in range(N_LEVELS):
        n = c >> (lvl + 1)
        same = (t // (2 * n)) == (s // (2 * n))
        masks[1 + lvl] = same & ((t % (2 * n)) >= n) & ((s % (2 * n)) < n)
    return np.tile(masks, (1, 1, pack))


def _block_diag_rows(x, pack, width):
    if pack == 1:
        return x
    head = lax.broadcasted_iota(jnp.int32, x.shape, 1) // width
    return jnp.concatenate([jnp.where(head == p, x, jnp.zeros_like(x)) for p in range(pack)], axis=0)


_NT = (((1,), (1,)), ((), ()))
_TN = (((0,), (0,)), ((), ()))


def _gla_core_kernel(*refs, mode, layer, scale, packs, pack, dk, dv):
    wk = pack * dk
    wv = pack * dv
    if mode == "gla":
        (q_ref, k_ref, v_ref, og_ref, glr_ref, wgk_ref, bgk_ref, gn_ref, mall_ref, masks_ref,
         o_ref, st_ref) = refs
    else:
        (q_ref, k_ref, v_ref, og_ref, lbl_ref, gn_ref, mall_ref, masks_ref, o_ref, st_ref) = refs

    @pl.when(pl.program_id(2) == 0)
    def _():
        st_ref[...] = jnp.zeros_like(st_ref)

    if mode == "hgrn":
        lg = lbl_ref[...]
        e = jnp.exp(lg - jnp.max(lg, axis=0, keepdims=True))
        sm = e / jnp.sum(e, axis=0, keepdims=True)
        lb_all = jnp.zeros_like(sm[0:1])
        for r in range(1, layer + 1):
            lb_all = lb_all + sm[r:r + 1]

    c = GLA_CHUNK
    c2 = 2 * c
    n_chunks = q_ref.shape[0] // c2
    odd_row = lax.broadcasted_iota(jnp.int32, (c, wk), 0) % 2 == 1
    if pack > 1:
        st_diag = (lax.broadcasted_iota(jnp.int32, (wv, wk), 0) // dv
                   == lax.broadcasted_iota(jnp.int32, (wv, wk), 1) // dk)

    def nt(a, b):
        return lax.dot_general(a, b, _NT, preferred_element_type=F32)

    def half_terms(qh, kh, gh):
        g_hi, g_mid, g_lo = _split3(gh)
        gs = jnp.concatenate([g_hi, g_mid, g_lo, jnp.zeros_like(g_hi)], axis=0)
        pre = jnp.dot(mall_ref[...], gs, preferred_element_type=F32)
        b = pre[0:c]
        lvl_args = [_coarse_level_args(b, c >> (lvl + 1)) for lvl in range(MATMUL_LEVELS[0])]
        lvl_args += [pre[c:2 * c], pre[2 * c:3 * c], jnp.where(odd_row, gh, 0.0)]
        att = masks_ref[0] * nt(qh.astype(BF16), _block_diag_rows(kh.astype(BF16), pack, dk))
        for lvl in range(N_LEVELS):
            el = jnp.exp2(lvl_args[lvl])
            att = att + masks_ref[1 + lvl] * nt(
                (qh * el).astype(BF16), _block_diag_rows((kh * el).astype(BF16), pack, dk))
        return b, att

    def chunk(ci, carry):
        rows = pl.ds(pl.multiple_of(ci * c2, c2), c2)
        if mode == "gla":
            z_all = jnp.dot(glr_ref[rows, :].astype(BF16), wgk_ref[...],
                            preferred_element_type=F32) + bgk_ref[...]
        for g in range(packs):
            ks = slice(g * wk, (g + 1) * wk)
            vs = slice(g * wv, (g + 1) * wv)
            vc = v_ref[rows, vs]
            og = og_ref[rows, vs]
            if mode == "gla":
                qc = q_ref[rows, ks] * scale
                kc = k_ref[rows, ks]
                z = z_all[:, ks]
                gc = (jnp.minimum(z, 0.0) - jnp.log1p(jnp.exp(-jnp.abs(z)))) * (LOG2E / GLA_GATE_NORM)
                gate = og * jax.nn.sigmoid(og)
            else:
                qq = q_ref[rows, ks]
                qc = qq * jax.nn.sigmoid(qq) * scale
                lb = lb_all[:, ks]
                f = lb + (1.0 - lb) * jax.nn.sigmoid(k_ref[rows, ks])
                kc = 1.0 - f
                gc = jnp.log(f) * LOG2E
                gate = jax.nn.sigmoid(og)

            q0, q1 = qc[0:c], qc[c:c2]
            k0, k1 = kc[0:c], kc[c:c2]
            b0, att0 = half_terms(q0, k0, gc[0:c])
            b1, att1 = half_terms(q1, k1, gc[c:c2])
            e_b0 = jnp.exp2(b0)
            e_b1 = jnp.exp2(b1)
            e_rev0 = jnp.exp2(b0[c - 1:c] - b0)
            e_rev1 = jnp.exp2(b1[c - 1:c] - b1)
            e_l0 = e_b0[c - 1:c]
            e_l1 = e_b1[c - 1:c]

            k0_end = (k0 * e_rev0).astype(BF16)
            cross = nt((q1 * e_b1).astype(BF16), _block_diag_rows(k0_end, pack, dk))

            st = st_ref[g]
            q_in = jnp.concatenate([q0 * e_b0, q1 * (e_b1 * e_l0)], axis=0).astype(BF16)
            o = nt(q_in, st.astype(BF16))
            vb = vc.astype(BF16)
            v0 = _block_diag_rows(vb[0:c], pack, dv)
            v1 = _block_diag_rows(vb[c:c2], pack, dv)
            o0 = o[0:c] + jnp.dot(att0.astype(BF16), v0, preferred_element_type=F32)
            o1 = (o[c:c2] + jnp.dot(cross.astype(BF16), v0, preferred_element_type=F32)
                  + jnp.dot(att1.astype(BF16), v1, preferred_element_type=F32))
            o = jnp.concatenate([o0, o1], axis=0)

            k_dec = jnp.concatenate([k0 * (e_rev0 * e_l1), k1 * e_rev1], axis=0).astype(BF16)
            upd = lax.dot_general(vb, k_dec, _TN, preferred_element_type=F32)
            if pack > 1:
                upd = jnp.where(st_diag, upd, 0.0)
            st_ref[g] = st * (e_l0 * e_l1) + upd

            for p in range(pack):
                hs = slice(p * dv, (p + 1) * dv)
                o_ref[rows, g * wv + p * dv:g * wv + (p + 1) * dv] = (
                    _rms(o[:, hs], gn_ref[...]) * gate[:, hs]).astype(o_ref.dtype)
        return carry

    lax.fori_loop(0, n_chunks, chunk, 0, unroll=2)


def _gla_core(proj, *, mode, layer, batch, seq, heads, hps, pack, dk, dv, col_q, col_k, col_v,
              col_og, extra, gn):
    tl = GLA_TOKENS_PER_STEP
    nl = seq // tl
    t = batch * seq
    wk = hps * dk
    wv = hps * dv
    mall = jnp.asarray(_gla_arg_matrix(), BF16)
    masks = jnp.asarray(_gla_masks(pack), F32)

    def rowblk(b, h, l):
        return b * nl + l

    in_specs = [
        pl.BlockSpec((tl, wk), lambda b, h, l: (rowblk(b, h, l), col_q // wk + h)),
        pl.BlockSpec((tl, wk), lambda b, h, l: (rowblk(b, h, l), col_k // wk + h)),
        pl.BlockSpec((tl, wv), lambda b, h, l: (rowblk(b, h, l), col_v // wv + h)),
        pl.BlockSpec((tl, wv), lambda b, h, l: (rowblk(b, h, l), col_og // wv + h)),
    ]
    args = [proj, proj, proj, proj]
    if mode == "gla":
        col_glr, wgk, bgk = extra
        in_specs += [
            pl.BlockSpec((tl, LANES), lambda b, h, l: (rowblk(b, h, l), col_glr // LANES)),
            pl.BlockSpec((LANES, wk), lambda b, h, l: (0, h)),
            pl.BlockSpec((1, wk), lambda b, h, l: (0, h)),
        ]
        args += [proj, wgk, bgk]
    else:
        (lbl,) = extra
        in_specs += [pl.BlockSpec((DEPTH, wk), lambda b, h, l: (0, h))]
        args += [lbl]
    in_specs += [
        pl.BlockSpec((1, dv), lambda b, h, l: (0, 0)),
        pl.BlockSpec(mall.shape, lambda b, h, l: (0, 0)),
        pl.BlockSpec(masks.shape, lambda b, h, l: (0, 0, 0)),
    ]
    args += [gn, mall, masks]
    return pl.pallas_call(
        functools.partial(_gla_core_kernel, mode=mode, layer=layer, scale=dk ** -0.5,
                          packs=hps // pack, pack=pack, dk=dk, dv=dv),
        grid=(batch, heads // hps, nl),
        in_specs=in_specs,
        out_specs=pl.BlockSpec((tl, wv), lambda b, h, l: (rowblk(b, h, l), h)),
        out_shape=jax.ShapeDtypeStruct((t, heads * dv), BF16),
        scratch_shapes=[pltpu.VMEM((hps // pack, pack * dv, pack * dk), F32)],
        compiler_params=_cparams(("parallel", "parallel", "arbitrary")),
        name=mode + "_core",
    )(*args)


N_PAIRS = SSM_HEADS // 2
PAIRS_PER_GROUP = N_PAIRS // SSM_GROUPS


def _ssd_kernel(z_ref, x_ref, bm_ref, cm_ref, dt_ref, cw_ref, cb_ref, dtb_ref, alog_ref, dsk_ref,
                nw_ref, tri3_ref, o_ref,
                ext_ref, xs_ref, bs_ref, cs_ref, st_ref, y_ref, bt_ref, dtt_ref, wdt_ref, elb_ref):
    c = SSD_CHUNK
    halo = CONV_HALO
    l = pl.program_id(1)

    @pl.when(l == 0)
    def _():
        st_ref[...] = jnp.zeros_like(st_ref)
        ext_ref[0:halo, :] = jnp.zeros((halo, ext_ref.shape[1]), F32)

    @pl.when(l > 0)
    def _():
        ext_ref[0:halo, :] = ext_ref[c:c + halo, :]

    ext_ref[halo:halo + c, 0:SSM_DINNER] = x_ref[...]
    ext_ref[halo:halo + c, SSM_DINNER:SSM_DINNER + 1024] = bm_ref[...]
    ext_ref[halo:halo + c, SSM_DINNER + 1024:SSM_CONV_DIM] = cm_ref[...]

    for slab in range(SSM_CONV_DIM // LANES):
        cs = slice(slab * LANES, (slab + 1) * LANES)
        acc = cb_ref[:, cs] + cw_ref[0:1, cs] * ext_ref[halo - 3:halo - 3 + c, cs]
        for j in range(1, SSM_CONV):
            acc = acc + cw_ref[j:j + 1, cs] * ext_ref[halo - 3 + j:halo - 3 + j + c, cs]
        act = acc * jax.nn.sigmoid(acc)
        if slab < N_PAIRS:
            xs_ref[slab] = act
        elif slab < N_PAIRS + SSM_GROUPS:
            bs_ref[slab - N_PAIRS] = act
        else:
            cs_ref[slab - N_PAIRS - SSM_GROUPS] = act

    lane = lax.broadcasted_iota(jnp.int32, (1, LANES), 1)
    dtr = dt_ref[...] + dtb_ref[...]
    dt = jnp.maximum(dtr, 0.0) + jnp.log1p(jnp.exp(-jnp.abs(dtr)))
    a = jnp.where(lane < SSM_HEADS, -jnp.exp(alog_ref[...]), 0.0)
    la_hi, la_mid, la_lo = _split3(dt * (a * LOG2E))
    b = jnp.dot(tri3_ref[...], jnp.concatenate([la_hi, la_mid, la_lo], axis=0),
                preferred_element_type=F32)
    b_last = b[c - 1:c]
    wd = jnp.exp2(b_last - b) * dt
    bt = b.T
    bt_ref[...] = bt
    dtt_ref[...] = dt.T
    wdt_ref[...] = wd.T
    elb_ref[...] = jnp.broadcast_to(jnp.exp2(bt[:, c - 1:c]), (LANES, LANES))

    row = lax.broadcasted_iota(jnp.int32, (c, c), 0)
    col = lax.broadcasted_iota(jnp.int32, (c, c), 1)
    causal = col <= row
    lo = lax.broadcasted_iota(jnp.int32, (c, LANES), 1) < SSM_HEADDIM

    def group(g, carry):
        bg = bs_ref[g]
        cg = cs_ref[g]
        cgb = cg.astype(BF16)
        cbm = lax.dot_general(cgb, bg.astype(BF16), _NT, preferred_element_type=F32)
        bgt = bg.T
        for j in range(PAIRS_PER_GROUP):
            p = g * PAIRS_PER_GROUP + j
            xp = xs_ref[p]
            sp = st_ref[p]
            x_lo = jnp.where(lo, xp, 0.0).astype(BF16)
            x_hi = jnp.where(lo, 0.0, xp).astype(BF16)
            x_bd = jnp.concatenate([x_lo, x_hi], axis=0)
            lhs_y = []
            lhs_s = []
            bhs = []
            for k in range(2):
                h = 2 * p + k
                bh = jnp.broadcast_to(bt_ref[pl.ds(h, 1), :], (c, c)).T
                rel = bh - bt_ref[pl.ds(h, 1), :]
                dec = jnp.exp2(jnp.where(causal, rel, MASKED_LOG2))
                lhs_y.append((cbm * dec * dtt_ref[pl.ds(h, 1), :]).astype(BF16))
                lhs_s.append((bgt * wdt_ref[pl.ds(h, 1), :]).astype(BF16))
                bhs.append(bh)
            y = jnp.exp2(jnp.where(lo, bhs[0], bhs[1])) * jnp.dot(
                cgb, sp.astype(BF16), preferred_element_type=F32)
            y = y + jnp.dot(jnp.concatenate(lhs_y, axis=1), x_bd, preferred_element_type=F32)
            su = jnp.dot(jnp.concatenate(lhs_s, axis=1), x_bd,
                         preferred_element_type=F32)
            el = jnp.where(lane < SSM_HEADDIM, elb_ref[pl.ds(2 * p, 1), :],
                           elb_ref[pl.ds(2 * p + 1, 1), :])
            st_ref[p] = sp * el + su
            y_ref[p] = y + dsk_ref[p] * xp
        return carry

    lax.fori_loop(0, SSM_GROUPS, group, 0, unroll=8)

    gw = PAIRS_PER_GROUP * LANES
    for g in range(SSM_GROUPS):
        cs = slice(g * gw, (g + 1) * gw)
        yg = jnp.concatenate([y_ref[g * PAIRS_PER_GROUP + j] for j in range(PAIRS_PER_GROUP)],
                             axis=1)
        zg = z_ref[:, cs]
        yg = yg * (zg * jax.nn.sigmoid(zg))
        o_ref[:, cs] = _rms(yg, nw_ref[:, cs]).astype(o_ref.dtype)


def _ssd_core(proj, conv_w, conv_b, dt_bias, a_log, d_skip, norm_w, *, batch, seq):
    c = SSD_CHUNK
    nl = seq // c
    t = batch * seq
    tri = np.tril(np.ones((c, c), np.float32))
    tri3 = jnp.asarray(np.concatenate([tri, tri, tri], axis=1), BF16)
    pad = LANES - SSM_HEADS
    dtb = jnp.pad(dt_bias.astype(F32), (0, pad)).reshape(1, LANES)
    alog = jnp.pad(a_log.astype(F32), (0, pad)).reshape(1, LANES)
    dsk = jnp.repeat(d_skip.astype(F32), SSM_HEADDIM).reshape(N_PAIRS, 1, LANES)

    def rb(b, l):
        return b * nl + l

    full2 = lambda b, l: (0, 0)
    in_specs = [
        pl.BlockSpec((c, SSM_DINNER), lambda b, l: (rb(b, l), 0)),
        pl.BlockSpec((c, SSM_DINNER), lambda b, l: (rb(b, l), 1)),
        pl.BlockSpec((c, 1024), lambda b, l: (rb(b, l), 2 * SSM_DINNER // 1024)),
        pl.BlockSpec((c, 1024), lambda b, l: (rb(b, l), 2 * SSM_DINNER // 1024 + 1)),
        pl.BlockSpec((c, LANES), lambda b, l: (rb(b, l), (SSM_DINNER + SSM_CONV_DIM) // LANES)),
        pl.BlockSpec((SSM_CONV, SSM_CONV_DIM), full2),
        pl.BlockSpec((1, SSM_CONV_DIM), full2),
        pl.BlockSpec((1, LANES), full2),
        pl.BlockSpec((1, LANES), full2),
        pl.BlockSpec((N_PAIRS, 1, LANES), lambda b, l: (0, 0, 0)),
        pl.BlockSpec((1, SSM_DINNER), full2),
        pl.BlockSpec(tri3.shape, full2),
    ]
    scratch = [
        pltpu.VMEM((CONV_HALO + c, SSM_CONV_DIM), F32),
        pltpu.VMEM((N_PAIRS, c, LANES), F32),
        pltpu.VMEM((SSM_GROUPS, c, SSM_STATE), F32),
        pltpu.VMEM((SSM_GROUPS, c, SSM_STATE), F32),
        pltpu.VMEM((N_PAIRS, SSM_STATE, LANES), F32),
        pltpu.VMEM((N_PAIRS, c, LANES), F32),
        pltpu.VMEM((LANES, c), F32),
        pltpu.VMEM((LANES, c), F32),
        pltpu.VMEM((LANES, c), F32),
        pltpu.VMEM((LANES, LANES), F32),
    ]
    return pl.pallas_call(
        _ssd_kernel,
        grid=(batch, nl),
        in_specs=in_specs,
        out_specs=pl.BlockSpec((c, SSM_DINNER), lambda b, l: (rb(b, l), 0)),
        out_shape=jax.ShapeDtypeStruct((t, SSM_DINNER), BF16),
        scratch_shapes=scratch,
        compiler_params=_cparams(("parallel", "arbitrary")),
        name="ssd_core",
    )(proj, proj, proj, proj, proj, conv_w.astype(F32), conv_b.astype(F32).reshape(1, -1),
      dtb, alog, dsk, norm_w.astype(F32).reshape(1, -1), tri3)


def _pad_last(w, n):
    return jnp.pad(w, [(0, 0)] * (w.ndim - 1) + [(0, n - w.shape[-1])])


def kernel(x, p, norm_mix, norm_mlp, norm_ple, norm_final, w_up, w_down, w_ple_proj, w_ple_gate,
           gla_w_in, gla_w_gk2, gla_b_gk, gla_gn, gla_w_out,
           hgrn_lb_logits, hgrn_w_in, hgrn_gn, hgrn_w_out,
           ssm_w_in, ssm_conv_w, ssm_conv_b, ssm_dt_bias, ssm_a_log, ssm_d, ssm_norm, ssm_w_out):
    batch, seq, d = x.shape
    t = batch * seq
    h = x.reshape(t, d)
    pf = p.reshape(DEPTH, t, PLE_DIM)
    row = lambda v: v.astype(F32).reshape(1, -1)

    gla_w_in_b = _pad_last(gla_w_in, GLA_IN_COLS).astype(BF16)
    hgrn_w_in_b = hgrn_w_in.astype(BF16)
    ssm_w_in_b = _pad_last(ssm_w_in, SSM_IN_COLS).astype(BF16)
    gla_w_out_b = gla_w_out.astype(BF16)
    hgrn_w_out_b = hgrn_w_out.astype(BF16)
    ssm_w_out_b = ssm_w_out.astype(BF16)
    w_up_b = w_up.astype(BF16)
    w_down_b = w_down.astype(BF16)
    w_gate_b = w_ple_gate.astype(BF16)
    w_proj_b = w_ple_proj.astype(BF16)

    for i in range(DEPTH):
        kind, j = i % N_MIXERS, i // N_MIXERS
        nw = row(norm_mix[i])
        if kind == 0:
            proj = _norm_matmul(h, nw, gla_w_in_b, j, tm=1024, tn=1280)
            kd = GLA_HEADS * GLA_DK
            vd = GLA_HEADS * GLA_DV
            wgk = jnp.pad(gla_w_gk2[j], ((0, LANES - GLA_GATE_RANK), (0, 0))).astype(BF16)
            o = _gla_core(proj, mode="gla", layer=i, batch=batch, seq=seq, heads=GLA_HEADS,
                          hps=GLA_HEADS, pack=1, dk=GLA_DK, dv=GLA_DV, col_q=0, col_k=kd, col_v=2 * kd,
                          col_og=2 * kd + vd,
                          extra=(2 * kd + 2 * vd, wgk, row(gla_b_gk[j])), gn=row(gla_gn[j]))
            h = _matmul_res(o, gla_w_out_b, j, h, tm=512)
        elif kind == 1:
            proj = _norm_matmul(h, nw, hgrn_w_in_b, j, tm=1024, tn=1024)
            fd = HGRN_HEADS * HGRN_DK
            vd = HGRN_HEADS * HGRN_DV
            o = _gla_core(proj, mode="hgrn", layer=i, batch=batch, seq=seq, heads=HGRN_HEADS,
                          hps=HGRN_HEADS // 2, pack=2, dk=HGRN_DK, dv=HGRN_DV, col_q=0, col_k=fd,
                          col_v=2 * fd, col_og=2 * fd + vd,
                          extra=(hgrn_lb_logits.astype(F32),), gn=row(hgrn_gn[j]))
            h = _matmul_res(o, hgrn_w_out_b, j, h, tm=512)
        else:
            proj = _norm_matmul(h, nw, ssm_w_in_b, j, tm=1024, tn=1792)
            o = _ssd_core(proj, ssm_conv_w[j], ssm_conv_b[j], ssm_dt_bias[j], ssm_a_log[j],
                          ssm_d[j], ssm_norm[j], batch=batch, seq=seq)
            h = _matmul_res(o, ssm_w_out_b, j, h, tm=512)
        h = _mlp(h, row(norm_mlp[i]), w_up_b, w_down_b, i, tm=1024, tf=512)
        h = _ple(h, row(norm_ple[i]), w_gate_b, pf, w_proj_b, row(norm_final), i,
                 final_norm=(i == DEPTH - 1), tm=512, tn=512)
    return h.reshape(batch, seq, d)
```

```python
import functools
import math

import numpy as np
import jax
import jax.numpy as jnp
from jax import lax
from jax.experimental import pallas as pl
from jax.experimental.pallas import tpu as pltpu

F32 = jnp.float32
BF16 = jnp.bfloat16

D_MODEL = 2048
DEPTH = 4
N_MIXERS = 3
PLE_DIM = 256
D_FF = 4 * D_MODEL
EPS = 1e-6

GLA_HEADS = 4
GLA_DK = 256
GLA_DV = 512
GLA_GATE_RANK = 16
GLA_GATE_NORM = 16.0
HGRN_HEADS = 16
HGRN_DK = 128
HGRN_DV = 128
SSM_DINNER = 4096
SSM_HEADDIM = 64
SSM_HEADS = 64
SSM_GROUPS = 8
SSM_STATE = 128
SSM_CONV = 4
SSM_CONV_DIM = SSM_DINNER + 2 * SSM_GROUPS * SSM_STATE

LANES = 128
VMEM_LIMIT_BYTES = 56 * 1024 * 1024

NORM_ROWS = 256
GLA_CHUNK = 64
GLA_TOKENS_PER_STEP = 512
SSD_CHUNK = 128
CONV_HALO = 8

GLA_IN_COLS = 6400
SSM_IN_COLS = 10752
N_LEVELS = 6
LOG2E = math.log2(math.e)
MASKED_LOG2 = -1e30


def _cparams(sem):
    return pltpu.CompilerParams(dimension_semantics=sem, vmem_limit_bytes=VMEM_LIMIT_BYTES)


def _rms(x, w):
    ms = jnp.mean(x * x, axis=-1, keepdims=True)
    return x * lax.rsqrt(ms + EPS) * w


def _split3(x):
    hi = x.astype(BF16)
    r1 = x - hi.astype(F32)
    mid = r1.astype(BF16)
    lo = (r1 - mid.astype(F32)).astype(BF16)
    return hi, mid, lo


def _norm_matmul_kernel(h_ref, nw_ref, w_ref, o_ref, u_ref):
    j = pl.program_id(1)

    @pl.when(j == 0)
    def _():
        for r0 in range(0, h_ref.shape[0], NORM_ROWS):
            rs = slice(r0, r0 + NORM_ROWS)
            u = _rms(h_ref[rs, :], nw_ref[...]).astype(BF16)
            u_ref[rs, :] = u
            o_ref[rs, :] = jnp.dot(u, w_ref[...], preferred_element_type=F32).astype(o_ref.dtype)

    @pl.when(j > 0)
    def _():
        o_ref[...] = jnp.dot(u_ref[...], w_ref[...], preferred_element_type=F32).astype(o_ref.dtype)


def _norm_matmul(h, nw, w, layer, *, tm, tn):
    t, d = h.shape
    n = w.shape[2]
    return pl.pallas_call(
        _norm_matmul_kernel,
        grid=(t // tm, n // tn),
        in_specs=[
            pl.BlockSpec((tm, d), lambda i, j: (i, 0)),
            pl.BlockSpec((1, d), lambda i, j: (0, 0)),
            pl.BlockSpec((None, d, tn), lambda i, j: (layer, 0, j)),
        ],
        out_specs=pl.BlockSpec((tm, tn), lambda i, j: (i, j)),
        out_shape=jax.ShapeDtypeStruct((t, n), F32),
        scratch_shapes=[pltpu.VMEM((tm, d), BF16)],
        compiler_params=_cparams(("parallel", "arbitrary")),
        name="norm_matmul",
    )(h, nw, w)


def _matmul_res_kernel(a_ref, w_ref, h_ref, o_ref):
    o_ref[...] = h_ref[...] + jnp.dot(a_ref[...], w_ref[...], preferred_element_type=F32)


def _matmul_res(a, w, layer, h, *, tm):
    t, k = a.shape
    n = w.shape[2]
    return pl.pallas_call(
        _matmul_res_kernel,
        grid=(t // tm,),
        in_specs=[
            pl.BlockSpec((tm, k), lambda i: (i, 0)),
            pl.BlockSpec((None, k, n), lambda i: (layer, 0, 0), pipeline_mode=pl.Buffered(1)),
            pl.BlockSpec((tm, n), lambda i: (i, 0)),
        ],
        out_specs=pl.BlockSpec((tm, n), lambda i: (i, 0)),
        out_shape=jax.ShapeDtypeStruct((t, n), F32),
        compiler_params=_cparams(("parallel",)),
        name="matmul_res",
    )(a, w, h)


def _mlp_kernel(h_ref, nw_ref, wu_ref, wd_ref, o_ref, u_ref):
    f = pl.program_id(1)

    def ff(u):
        a = jnp.dot(u, wu_ref[...], preferred_element_type=F32)
        a = jnp.square(jnp.maximum(a, 0.0)).astype(BF16)
        return jnp.dot(a, wd_ref[...], preferred_element_type=F32)

    @pl.when(f == 0)
    def _():
        for r0 in range(0, h_ref.shape[0], NORM_ROWS):
            rs = slice(r0, r0 + NORM_ROWS)
            x = h_ref[rs, :]
            u = _rms(x, nw_ref[...]).astype(BF16)
            u_ref[rs, :] = u
            o_ref[rs, :] = x + ff(u)

    @pl.when(f > 0)
    def _():
        o_ref[...] += ff(u_ref[...])


def _mlp(h, nw, wu, wd, layer, *, tm, tf):
    t, d = h.shape
    ff = wu.shape[2]
    return pl.pallas_call(
        _mlp_kernel,
        grid=(t // tm, ff // tf),
        in_specs=[
            pl.BlockSpec((tm, d), lambda i, f: (i, 0)),
            pl.BlockSpec((1, d), lambda i, f: (0, 0)),
            pl.BlockSpec((None, d, tf), lambda i, f: (layer, 0, f)),
            pl.BlockSpec((None, tf, d), lambda i, f: (layer, f, 0)),
        ],
        out_specs=pl.BlockSpec((tm, d), lambda i, f: (i, 0)),
        out_shape=jax.ShapeDtypeStruct((t, d), F32),
        scratch_shapes=[pltpu.VMEM((tm, d), BF16)],
        compiler_params=_cparams(("parallel", "arbitrary")),
        name="mlp",
    )(h, nw, wu, wd)


def _ple_kernel(h_ref, nw_ref, wg_ref, p_ref, wp_ref, nf_ref, o_ref, *, final_norm, tn):
    d = h_ref.shape[1]
    for r0 in range(0, h_ref.shape[0], NORM_ROWS):
        rs = slice(r0, r0 + NORM_ROWS)
        u = _rms(h_ref[rs, :], nw_ref[...]).astype(BF16)
        pb = p_ref[rs, :].astype(BF16)
        for n0 in range(0, d, tn):
            cs = slice(n0, n0 + tn)
            gate = jax.nn.sigmoid(jnp.dot(u, wg_ref[:, cs], preferred_element_type=F32))
            proj = jnp.dot(pb, wp_ref[:, cs], preferred_element_type=F32)
            o_ref[rs, cs] = h_ref[rs, cs] + gate * proj
        if final_norm:
            o_ref[rs, :] = _rms(o_ref[rs, :], nf_ref[...])


def _ple(h, nw, wg, p, wp, nf, layer, *, final_norm, tm, tn):
    t, d = h.shape
    pd = p.shape[2]
    return pl.pallas_call(
        functools.partial(_ple_kernel, final_norm=final_norm, tn=tn),
        grid=(t // tm,),
        in_specs=[
            pl.BlockSpec((tm, d), lambda i: (i, 0)),
            pl.BlockSpec((1, d), lambda i: (0, 0)),
            pl.BlockSpec((None, d, d), lambda i: (layer, 0, 0)),
            pl.BlockSpec((None, tm, pd), lambda i: (layer, i, 0)),
            pl.BlockSpec((None, pd, d), lambda i: (layer, 0, 0)),
            pl.BlockSpec((1, d), lambda i: (0, 0)),
        ],
        out_specs=pl.BlockSpec((tm, d), lambda i: (i, 0)),
        out_shape=jax.ShapeDtypeStruct((t, d), F32),
        compiler_params=_cparams(("parallel",)),
        name="ple",
    )(h, nw, wg, p, wp, nf)


MATMUL_LEVELS = (3, 4)


def _gla_arg_matrix():
    c = GLA_CHUNK
    t = np.arange(c)[:, None]
    r = np.arange(c)[None, :]
    blocks = [(r <= t)]
    for lvl in MATMUL_LEVELS:
        n = c >> (lvl + 1)
        ref = (t // (2 * n)) * (2 * n) + n - 1
        lower = t > ref
        blocks.append(np.where(lower, (r > ref) & (r <= t), (r > t) & (r <= ref)))
    m = np.concatenate(blocks, axis=0).astype(np.float32)
    return np.concatenate([m, m, m, np.zeros_like(m)], axis=1)


def _coarse_level_args(b, n):
    out = []
    for blk in range(GLA_CHUNK // (2 * n)):
        r0 = blk * 2 * n
        ref = b[r0 + n - 1:r0 + n]
        out.append(ref - b[r0:r0 + n])
        out.append(b[r0 + n:r0 + 2 * n] - ref)
    return jnp.concatenate(out, axis=0)


def _gla_masks(pack):
    c = GLA_CHUNK
    t = np.arange(c)[:, None]
    s = np.arange(c)[None, :]
    masks = np.zeros((1 + N_LEVELS, c, c), np.float32)
    masks[0] = (t == s)
    for lvl in range(N_LEVELS):
        n = c >> (lvl + 1)
        same = (t // (2 * n)) == (s // (2 * n))
        masks[1 + lvl] = same & ((t % (2 * n)) >= n) & ((s % (2 * n)) < n)
    return np.tile(masks, (1, 1, pack))


def _block_diag_rows(x, pack, width):
    if pack == 1:
        return x
    head = lax.broadcasted_iota(jnp.int32, x.shape, 1) // width
    return jnp.concatenate([jnp.where(head == p, x, jnp.zeros_like(x)) for p in range(pack)], axis=0)


_NT = (((1,), (1,)), ((), ()))
_TN = (((0,), (0,)), ((), ()))


def _gla_core_kernel(*refs, mode, layer, scale, packs, pack, dk, dv):
    wk = pack * dk
    wv = pack * dv
    if mode == "gla":
        (q_ref, k_ref, v_ref, og_ref, glr_ref, wgk_ref, bgk_ref, gn_ref, mall_ref, masks_ref,
         o_ref, st_ref) = refs
    else:
        (q_ref, k_ref, v_ref, og_ref, lbl_ref, gn_ref, mall_ref, masks_ref, o_ref, st_ref) = refs

    @pl.when(pl.program_id(2) == 0)
    def _():
        st_ref[...] = jnp.zeros_like(st_ref)

    if mode == "hgrn":
        lg = lbl_ref[...]
        e = jnp.exp(lg - jnp.max(lg, axis=0, keepdims=True))
        sm = e / jnp.sum(e, axis=0, keepdims=True)
        lb_all = jnp.zeros_like(sm[0:1])
        for r in range(1, layer + 1):
            lb_all = lb_all + sm[r:r + 1]

    c = GLA_CHUNK
    c2 = 2 * c
    n_chunks = q_ref.shape[0] // c2
    odd_row = lax.broadcasted_iota(jnp.int32, (c, wk), 0) % 2 == 1
    if pack > 1:
        st_diag = (lax.broadcasted_iota(jnp.int32, (wv, wk), 0) // dv
                   == lax.broadcasted_iota(jnp.int32, (wv, wk), 1) // dk)

    def nt(a, b):
        return lax.dot_general(a, b, _NT, preferred_element_type=F32)

    def half_terms(qh, kh, gh):
        g_hi, g_mid, g_lo = _split3(gh)
        gs = jnp.concatenate([g_hi, g_mid, g_lo, jnp.zeros_like(g_hi)], axis=0)
        pre = jnp.dot(mall_ref[...], gs, preferred_element_type=F32)
        b = pre[0:c]
        lvl_args = [_coarse_level_args(b, c >> (lvl + 1)) for lvl in range(MATMUL_LEVELS[0])]
        lvl_args += [pre[c:2 * c], pre[2 * c:3 * c], jnp.where(odd_row, gh, 0.0)]
        att = masks_ref[0] * nt(qh.astype(BF16), _block_diag_rows(kh.astype(BF16), pack, dk))
        for lvl in range(N_LEVELS):
            el = jnp.exp2(lvl_args[lvl])
            att = att + masks_ref[1 + lvl] * nt(
                (qh * el).astype(BF16), _block_diag_rows((kh * el).astype(BF16), pack, dk))
        return b, att

    def chunk(ci, carry):
        rows = pl.ds(pl.multiple_of(ci * c2, c2), c2)
        if mode == "gla":
            z_all = jnp.dot(glr_ref[rows, :].astype(BF16), wgk_ref[...],
                            preferred_element_type=F32) + bgk_ref[...]
        for g in range(packs):
            ks = slice(g * wk, (g + 1) * wk)
            vs = slice(g * wv, (g + 1) * wv)
            vc = v_ref[rows, vs]
            og = og_ref[rows, vs]
            if mode == "gla":
                qc = q_ref[rows, ks] * scale
                kc = k_ref[rows, ks]
                z = z_all[:, ks]
                gc = (jnp.minimum(z, 0.0) - jnp.log1p(jnp.exp(-jnp.abs(z)))) * (LOG2E / GLA_GATE_NORM)
                gate = og * jax.nn.sigmoid(og)
            else:
                qq = q_ref[rows, ks]
                qc = qq * jax.nn.sigmoid(qq) * scale
                lb = lb_all[:, ks]
                f = lb + (1.0 - lb) * jax.nn.sigmoid(k_ref[rows, ks])
                kc = 1.0 - f
                gc = jnp.log(f) * LOG2E
                gate = jax.nn.sigmoid(og)

            q0, q1 = qc[0:c], qc[c:c2]
            k0, k1 = kc[0:c], kc[c:c2]
            b0, att0 = half_terms(q0, k0, gc[0:c])
            b1, att1 = half_terms(q1, k1, gc[c:c2])
            e_b0 = jnp.exp2(b0)
            e_b1 = jnp.exp2(b1)
            e_rev0 = jnp.exp2(b0[c - 1:c] - b0)
            e_rev1 = jnp.exp2(b1[c - 1:c] - b1)
            e_l0 = e_b0[c - 1:c]
            e_l1 = e_b1[c - 1:c]

            k0_end = (k0 * e_rev0).astype(BF16)
            cross = nt((q1 * e_b1).astype(BF16), _block_diag_rows(k0_end, pack, dk))

            st = st_ref[g]
            q_in = jnp.concatenate([q0 * e_b0, q1 * (e_b1 * e_l0)], axis=0).astype(BF16)
            o = nt(q_in, st.astype(BF16))
            vb = vc.astype(BF16)
            v0 = _block_diag_rows(vb[0:c], pack, dv)
            v1 = _block_diag_rows(vb[c:c2], pack, dv)
            o0 = o[0:c] + jnp.dot(att0.astype(BF16), v0, preferred_element_type=F32)
            o1 = (o[c:c2] + jnp.dot(cross.astype(BF16), v0, preferred_element_type=F32)
                  + jnp.dot(att1.astype(BF16), v1, preferred_element_type=F32))
            o = jnp.concatenate([o0, o1], axis=0)

            k_dec = jnp.concatenate([k0 * (e_rev0 * e_l1), k1 * e_rev1], axis=0).astype(BF16)
            upd = lax.dot_general(vb, k_dec, _TN, preferred_element_type=F32)
            if pack > 1:
                upd = jnp.where(st_diag, upd, 0.0)
            st_ref[g] = st * (e_l0 * e_l1) + upd

            for p in range(pack):
                hs = slice(p * dv, (p + 1) * dv)
                o_ref[rows, g * wv + p * dv:g * wv + (p + 1) * dv] = (
                    _rms(o[:, hs], gn_ref[...]) * gate[:, hs]).astype(o_ref.dtype)
        return carry

    lax.fori_loop(0, n_chunks, chunk, 0, unroll=2)


def _gla_core(proj, *, mode, layer, batch, seq, heads, hps, pack, dk, dv, col_q, col_k, col_v,
              col_og, extra, gn):
    tl = GLA_TOKENS_PER_STEP
    nl = seq // tl
    t = batch * seq
    wk = hps * dk
    wv = hps * dv
    mall = jnp.asarray(_gla_arg_matrix(), BF16)
    masks = jnp.asarray(_gla_masks(pack), F32)

    def rowblk(b, h, l):
        return b * nl + l

    in_specs = [
        pl.BlockSpec((tl, wk), lambda b, h, l: (rowblk(b, h, l), col_q // wk + h)),
        pl.BlockSpec((tl, wk), lambda b, h, l: (rowblk(b, h, l), col_k // wk + h)),
        pl.BlockSpec((tl, wv), lambda b, h, l: (rowblk(b, h, l), col_v // wv + h)),
        pl.BlockSpec((tl, wv), lambda b, h, l: (rowblk(b, h, l), col_og // wv + h)),
    ]
    args = [proj, proj, proj, proj]
    if mode == "gla":
        col_glr, wgk, bgk = extra
        in_specs += [
            pl.BlockSpec((tl, LANES), lambda b, h, l: (rowblk(b, h, l), col_glr // LANES)),
            pl.BlockSpec((LANES, wk), lambda b, h, l: (0, h)),
            pl.BlockSpec((1, wk), lambda b, h, l: (0, h)),
        ]
        args += [proj, wgk, bgk]
    else:
        (lbl,) = extra
        in_specs += [pl.BlockSpec((DEPTH, wk), lambda b, h, l: (0, h))]
        args += [lbl]
    in_specs += [
        pl.BlockSpec((1, dv), lambda b, h, l: (0, 0)),
        pl.BlockSpec(mall.shape, lambda b, h, l: (0, 0)),
        pl.BlockSpec(masks.shape, lambda b, h, l: (0, 0, 0)),
    ]
    args += [gn, mall, masks]
    return pl.pallas_call(
        functools.partial(_gla_core_kernel, mode=mode, layer=layer, scale=dk ** -0.5,
                          packs=hps // pack, pack=pack, dk=dk, dv=dv),
        grid=(batch, heads // hps, nl),
        in_specs=in_specs,
        out_specs=pl.BlockSpec((tl, wv), lambda b, h, l: (rowblk(b, h, l), h)),
        out_shape=jax.ShapeDtypeStruct((t, heads * dv), BF16),
        scratch_shapes=[pltpu.VMEM((hps // pack, pack * dv, pack * dk), F32)],
        compiler_params=_cparams(("parallel", "parallel", "arbitrary")),
        name=mode + "_core",
    )(*args)


N_PAIRS = SSM_HEADS // 2
PAIRS_PER_GROUP = N_PAIRS // SSM_GROUPS


def _ssd_kernel(z_ref, x_ref, bm_ref, cm_ref, dt_ref, cw_ref, cb_ref, dtb_ref, alog_ref, dsk_ref,
                nw_ref, tri3_ref, o_ref,
                ext_ref, xs_ref, bs_ref, cs_ref, st_ref, y_ref, bt_ref, dtt_ref, wdt_ref, elb_ref):
    c = SSD_CHUNK
    halo = CONV_HALO
    l = pl.program_id(1)

    @pl.when(l == 0)
    def _():
        st_ref[...] = jnp.zeros_like(st_ref)
        ext_ref[0:halo, :] = jnp.zeros((halo, ext_ref.shape[1]), F32)

    @pl.when(l > 0)
    def _():
        ext_ref[0:halo, :] = ext_ref[c:c + halo, :]

    ext_ref[halo:halo + c, 0:SSM_DINNER] = x_ref[...]
    ext_ref[halo:halo + c, SSM_DINNER:SSM_DINNER + 1024] = bm_ref[...]
    ext_ref[halo:halo + c, SSM_DINNER + 1024:SSM_CONV_DIM] = cm_ref[...]

    for slab in range(SSM_CONV_DIM // LANES):
        cs = slice(slab * LANES, (slab + 1) * LANES)
        acc = cb_ref[:, cs] + cw_ref[0:1, cs] * ext_ref[halo - 3:halo - 3 + c, cs]
        for j in range(1, SSM_CONV):
            acc = acc + cw_ref[j:j + 1, cs] * ext_ref[halo - 3 + j:halo - 3 + j + c, cs]
        act = acc * jax.nn.sigmoid(acc)
        if slab < N_PAIRS:
            xs_ref[slab] = act
        elif slab < N_PAIRS + SSM_GROUPS:
            bs_ref[slab - N_PAIRS] = act
        else:
            cs_ref[slab - N_PAIRS - SSM_GROUPS] = act

    lane = lax.broadcasted_iota(jnp.int32, (1, LANES), 1)
    dtr = dt_ref[...] + dtb_ref[...]
    dt = jnp.maximum(dtr, 0.0) + jnp.log1p(jnp.exp(-jnp.abs(dtr)))
    a = jnp.where(lane < SSM_HEADS, -jnp.exp(alog_ref[...]), 0.0)
    la_hi, la_mid, la_lo = _split3(dt * (a * LOG2E))
    b = jnp.dot(tri3_ref[...], jnp.concatenate([la_hi, la_mid, la_lo], axis=0),
                preferred_element_type=F32)
    b_last = b[c - 1:c]
    wd = jnp.exp2(b_last - b) * dt
    bt = b.T
    bt_ref[...] = bt
    dtt_ref[...] = dt.T
    wdt_ref[...] = wd.T
    elb_ref[...] = jnp.broadcast_to(jnp.exp2(bt[:, c - 1:c]), (LANES, LANES))

    row = lax.broadcasted_iota(jnp.int32, (c, c), 0)
    col = lax.broadcasted_iota(jnp.int32, (c, c), 1)
    causal = col <= row
    lo = lax.broadcasted_iota(jnp.int32, (c, LANES), 1) < SSM_HEADDIM

    def group(g, carry):
        bg = bs_ref[g]
        cg = cs_ref[g]
        cgb = cg.astype(BF16)
        cbm = lax.dot_general(cgb, bg.astype(BF16), _NT, preferred_element_type=F32)
        bgt = bg.T
        for j in range(PAIRS_PER_GROUP):
            p = g * PAIRS_PER_GROUP + j
            xp = xs_ref[p]
            sp = st_ref[p]
            x_lo = jnp.where(lo, xp, 0.0).astype(BF16)
            x_hi = jnp.where(lo, 0.0, xp).astype(BF16)
            x_bd = jnp.concatenate([x_lo, x_hi], axis=0)
            lhs_y = []
            lhs_s = []
            bhs = []
            for k in range(2):
                h = 2 * p + k
                bh = jnp.broadcast_to(bt_ref[pl.ds(h, 1), :], (c, c)).T
                rel = bh - bt_ref[pl.ds(h, 1), :]
                dec = jnp.exp2(jnp.where(causal, rel, MASKED_LOG2))
                lhs_y.append((cbm * dec * dtt_ref[pl.ds(h, 1), :]).astype(BF16))
                lhs_s.append((bgt * wdt_ref[pl.ds(h, 1), :]).astype(BF16))
                bhs.append(bh)
            y = jnp.exp2(jnp.where(lo, bhs[0], bhs[1])) * jnp.dot(
                cgb, sp.astype(BF16), preferred_element_type=F32)
            y = y + jnp.dot(jnp.concatenate(lhs_y, axis=1), x_bd, preferred_element_type=F32)
            su = jnp.dot(jnp.concatenate(lhs_s, axis=1), x_bd,
                         preferred_element_type=F32)
            el = jnp.where(lane < SSM_HEADDIM, elb_ref[pl.ds(2 * p, 1), :],
                           elb_ref[pl.ds(2 * p + 1, 1), :])
            st_ref[p] = sp * el + su
            y_ref[p] = y + dsk_ref[p] * xp
        return carry

    lax.fori_loop(0, SSM_GROUPS, group, 0, unroll=8)

    gw = PAIRS_PER_GROUP * LANES
    for g in range(SSM_GROUPS):
        cs = slice(g * gw, (g + 1) * gw)
        yg = jnp.concatenate([y_ref[g * PAIRS_PER_GROUP + j] for j in range(PAIRS_PER_GROUP)],
                             axis=1)
        zg = z_ref[:, cs]
        yg = yg * (zg * jax.nn.sigmoid(zg))
        o_ref[:, cs] = _rms(yg, nw_ref[:, cs]).astype(o_ref.dtype)


def _ssd_core(proj, conv_w, conv_b, dt_bias, a_log, d_skip, norm_w, *, batch, seq):
    c = SSD_CHUNK
    nl = seq // c
    t = batch * seq
    tri = np.tril(np.ones((c, c), np.float32))
    tri3 = jnp.asarray(np.concatenate([tri, tri, tri], axis=1), BF16)
    pad = LANES - SSM_HEADS
    dtb = jnp.pad(dt_bias.astype(F32), (0, pad)).reshape(1, LANES)
    alog = jnp.pad(a_log.astype(F32), (0, pad)).reshape(1, LANES)
    dsk = jnp.repeat(d_skip.astype(F32), SSM_HEADDIM).reshape(N_PAIRS, 1, LANES)

    def rb(b, l):
        return b * nl + l

    full2 = lambda b, l: (0, 0)
    in_specs = [
        pl.BlockSpec((c, SSM_DINNER), lambda b, l: (rb(b, l), 0)),
        pl.BlockSpec((c, SSM_DINNER), lambda b, l: (rb(b, l), 1)),
        pl.BlockSpec((c, 1024), lambda b, l: (rb(b, l), 2 * SSM_DINNER // 1024)),
        pl.BlockSpec((c, 1024), lambda b, l: (rb(b, l), 2 * SSM_DINNER // 1024 + 1)),
        pl.BlockSpec((c, LANES), lambda b, l: (rb(b, l), (SSM_DINNER + SSM_CONV_DIM) // LANES)),
        pl.BlockSpec((SSM_CONV, SSM_CONV_DIM), full2),
        pl.BlockSpec((1, SSM_CONV_DIM), full2),
        pl.BlockSpec((1, LANES), full2),
        pl.BlockSpec((1, LANES), full2),
        pl.BlockSpec((N_PAIRS, 1, LANES), lambda b, l: (0, 0, 0)),
        pl.BlockSpec((1, SSM_DINNER), full2),
        pl.BlockSpec(tri3.shape, full2),
    ]
    scratch = [
        pltpu.VMEM((CONV_HALO + c, SSM_CONV_DIM), F32),
        pltpu.VMEM((N_PAIRS, c, LANES), F32),
        pltpu.VMEM((SSM_GROUPS, c, SSM_STATE), F32),
        pltpu.VMEM((SSM_GROUPS, c, SSM_STATE), F32),
        pltpu.VMEM((N_PAIRS, SSM_STATE, LANES), F32),
        pltpu.VMEM((N_PAIRS, c, LANES), F32),
        pltpu.VMEM((LANES, c), F32),
        pltpu.VMEM((LANES, c), F32),
        pltpu.VMEM((LANES, c), F32),
        pltpu.VMEM((LANES, LANES), F32),
    ]
    return pl.pallas_call(
        _ssd_kernel,
        grid=(batch, nl),
        in_specs=in_specs,
        out_specs=pl.BlockSpec((c, SSM_DINNER), lambda b, l: (rb(b, l), 0)),
        out_shape=jax.ShapeDtypeStruct((t, SSM_DINNER), BF16),
        scratch_shapes=scratch,
        compiler_params=_cparams(("parallel", "arbitrary")),
        name="ssd_core",
    )(proj, proj, proj, proj, proj, conv_w.astype(F32), conv_b.astype(F32).reshape(1, -1),
      dtb, alog, dsk, norm_w.astype(F32).reshape(1, -1), tri3)


def _pad_last(w, n):
    return jnp.pad(w, [(0, 0)] * (w.ndim - 1) + [(0, n - w.shape[-1])])


def kernel(x, p, norm_mix, norm_mlp, norm_ple, norm_final, w_up, w_down, w_ple_proj, w_ple_gate,
           gla_w_in, gla_w_gk2, gla_b_gk, gla_gn, gla_w_out,
           hgrn_lb_logits, hgrn_w_in, hgrn_gn, hgrn_w_out,
           ssm_w_in, ssm_conv_w, ssm_conv_b, ssm_dt_bias, ssm_a_log, ssm_d, ssm_norm, ssm_w_out):
    batch, seq, d = x.shape
    t = batch * seq
    h = x.reshape(t, d)
    pf = p.reshape(DEPTH, t, PLE_DIM)
    row = lambda v: v.astype(F32).reshape(1, -1)

    gla_w_in_b = _pad_last(gla_w_in, GLA_IN_COLS).astype(BF16)
    hgrn_w_in_b = hgrn_w_in.astype(BF16)
    ssm_w_in_b = _pad_last(ssm_w_in, SSM_IN_COLS).astype(BF16)
    gla_w_out_b = gla_w_out.astype(BF16)
    hgrn_w_out_b = hgrn_w_out.astype(BF16)
    ssm_w_out_b = ssm_w_out.astype(BF16)
    w_up_b = w_up.astype(BF16)
    w_down_b = w_down.astype(BF16)
    w_gate_b = w_ple_gate.astype(BF16)
    w_proj_b = w_ple_proj.astype(BF16)

    for i in range(DEPTH):
        kind, j = i % N_MIXERS, i // N_MIXERS
        nw = row(norm_mix[i])
        if kind == 0:
            proj = _norm_matmul(h, nw, gla_w_in_b, j, tm=1024, tn=1280)
            kd = GLA_HEADS * GLA_DK
            vd = GLA_HEADS * GLA_DV
            wgk = jnp.pad(gla_w_gk2[j], ((0, LANES - GLA_GATE_RANK), (0, 0))).astype(BF16)
            o = _gla_core(proj, mode="gla", layer=i, batch=batch, seq=seq, heads=GLA_HEADS,
                          hps=GLA_HEADS, pack=1, dk=GLA_DK, dv=GLA_DV, col_q=0, col_k=kd, col_v=2 * kd,
                          col_og=2 * kd + vd,
                          extra=(2 * kd + 2 * vd, wgk, row(gla_b_gk[j])), gn=row(gla_gn[j]))
            h = _matmul_res(o, gla_w_out_b, j, h, tm=512)
        elif kind == 1:
            proj = _norm_matmul(h, nw, hgrn_w_in_b, j, tm=1024, tn=1024)
            fd = HGRN_HEADS * HGRN_DK
            vd = HGRN_HEADS * HGRN_DV
            o = _gla_core(proj, mode="hgrn", layer=i, batch=batch, seq=seq, heads=HGRN_HEADS,
                          hps=HGRN_HEADS // 2, pack=2, dk=HGRN_DK, dv=HGRN_DV, col_q=0, col_k=fd,
                          col_v=2 * fd, col_og=2 * fd + vd,
                          extra=(hgrn_lb_logits.astype(F32),), gn=row(hgrn_gn[j]))
            h = _matmul_res(o, hgrn_w_out_b, j, h, tm=512)
        else:
            proj = _norm_matmul(h, nw, ssm_w_in_b, j, tm=1024, tn=1792)
            o = _ssd_core(proj, ssm_conv_w[j], ssm_conv_b[j], ssm_dt_bias[j], ssm_a_log[j],
                          ssm_d[j], ssm_norm[j], batch=batch, seq=seq)
            h = _matmul_res(o, ssm_w_out_b, j, h, tm=512)
        h = _mlp(h, row(norm_mlp[i]), w_up_b, w_down_b, i, tm=1024, tf=512)
        h = _ple(h, row(norm_ple[i]), w_gate_b, pf, w_proj_b, row(norm_final), i,
                 final_norm=(i == DEPTH - 1), tm=512, tn=512)
    return h.reshape(batch, seq, d)
```

```python
import functools
import math

import numpy as np
import jax
import jax.numpy as jnp
from jax import lax
from jax.experimental import pallas as pl
from jax.experimental.pallas import tpu as pltpu

F32 = jnp.float32
BF16 = jnp.bfloat16

D_MODEL = 2048
DEPTH = 4
N_MIXERS = 3
PLE_DIM = 256
D_FF = 4 * D_MODEL
EPS = 1e-6

GLA_HEADS = 4
GLA_DK = 256
GLA_DV = 512
GLA_GATE_RANK = 16
GLA_GATE_NORM = 16.0
HGRN_HEADS = 16
HGRN_DK = 128
HGRN_DV = 128
SSM_DINNER = 4096
SSM_HEADDIM = 64
SSM_HEADS = 64
SSM_GROUPS = 8
SSM_STATE = 128
SSM_CONV = 4
SSM_CONV_DIM = SSM_DINNER + 2 * SSM_GROUPS * SSM_STATE

LANES = 128
VMEM_LIMIT_BYTES = 60 * 1024 * 1024

NORM_ROWS = 256
GLA_CHUNK = 64
GLA_TOKENS_PER_STEP = 512
SSD_CHUNK = 128
CONV_HALO = 8

GLA_IN_COLS = 6400
SSM_IN_COLS = 10752
N_LEVELS = 6
LOG2E = math.log2(math.e)
MASKED_LOG2 = -1e30


def _cparams(sem):
    return pltpu.CompilerParams(dimension_semantics=sem, vmem_limit_bytes=VMEM_LIMIT_BYTES)


def _rms(x, w):
    ms = jnp.mean(x * x, axis=-1, keepdims=True)
    return x * lax.rsqrt(ms + EPS) * w


def _split3(x):
    hi = x.astype(BF16)
    r1 = x - hi.astype(F32)
    mid = r1.astype(BF16)
    lo = (r1 - mid.astype(F32)).astype(BF16)
    return hi, mid, lo


def _norm_matmul_kernel(h_ref, nw_ref, w_ref, o_ref, u_ref):
    j = pl.program_id(1)

    @pl.when(j == 0)
    def _():
        for r0 in range(0, h_ref.shape[0], NORM_ROWS):
            rs = slice(r0, r0 + NORM_ROWS)
            u = _rms(h_ref[rs, :], nw_ref[...]).astype(BF16)
            u_ref[rs, :] = u
            o_ref[rs, :] = jnp.dot(u, w_ref[...], preferred_element_type=F32).astype(o_ref.dtype)

    @pl.when(j > 0)
    def _():
        o_ref[...] = jnp.dot(u_ref[...], w_ref[...], preferred_element_type=F32).astype(o_ref.dtype)


def _norm_matmul(h, nw, w, layer, *, tm, tn):
    t, d = h.shape
    n = w.shape[2]
    return pl.pallas_call(
        _norm_matmul_kernel,
        grid=(t // tm, n // tn),
        in_specs=[
            pl.BlockSpec((tm, d), lambda i, j: (i, 0)),
            pl.BlockSpec((1, d), lambda i, j: (0, 0)),
            pl.BlockSpec((None, d, tn), lambda i, j: (layer, 0, j)),
        ],
        out_specs=pl.BlockSpec((tm, tn), lambda i, j: (i, j)),
        out_shape=jax.ShapeDtypeStruct((t, n), F32),
        scratch_shapes=[pltpu.VMEM((tm, d), BF16)],
        compiler_params=_cparams(("parallel", "arbitrary")),
        name="norm_matmul",
    )(h, nw, w)


def _matmul_res_kernel(a_ref, w_ref, h_ref, o_ref):
    o_ref[...] = h_ref[...] + jnp.dot(a_ref[...], w_ref[...], preferred_element_type=F32)


def _matmul_res(a, w, layer, h, *, tm):
    t, k = a.shape
    n = w.shape[2]
    return pl.pallas_call(
        _matmul_res_kernel,
        grid=(t // tm,),
        in_specs=[
            pl.BlockSpec((tm, k), lambda i: (i, 0)),
            pl.BlockSpec((None, k, n), lambda i: (layer, 0, 0), pipeline_mode=pl.Buffered(1)),
            pl.BlockSpec((tm, n), lambda i: (i, 0)),
        ],
        out_specs=pl.BlockSpec((tm, n), lambda i: (i, 0)),
        out_shape=jax.ShapeDtypeStruct((t, n), F32),
        compiler_params=_cparams(("parallel",)),
        name="matmul_res",
    )(a, w, h)


def _mlp_kernel(h_ref, nw_ref, wu_ref, wd_ref, o_ref, u_ref):
    f = pl.program_id(1)

    def ff(u):
        a = jnp.dot(u, wu_ref[...], preferred_element_type=F32)
        a = jnp.square(jnp.maximum(a, 0.0)).astype(BF16)
        return jnp.dot(a, wd_ref[...], preferred_element_type=F32)

    @pl.when(f == 0)
    def _():
        for r0 in range(0, h_ref.shape[0], NORM_ROWS):
            rs = slice(r0, r0 + NORM_ROWS)
            x = h_ref[rs, :]
            u = _rms(x, nw_ref[...]).astype(BF16)
            u_ref[rs, :] = u
            o_ref[rs, :] = x + ff(u)

    @pl.when(f > 0)
    def _():
        o_ref[...] += ff(u_ref[...])


def _mlp(h, nw, wu, wd, layer, *, tm, tf):
    t, d = h.shape
    ff = wu.shape[2]
    return pl.pallas_call(
        _mlp_kernel,
        grid=(t // tm, ff // tf),
        in_specs=[
            pl.BlockSpec((tm, d), lambda i, f: (i, 0)),
            pl.BlockSpec((1, d), lambda i, f: (0, 0)),
            pl.BlockSpec((None, d, tf), lambda i, f: (layer, 0, f)),
            pl.BlockSpec((None, tf, d), lambda i, f: (layer, f, 0)),
        ],
        out_specs=pl.BlockSpec((tm, d), lambda i, f: (i, 0)),
        out_shape=jax.ShapeDtypeStruct((t, d), F32),
        scratch_shapes=[pltpu.VMEM((tm, d), BF16)],
        compiler_params=_cparams(("parallel", "arbitrary")),
        name="mlp",
    )(h, nw, wu, wd)


def _ple_kernel(h_ref, nw_ref, wg_ref, p_ref, wp_ref, nf_ref, o_ref, *, final_norm, tn):
    d = h_ref.shape[1]
    for r0 in range(0, h_ref.shape[0], NORM_ROWS):
        rs = slice(r0, r0 + NORM_ROWS)
        u = _rms(h_ref[rs, :], nw_ref[...]).astype(BF16)
        pb = p_ref[rs, :].astype(BF16)
        for n0 in range(0, d, tn):
            cs = slice(n0, n0 + tn)
            gate = jax.nn.sigmoid(jnp.dot(u, wg_ref[:, cs], preferred_element_type=F32))
            proj = jnp.dot(pb, wp_ref[:, cs], preferred_element_type=F32)
            o_ref[rs, cs] = h_ref[rs, cs] + gate * proj
        if final_norm:
            o_ref[rs, :] = _rms(o_ref[rs, :], nf_ref[...])


def _ple(h, nw, wg, p, wp, nf, layer, *, final_norm, tm, tn):
    t, d = h.shape
    pd = p.shape[2]
    return pl.pallas_call(
        functools.partial(_ple_kernel, final_norm=final_norm, tn=tn),
        grid=(t // tm,),
        in_specs=[
            pl.BlockSpec((tm, d), lambda i: (i, 0)),
            pl.BlockSpec((1, d), lambda i: (0, 0)),
            pl.BlockSpec((None, d, d), lambda i: (layer, 0, 0), pipeline_mode=pl.Buffered(1)),
            pl.BlockSpec((None, tm, pd), lambda i: (layer, i, 0)),
            pl.BlockSpec((None, pd, d), lambda i: (layer, 0, 0), pipeline_mode=pl.Buffered(1)),
            pl.BlockSpec((1, d), lambda i: (0, 0)),
        ],
        out_specs=pl.BlockSpec((tm, d), lambda i: (i, 0)),
        out_shape=jax.ShapeDtypeStruct((t, d), F32),
        compiler_params=_cparams(("parallel",)),
        name="ple",
    )(h, nw, wg, p, wp, nf)


MATMUL_LEVELS = (3, 4)


def _gla_arg_matrix():
    c = GLA_CHUNK
    t = np.arange(c)[:, None]
    r = np.arange(c)[None, :]
    blocks = [(r <= t)]
    for lvl in MATMUL_LEVELS:
        n = c >> (lvl + 1)
        ref = (t // (2 * n)) * (2 * n) + n - 1
        lower = t > ref
        blocks.append(np.where(lower, (r > ref) & (r <= t), (r > t) & (r <= ref)))
    m = np.concatenate(blocks, axis=0).astype(np.float32)
    return np.concatenate([m, m, m, np.zeros_like(m)], axis=1)


def _coarse_level_args(b, n):
    out = []
    for blk in range(GLA_CHUNK // (2 * n)):
        r0 = blk * 2 * n
        ref = b[r0 + n - 1:r0 + n]
        out.append(ref - b[r0:r0 + n])
        out.append(b[r0 + n:r0 + 2 * n] - ref)
    return jnp.concatenate(out, axis=0)


def _gla_masks(pack):
    c = GLA_CHUNK
    t = np.arange(c)[:, None]
    s = np.arange(c)[None, :]
    masks = np.zeros((1 + N_LEVELS, c, c), np.float32)
    masks[0] = (t == s)
    for lvl in range(N_LEVELS):
        n = c >> (lvl + 1)
        same = (t // (2 * n)) == (s // (2 * n))
        masks[1 + lvl] = same & ((t % (2 * n)) >= n) & ((s % (2 * n)) < n)
    return np.tile(masks, (1, 1, pack))


def _block_diag_rows(x, pack, width):
    if pack == 1:
        return x
    head = lax.broadcasted_iota(jnp.int32, x.shape, 1) // width
    return jnp.concatenate([jnp.where(head == p, x, jnp.zeros_like(x)) for p in range(pack)], axis=0)


_NT = (((1,), (1,)), ((), ()))
_TN = (((0,), (0,)), ((), ()))


def _gla_core_kernel(*refs, mode, layer, scale, packs, pack, dk, dv):
    wk = pack * dk
    wv = pack * dv
    if mode == "gla":
        (q_ref, k_ref, v_ref, og_ref, glr_ref, wgk_ref, bgk_ref, gn_ref, mall_ref, masks_ref,
         o_ref, st_ref) = refs
    else:
        (q_ref, k_ref, v_ref, og_ref, lbl_ref, gn_ref, mall_ref, masks_ref, o_ref, st_ref) = refs

    @pl.when(pl.program_id(2) == 0)
    def _():
        st_ref[...] = jnp.zeros_like(st_ref)

    if mode == "hgrn":
        lg = lbl_ref[...]
        e = jnp.exp(lg - jnp.max(lg, axis=0, keepdims=True))
        sm = e / jnp.sum(e, axis=0, keepdims=True)
        lb_all = jnp.zeros_like(sm[0:1])
        for r in range(1, layer + 1):
            lb_all = lb_all + sm[r:r + 1]

    c = GLA_CHUNK
    c2 = 2 * c
    n_chunks = q_ref.shape[0] // c2
    odd_row = lax.broadcasted_iota(jnp.int32, (c, wk), 0) % 2 == 1

    def nt(a, b):
        return lax.dot_general(a, b, _NT, preferred_element_type=F32)

    def half_terms(qh, kh, gh):
        g_hi, g_mid, g_lo = _split3(gh)
        gs = jnp.concatenate([g_hi, g_mid, g_lo, jnp.zeros_like(g_hi)], axis=0)
        pre = jnp.dot(mall_ref[...], gs, preferred_element_type=F32)
        b = pre[0:c]
        lvl_args = [_coarse_level_args(b, c >> (lvl + 1)) for lvl in range(MATMUL_LEVELS[0])]
        lvl_args += [pre[c:2 * c], pre[2 * c:3 * c], jnp.where(odd_row, gh, 0.0)]
        att = masks_ref[0] * nt(qh.astype(BF16), _block_diag_rows(kh.astype(BF16), pack, dk))
        for lvl in range(N_LEVELS):
            n = c >> (lvl + 1)
            el = jnp.exp2(lvl_args[lvl])
            if n % 8 == 0:
                mixed = jnp.concatenate(
                    [x[r0:r0 + n] for blk in range(c // (2 * n))
                     for x, r0 in ((kh, blk * 2 * n), (qh, blk * 2 * n + n))], axis=0)
                ql = (mixed * el).astype(BF16)
                kl = ql
            else:
                ql = (qh * el).astype(BF16)
                kl = (kh * el).astype(BF16)
            att = att + masks_ref[1 + lvl] * nt(ql, _block_diag_rows(kl, pack, dk))
        return b, att

    def chunk(ci, carry):
        rows = pl.ds(pl.multiple_of(ci * c2, c2), c2)
        if mode == "gla":
            z_all = jnp.dot(glr_ref[rows, :].astype(BF16), wgk_ref[...],
                            preferred_element_type=F32) + bgk_ref[...]
        for g in range(packs):
            ks = slice(g * wk, (g + 1) * wk)
            vs = slice(g * wv, (g + 1) * wv)
            vc = v_ref[rows, vs]
            og = og_ref[rows, vs]
            if mode == "gla":
                qc = q_ref[rows, ks] * scale
                kc = k_ref[rows, ks]
                z = z_all[:, ks]
                gc = (jnp.minimum(z, 0.0) - jnp.log1p(jnp.exp(-jnp.abs(z)))) * (LOG2E / GLA_GATE_NORM)
                gate = og * jax.nn.sigmoid(og)
            else:
                qq = q_ref[rows, ks]
                qc = qq * jax.nn.sigmoid(qq) * scale
                lb = lb_all[:, ks]
                f = lb + (1.0 - lb) * jax.nn.sigmoid(k_ref[rows, ks])
                kc = 1.0 - f
                gc = jnp.log(f) * LOG2E
                gate = jax.nn.sigmoid(og)

            q0, q1 = qc[0:c], qc[c:c2]
            k0, k1 = kc[0:c], kc[c:c2]
            b0, att0 = half_terms(q0, k0, gc[0:c])
            b1, att1 = half_terms(q1, k1, gc[c:c2])
            e_b0 = jnp.exp2(b0)
            e_b1 = jnp.exp2(b1)
            e_rev0 = jnp.exp2(b0[c - 1:c] - b0)
            e_rev1 = jnp.exp2(b1[c - 1:c] - b1)
            e_l0 = e_b0[c - 1:c]
            e_l1 = e_b1[c - 1:c]

            k0_end = (k0 * e_rev0).astype(BF16)
            cross = nt((q1 * e_b1).astype(BF16), _block_diag_rows(k0_end, pack, dk))

            st = st_ref[g]
            q_in = jnp.concatenate([q0 * e_b0, q1 * (e_b1 * e_l0)], axis=0).astype(BF16)
            o = nt(q_in, st.astype(BF16))
            vb = vc.astype(BF16)
            v0 = _block_diag_rows(vb[0:c], pack, dv)
            v1 = _block_diag_rows(vb[c:c2], pack, dv)
            o0 = o[0:c] + jnp.dot(att0.astype(BF16), v0, preferred_element_type=F32)
            o1 = (o[c:c2] + jnp.dot(cross.astype(BF16), v0, preferred_element_type=F32)
                  + jnp.dot(att1.astype(BF16), v1, preferred_element_type=F32))
            o = jnp.concatenate([o0, o1], axis=0)

            k_dec = jnp.concatenate([k0 * (e_rev0 * e_l1), k1 * e_rev1], axis=0).astype(BF16)
            e_step = e_l0 * e_l1
            for p in range(pack):
                rs = slice(p * dv, (p + 1) * dv)
                cs = slice(p * dk, (p + 1) * dk)
                upd = lax.dot_general(vb[:, rs], k_dec[:, cs], _TN, preferred_element_type=F32)
                st_ref[g, rs, cs] = st[rs, cs] * e_step[:, cs] + upd

            for p in range(pack):
                hs = slice(p * dv, (p + 1) * dv)
                o_ref[rows, g * wv + p * dv:g * wv + (p + 1) * dv] = (
                    _rms(o[:, hs], gn_ref[...]) * gate[:, hs]).astype(o_ref.dtype)
        return carry

    lax.fori_loop(0, n_chunks, chunk, 0, unroll=2)


def _gla_core(proj, *, mode, layer, batch, seq, heads, hps, pack, dk, dv, col_q, col_k, col_v,
              col_og, extra, gn):
    tl = GLA_TOKENS_PER_STEP
    nl = seq // tl
    t = batch * seq
    wk = hps * dk
    wv = hps * dv
    mall = jnp.asarray(_gla_arg_matrix(), BF16)
    masks = jnp.asarray(_gla_masks(pack), F32)

    def rowblk(b, h, l):
        return b * nl + l

    in_specs = [
        pl.BlockSpec((tl, wk), lambda b, h, l: (rowblk(b, h, l), col_q // wk + h)),
        pl.BlockSpec((tl, wk), lambda b, h, l: (rowblk(b, h, l), col_k // wk + h)),
        pl.BlockSpec((tl, wv), lambda b, h, l: (rowblk(b, h, l), col_v // wv + h)),
        pl.BlockSpec((tl, wv), lambda b, h, l: (rowblk(b, h, l), col_og // wv + h)),
    ]
    args = [proj, proj, proj, proj]
    if mode == "gla":
        col_glr, wgk, bgk = extra
        in_specs += [
            pl.BlockSpec((tl, LANES), lambda b, h, l: (rowblk(b, h, l), col_glr // LANES)),
            pl.BlockSpec((LANES, wk), lambda b, h, l: (0, h)),
            pl.BlockSpec((1, wk), lambda b, h, l: (0, h)),
        ]
        args += [proj, wgk, bgk]
    else:
        (lbl,) = extra
        in_specs += [pl.BlockSpec((DEPTH, wk), lambda b, h, l: (0, h))]
        args += [lbl]
    in_specs += [
        pl.BlockSpec((1, dv), lambda b, h, l: (0, 0)),
        pl.BlockSpec(mall.shape, lambda b, h, l: (0, 0)),
        pl.BlockSpec(masks.shape, lambda b, h, l: (0, 0, 0)),
    ]
    args += [gn, mall, masks]
    return pl.pallas_call(
        functools.partial(_gla_core_kernel, mode=mode, layer=layer, scale=dk ** -0.5,
                          packs=hps // pack, pack=pack, dk=dk, dv=dv),
        grid=(batch, heads // hps, nl),
        in_specs=in_specs,
        out_specs=pl.BlockSpec((tl, wv), lambda b, h, l: (rowblk(b, h, l), h)),
        out_shape=jax.ShapeDtypeStruct((t, heads * dv), BF16),
        scratch_shapes=[pltpu.VMEM((hps // pack, pack * dv, pack * dk), F32)],
        compiler_params=_cparams(("parallel", "parallel", "arbitrary")),
        name=mode + "_core",
    )(*args)


N_PAIRS = SSM_HEADS // 2
PAIRS_PER_GROUP = N_PAIRS // SSM_GROUPS


def _ssd_kernel(z_ref, x_ref, bm_ref, cm_ref, dt_ref, cw_ref, cb_ref, dtb_ref, alog_ref, dsk_ref,
                nw_ref, tri3_ref, o_ref,
                ext_ref, xs_ref, bs_ref, cs_ref, st_ref, y_ref, bt_ref, dtt_ref, wdt_ref, elb_ref):
    c = SSD_CHUNK
    halo = CONV_HALO
    l = pl.program_id(1)

    @pl.when(l == 0)
    def _():
        st_ref[...] = jnp.zeros_like(st_ref)
        ext_ref[0:halo, :] = jnp.zeros((halo, ext_ref.shape[1]), F32)

    @pl.when(l > 0)
    def _():
        ext_ref[0:halo, :] = ext_ref[c:c + halo, :]

    ext_ref[halo:halo + c, 0:SSM_DINNER] = x_ref[...]
    ext_ref[halo:halo + c, SSM_DINNER:SSM_DINNER + 1024] = bm_ref[...]
    ext_ref[halo:halo + c, SSM_DINNER + 1024:SSM_CONV_DIM] = cm_ref[...]

    for slab in range(SSM_CONV_DIM // LANES):
        cs = slice(slab * LANES, (slab + 1) * LANES)
        acc = cb_ref[:, cs] + cw_ref[0:1, cs] * ext_ref[halo - 3:halo - 3 + c, cs]
        for j in range(1, SSM_CONV):
            acc = acc + cw_ref[j:j + 1, cs] * ext_ref[halo - 3 + j:halo - 3 + j + c, cs]
        act = acc * jax.nn.sigmoid(acc)
        if slab < N_PAIRS:
            xs_ref[slab] = act
        elif slab < N_PAIRS + SSM_GROUPS:
            bs_ref[slab - N_PAIRS] = act
        else:
            cs_ref[slab - N_PAIRS - SSM_GROUPS] = act

    lane = lax.broadcasted_iota(jnp.int32, (1, LANES), 1)
    dtr = dt_ref[...] + dtb_ref[...]
    dt = jnp.maximum(dtr, 0.0) + jnp.log1p(jnp.exp(-jnp.abs(dtr)))
    a = jnp.where(lane < SSM_HEADS, -jnp.exp(alog_ref[...]), 0.0)
    la_hi, la_mid, la_lo = _split3(dt * (a * LOG2E))
    b = jnp.dot(tri3_ref[...], jnp.concatenate([la_hi, la_mid, la_lo], axis=0),
                preferred_element_type=F32)
    b_last = b[c - 1:c]
    wd = jnp.exp2(b_last - b) * dt
    bt = b.T
    bt_ref[...] = bt
    dtt_ref[...] = dt.T
    wdt_ref[...] = wd.T
    elb_ref[...] = jnp.broadcast_to(jnp.exp2(bt[:, c - 1:c]), (LANES, LANES))

    row = lax.broadcasted_iota(jnp.int32, (c, c), 0)
    col = lax.broadcasted_iota(jnp.int32, (c, c), 1)
    causal = col <= row
    lo = lax.broadcasted_iota(jnp.int32, (c, LANES), 1) < SSM_HEADDIM

    def group(g, carry):
        bg = bs_ref[g]
        cg = cs_ref[g]
        cgb = cg.astype(BF16)
        cbm = lax.dot_general(cgb, bg.astype(BF16), _NT, preferred_element_type=F32)
        bgt = bg.T
        for j in range(PAIRS_PER_GROUP):
            p = g * PAIRS_PER_GROUP + j
            xp = xs_ref[p]
            sp = st_ref[p]
            x_lo = jnp.where(lo, xp, 0.0).astype(BF16)
            x_hi = jnp.where(lo, 0.0, xp).astype(BF16)
            x_bd = jnp.concatenate([x_lo, x_hi], axis=0)
            lhs_y = []
            lhs_s = []
            bhs = []
            for k in range(2):
                h = 2 * p + k
                bh = jnp.broadcast_to(bt_ref[pl.ds(h, 1), :], (c, c)).T
                rel = bh - bt_ref[pl.ds(h, 1), :]
                dec = jnp.exp2(jnp.where(causal, rel, MASKED_LOG2))
                lhs_y.append((cbm * dec * dtt_ref[pl.ds(h, 1), :]).astype(BF16))
                lhs_s.append((bgt * wdt_ref[pl.ds(h, 1), :]).astype(BF16))
                bhs.append(bh)
            y = jnp.exp2(jnp.where(lo, bhs[0], bhs[1])) * jnp.dot(
                cgb, sp.astype(BF16), preferred_element_type=F32)
            y = y + jnp.dot(jnp.concatenate(lhs_y, axis=1), x_bd, preferred_element_type=F32)
            su = jnp.dot(jnp.concatenate(lhs_s, axis=1), x_bd,
                         preferred_element_type=F32)
            el = jnp.where(lane < SSM_HEADDIM, elb_ref[pl.ds(2 * p, 1), :],
                           elb_ref[pl.ds(2 * p + 1, 1), :])
            st_ref[p] = sp * el + su
            y_ref[p] = y + dsk_ref[p] * xp
        return carry

    lax.fori_loop(0, SSM_GROUPS, group, 0, unroll=8)

    gw = PAIRS_PER_GROUP * LANES
    for g in range(SSM_GROUPS):
        cs = slice(g * gw, (g + 1) * gw)
        yg = jnp.concatenate([y_ref[g * PAIRS_PER_GROUP + j] for j in range(PAIRS_PER_GROUP)],
                             axis=1)
        zg = z_ref[:, cs]
        yg = yg * (zg * jax.nn.sigmoid(zg))
        o_ref[:, cs] = _rms(yg, nw_ref[:, cs]).astype(o_ref.dtype)


def _ssd_core(proj, conv_w, conv_b, dt_bias, a_log, d_skip, norm_w, *, batch, seq):
    c = SSD_CHUNK
    nl = seq // c
    t = batch * seq
    tri = np.tril(np.ones((c, c), np.float32))
    tri3 = jnp.asarray(np.concatenate([tri, tri, tri], axis=1), BF16)
    pad = LANES - SSM_HEADS
    dtb = jnp.pad(dt_bias.astype(F32), (0, pad)).reshape(1, LANES)
    alog = jnp.pad(a_log.astype(F32), (0, pad)).reshape(1, LANES)
    dsk = jnp.repeat(d_skip.astype(F32), SSM_HEADDIM).reshape(N_PAIRS, 1, LANES)

    def rb(b, l):
        return b * nl + l

    full2 = lambda b, l: (0, 0)
    in_specs = [
        pl.BlockSpec((c, SSM_DINNER), lambda b, l: (rb(b, l), 0)),
        pl.BlockSpec((c, SSM_DINNER), lambda b, l: (rb(b, l), 1)),
        pl.BlockSpec((c, 1024), lambda b, l: (rb(b, l), 2 * SSM_DINNER // 1024)),
        pl.BlockSpec((c, 1024), lambda b, l: (rb(b, l), 2 * SSM_DINNER // 1024 + 1)),
        pl.BlockSpec((c, LANES), lambda b, l: (rb(b, l), (SSM_DINNER + SSM_CONV_DIM) // LANES)),
        pl.BlockSpec((SSM_CONV, SSM_CONV_DIM), full2),
        pl.BlockSpec((1, SSM_CONV_DIM), full2),
        pl.BlockSpec((1, LANES), full2),
        pl.BlockSpec((1, LANES), full2),
        pl.BlockSpec((N_PAIRS, 1, LANES), lambda b, l: (0, 0, 0)),
        pl.BlockSpec((1, SSM_DINNER), full2),
        pl.BlockSpec(tri3.shape, full2),
    ]
    scratch = [
        pltpu.VMEM((CONV_HALO + c, SSM_CONV_DIM), F32),
        pltpu.VMEM((N_PAIRS, c, LANES), F32),
        pltpu.VMEM((SSM_GROUPS, c, SSM_STATE), F32),
        pltpu.VMEM((SSM_GROUPS, c, SSM_STATE), F32),
        pltpu.VMEM((N_PAIRS, SSM_STATE, LANES), F32),
        pltpu.VMEM((N_PAIRS, c, LANES), F32),
        pltpu.VMEM((LANES, c), F32),
        pltpu.VMEM((LANES, c), F32),
        pltpu.VMEM((LANES, c), F32),
        pltpu.VMEM((LANES, LANES), F32),
    ]
    return pl.pallas_call(
        _ssd_kernel,
        grid=(batch, nl),
        in_specs=in_specs,
        out_specs=pl.BlockSpec((c, SSM_DINNER), lambda b, l: (rb(b, l), 0)),
        out_shape=jax.ShapeDtypeStruct((t, SSM_DINNER), BF16),
        scratch_shapes=scratch,
        compiler_params=_cparams(("parallel", "arbitrary")),
        name="ssd_core",
    )(proj, proj, proj, proj, proj, conv_w.astype(F32), conv_b.astype(F32).reshape(1, -1),
      dtb, alog, dsk, norm_w.astype(F32).reshape(1, -1), tri3)


def _pad_last(w, n):
    return jnp.pad(w, [(0, 0)] * (w.ndim - 1) + [(0, n - w.shape[-1])])


def kernel(x, p, norm_mix, norm_mlp, norm_ple, norm_final, w_up, w_down, w_ple_proj, w_ple_gate,
           gla_w_in, gla_w_gk2, gla_b_gk, gla_gn, gla_w_out,
           hgrn_lb_logits, hgrn_w_in, hgrn_gn, hgrn_w_out,
           ssm_w_in, ssm_conv_w, ssm_conv_b, ssm_dt_bias, ssm_a_log, ssm_d, ssm_norm, ssm_w_out):
    batch, seq, d = x.shape
    t = batch * seq
    h = x.reshape(t, d)
    pf = p.reshape(DEPTH, t, PLE_DIM)
    row = lambda v: v.astype(F32).reshape(1, -1)

    gla_w_in_b = _pad_last(gla_w_in, GLA_IN_COLS).astype(BF16)
    hgrn_w_in_b = hgrn_w_in.astype(BF16)
    ssm_w_in_b = _pad_last(ssm_w_in, SSM_IN_COLS).astype(BF16)
    gla_w_out_b = gla_w_out.astype(BF16)
    hgrn_w_out_b = hgrn_w_out.astype(BF16)
    ssm_w_out_b = ssm_w_out.astype(BF16)
    w_up_b = w_up.astype(BF16)
    w_down_b = w_down.astype(BF16)
    w_gate_b = w_ple_gate.astype(BF16)
    w_proj_b = w_ple_proj.astype(BF16)

    for i in range(DEPTH):
        kind, j = i % N_MIXERS, i // N_MIXERS
        nw = row(norm_mix[i])
        if kind == 0:
            proj = _norm_matmul(h, nw, gla_w_in_b, j, tm=1024, tn=1280)
            kd = GLA_HEADS * GLA_DK
            vd = GLA_HEADS * GLA_DV
            wgk = jnp.pad(gla_w_gk2[j], ((0, LANES - GLA_GATE_RANK), (0, 0))).astype(BF16)
            o = _gla_core(proj, mode="gla", layer=i, batch=batch, seq=seq, heads=GLA_HEADS,
                          hps=GLA_HEADS, pack=1, dk=GLA_DK, dv=GLA_DV, col_q=0, col_k=kd, col_v=2 * kd,
                          col_og=2 * kd + vd,
                          extra=(2 * kd + 2 * vd, wgk, row(gla_b_gk[j])), gn=row(gla_gn[j]))
            h = _matmul_res(o, gla_w_out_b, j, h, tm=1024)
        elif kind == 1:
            proj = _norm_matmul(h, nw, hgrn_w_in_b, j, tm=1024, tn=1024)
            fd = HGRN_HEADS * HGRN_DK
            vd = HGRN_HEADS * HGRN_DV
            o = _gla_core(proj, mode="hgrn", layer=i, batch=batch, seq=seq, heads=HGRN_HEADS,
                          hps=HGRN_HEADS // 2, pack=2, dk=HGRN_DK, dv=HGRN_DV, col_q=0, col_k=fd,
                          col_v=2 * fd, col_og=2 * fd + vd,
                          extra=(hgrn_lb_logits.astype(F32),), gn=row(hgrn_gn[j]))
            h = _matmul_res(o, hgrn_w_out_b, j, h, tm=1024)
        else:
            proj = _norm_matmul(h, nw, ssm_w_in_b, j, tm=1024, tn=1792)
            o = _ssd_core(proj, ssm_conv_w[j], ssm_conv_b[j], ssm_dt_bias[j], ssm_a_log[j],
                          ssm_d[j], ssm_norm[j], batch=batch, seq=seq)
            h = _matmul_res(o, ssm_w_out_b, j, h, tm=512)
        h = _mlp(h, row(norm_mlp[i]), w_up_b, w_down_b, i, tm=1024, tf=1024)
        h = _ple(h, row(norm_ple[i]), w_gate_b, pf, w_proj_b, row(norm_final), i,
                 final_norm=(i == DEPTH - 1), tm=1024, tn=512)
    return h.reshape(batch, seq, d)
```

```python
import functools
import math

import numpy as np
import jax
import jax.numpy as jnp
from jax import lax
from jax.experimental import pallas as pl
from jax.experimental.pallas import tpu as pltpu

F32 = jnp.float32
BF16 = jnp.bfloat16

D_MODEL = 2048
DEPTH = 4
N_MIXERS = 3
PLE_DIM = 256
D_FF = 4 * D_MODEL
EPS = 1e-6

GLA_HEADS = 4
GLA_DK = 256
GLA_DV = 512
GLA_GATE_RANK = 16
GLA_GATE_NORM = 16.0
HGRN_HEADS = 16
HGRN_DK = 128
HGRN_DV = 128
SSM_DINNER = 4096
SSM_HEADDIM = 64
SSM_HEADS = 64
SSM_GROUPS = 8
SSM_STATE = 128
SSM_CONV = 4
SSM_CONV_DIM = SSM_DINNER + 2 * SSM_GROUPS * SSM_STATE

LANES = 128
VMEM_LIMIT_BYTES = 60 * 1024 * 1024

NORM_ROWS = 256
GLA_CHUNK = 64
GLA_TOKENS_PER_STEP = 512
SSD_CHUNK = 128
CONV_HALO_ROWS = 8 * (SSM_CONV - 1)

GLA_IN_COLS = 6400
SSM_IN_COLS = 10752
N_LEVELS = 6
LOG2E = math.log2(math.e)
MASKED_LOG2 = -1e30


def _cparams(sem):
    return pltpu.CompilerParams(dimension_semantics=sem, vmem_limit_bytes=VMEM_LIMIT_BYTES)


def _rms(x, w):
    ms = jnp.mean(x * x, axis=-1, keepdims=True)
    return x * lax.rsqrt(ms + EPS) * w


def _split3(x):
    hi = x.astype(BF16)
    r1 = x - hi.astype(F32)
    mid = r1.astype(BF16)
    lo = (r1 - mid.astype(F32)).astype(BF16)
    return hi, mid, lo


def _interleave_matrix():
    r = np.arange(SSD_CHUNK)
    p = np.zeros((SSD_CHUNK, SSD_CHUNK), np.float32)
    p[r, (r % 8) * (SSD_CHUNK // 8) + r // 8] = 1.0
    return p


def _norm_matmul_kernel(*refs, interleave):
    if interleave:
        h_ref, nw_ref, w_ref, pm_ref, o_ref, u_ref = refs
    else:
        h_ref, nw_ref, w_ref, o_ref, u_ref = refs
    j = pl.program_id(1)

    @pl.when(j == 0)
    def _():
        for r0 in range(0, h_ref.shape[0], NORM_ROWS):
            rs = slice(r0, r0 + NORM_ROWS)
            u = _rms(h_ref[rs, :], nw_ref[...]).astype(BF16)
            if interleave:
                u = jnp.concatenate(
                    [jnp.dot(pm_ref[...], u[g0:g0 + SSD_CHUNK], preferred_element_type=F32)
                     for g0 in range(0, NORM_ROWS, SSD_CHUNK)], axis=0).astype(BF16)
            u_ref[rs, :] = u
            o_ref[rs, :] = jnp.dot(u, w_ref[...], preferred_element_type=F32).astype(o_ref.dtype)

    @pl.when(j > 0)
    def _():
        o_ref[...] = jnp.dot(u_ref[...], w_ref[...], preferred_element_type=F32).astype(o_ref.dtype)


def _norm_matmul(h, nw, w, layer, *, tm, tn, interleave=False):
    t, d = h.shape
    n = w.shape[2]
    in_specs = [
        pl.BlockSpec((tm, d), lambda i, j: (i, 0)),
        pl.BlockSpec((1, d), lambda i, j: (0, 0)),
        pl.BlockSpec((None, d, tn), lambda i, j: (layer, 0, j)),
    ]
    args = [h, nw, w]
    if interleave:
        in_specs.append(pl.BlockSpec((SSD_CHUNK, SSD_CHUNK), lambda i, j: (0, 0)))
        args.append(jnp.asarray(_interleave_matrix(), BF16))
    return pl.pallas_call(
        functools.partial(_norm_matmul_kernel, interleave=interleave),
        grid=(t // tm, n // tn),
        in_specs=in_specs,
        out_specs=pl.BlockSpec((tm, tn), lambda i, j: (i, j)),
        out_shape=jax.ShapeDtypeStruct((t, n), F32),
        scratch_shapes=[pltpu.VMEM((tm, d), BF16)],
        compiler_params=_cparams(("parallel", "arbitrary")),
        name="norm_matmul",
    )(*args)


def _matmul_res_kernel(a_ref, w_ref, h_ref, o_ref):
    o_ref[...] = h_ref[...] + jnp.dot(a_ref[...], w_ref[...], preferred_element_type=F32)


def _matmul_res(a, w, layer, h, *, tm):
    t, k = a.shape
    n = w.shape[2]
    return pl.pallas_call(
        _matmul_res_kernel,
        grid=(t // tm,),
        in_specs=[
            pl.BlockSpec((tm, k), lambda i: (i, 0)),
            pl.BlockSpec((None, k, n), lambda i: (layer, 0, 0), pipeline_mode=pl.Buffered(1)),
            pl.BlockSpec((tm, n), lambda i: (i, 0)),
        ],
        out_specs=pl.BlockSpec((tm, n), lambda i: (i, 0)),
        out_shape=jax.ShapeDtypeStruct((t, n), F32),
        compiler_params=_cparams(("parallel",)),
        name="matmul_res",
    )(a, w, h)


def _mlp_kernel(h_ref, nw_ref, wu_ref, wd_ref, o_ref, u_ref):
    f = pl.program_id(1)

    def ff(u):
        a = jnp.dot(u, wu_ref[...], preferred_element_type=F32)
        a = jnp.square(jnp.maximum(a, 0.0)).astype(BF16)
        return jnp.dot(a, wd_ref[...], preferred_element_type=F32)

    @pl.when(f == 0)
    def _():
        for r0 in range(0, h_ref.shape[0], NORM_ROWS):
            rs = slice(r0, r0 + NORM_ROWS)
            x = h_ref[rs, :]
            u = _rms(x, nw_ref[...]).astype(BF16)
            u_ref[rs, :] = u
            o_ref[rs, :] = x + ff(u)

    @pl.when(f > 0)
    def _():
        o_ref[...] += ff(u_ref[...])


def _mlp(h, nw, wu, wd, layer, *, tm, tf):
    t, d = h.shape
    ff = wu.shape[2]
    return pl.pallas_call(
        _mlp_kernel,
        grid=(t // tm, ff // tf),
        in_specs=[
            pl.BlockSpec((tm, d), lambda i, f: (i, 0)),
            pl.BlockSpec((1, d), lambda i, f: (0, 0)),
            pl.BlockSpec((None, d, tf), lambda i, f: (layer, 0, f)),
            pl.BlockSpec((None, tf, d), lambda i, f: (layer, f, 0)),
        ],
        out_specs=pl.BlockSpec((tm, d), lambda i, f: (i, 0)),
        out_shape=jax.ShapeDtypeStruct((t, d), F32),
        scratch_shapes=[pltpu.VMEM((tm, d), BF16)],
        compiler_params=_cparams(("parallel", "arbitrary")),
        name="mlp",
    )(h, nw, wu, wd)


def _ple_kernel(h_ref, nw_ref, wg_ref, p_ref, wp_ref, nf_ref, o_ref, *, final_norm, tn):
    d = h_ref.shape[1]
    for r0 in range(0, h_ref.shape[0], NORM_ROWS):
        rs = slice(r0, r0 + NORM_ROWS)
        u = _rms(h_ref[rs, :], nw_ref[...]).astype(BF16)
        pb = p_ref[rs, :].astype(BF16)
        for n0 in range(0, d, tn):
            cs = slice(n0, n0 + tn)
            gate = jax.nn.sigmoid(jnp.dot(u, wg_ref[:, cs], preferred_element_type=F32))
            proj = jnp.dot(pb, wp_ref[:, cs], preferred_element_type=F32)
            o_ref[rs, cs] = h_ref[rs, cs] + gate * proj
        if final_norm:
            o_ref[rs, :] = _rms(o_ref[rs, :], nf_ref[...])


def _ple(h, nw, wg, p, wp, nf, layer, *, final_norm, tm, tn):
    t, d = h.shape
    pd = p.shape[2]
    return pl.pallas_call(
        functools.partial(_ple_kernel, final_norm=final_norm, tn=tn),
        grid=(t // tm,),
        in_specs=[
            pl.BlockSpec((tm, d), lambda i: (i, 0)),
            pl.BlockSpec((1, d), lambda i: (0, 0)),
            pl.BlockSpec((None, d, d), lambda i: (layer, 0, 0), pipeline_mode=pl.Buffered(1)),
            pl.BlockSpec((None, tm, pd), lambda i: (layer, i, 0)),
            pl.BlockSpec((None, pd, d), lambda i: (layer, 0, 0), pipeline_mode=pl.Buffered(1)),
            pl.BlockSpec((1, d), lambda i: (0, 0)),
        ],
        out_specs=pl.BlockSpec((tm, d), lambda i: (i, 0)),
        out_shape=jax.ShapeDtypeStruct((t, d), F32),
        compiler_params=_cparams(("parallel",)),
        name="ple",
    )(h, nw, wg, p, wp, nf)


MATMUL_LEVELS = (3, 4)


def _gla_arg_matrix():
    c = GLA_CHUNK
    t = np.arange(c)[:, None]
    r = np.arange(c)[None, :]
    blocks = [(r <= t)]
    for lvl in MATMUL_LEVELS:
        n = c >> (lvl + 1)
        ref = (t // (2 * n)) * (2 * n) + n - 1
        lower = t > ref
        blocks.append(np.where(lower, (r > ref) & (r <= t), (r > t) & (r <= ref)))
    m = np.concatenate(blocks, axis=0).astype(np.float32)
    return np.concatenate([m, m, m, np.zeros_like(m)], axis=1)


def _coarse_level_args(b, n):
    out = []
    for blk in range(GLA_CHUNK // (2 * n)):
        r0 = blk * 2 * n
        ref = b[r0 + n - 1:r0 + n]
        out.append(ref - b[r0:r0 + n])
        out.append(b[r0 + n:r0 + 2 * n] - ref)
    return jnp.concatenate(out, axis=0)


def _gla_masks(pack):
    c = GLA_CHUNK
    t = np.arange(c)[:, None]
    s = np.arange(c)[None, :]
    masks = np.zeros((1 + N_LEVELS, c, c), np.float32)
    masks[0] = (t == s)
    for lvl in range(N_LEVELS):
        n = c >> (lvl + 1)
        same = (t // (2 * n)) == (s // (2 * n))
        masks[1 + lvl] = same & ((t % (2 * n)) >= n) & ((s % (2 * n)) < n)
    return np.tile(masks, (1, 1, pack))


def _block_diag_rows(x, pack, width):
    if pack == 1:
        return x
    head = lax.broadcasted_iota(jnp.int32, x.shape, 1) // width
    return jnp.concatenate([jnp.where(head == p, x, jnp.zeros_like(x)) for p in range(pack)], axis=0)


_NT = (((1,), (1,)), ((), ()))
_TN = (((0,), (0,)), ((), ()))


def _gla_core_kernel(*refs, mode, layer, scale, packs, pack, dk, dv):
    wk = pack * dk
    wv = pack * dv
    if mode == "gla":
        (q_ref, k_ref, v_ref, og_ref, glr_ref, wgk_ref, bgk_ref, gn_ref, mall_ref, masks_ref,
         o_ref, st_ref) = refs
    else:
        (q_ref, k_ref, v_ref, og_ref, lbl_ref, gn_ref, mall_ref, masks_ref, o_ref, st_ref) = refs

    @pl.when(pl.program_id(2) == 0)
    def _():
        st_ref[...] = jnp.zeros_like(st_ref)

    if mode == "hgrn":
        lg = lbl_ref[...]
        e = jnp.exp(lg - jnp.max(lg, axis=0, keepdims=True))
        sm = e / jnp.sum(e, axis=0, keepdims=True)
        lb_all = jnp.zeros_like(sm[0:1])
        for r in range(1, layer + 1):
            lb_all = lb_all + sm[r:r + 1]

    c = GLA_CHUNK
    c2 = 2 * c
    n_chunks = q_ref.shape[0] // c2
    odd_row = lax.broadcasted_iota(jnp.int32, (c, wk), 0) % 2 == 1

    def nt(a, b):
        return lax.dot_general(a, b, _NT, preferred_element_type=F32)

    def half_terms(qh, kh, gh):
        g_hi, g_mid, g_lo = _split3(gh)
        gs = jnp.concatenate([g_hi, g_mid, g_lo, jnp.zeros_like(g_hi)], axis=0)
        pre = jnp.dot(mall_ref[...], gs, preferred_element_type=F32)
        b = pre[0:c]
        lvl_args = [_coarse_level_args(b, c >> (lvl + 1)) for lvl in range(MATMUL_LEVELS[0])]
        lvl_args += [pre[c:2 * c], pre[2 * c:3 * c], jnp.where(odd_row, gh, 0.0)]
        att = masks_ref[0] * nt(qh.astype(BF16), _block_diag_rows(kh.astype(BF16), pack, dk))
        for lvl in range(N_LEVELS):
            n = c >> (lvl + 1)
            el = jnp.exp2(lvl_args[lvl])
            if n % 8 == 0:
                mixed = jnp.concatenate(
                    [x[r0:r0 + n] for blk in range(c // (2 * n))
                     for x, r0 in ((kh, blk * 2 * n), (qh, blk * 2 * n + n))], axis=0)
                ql = (mixed * el).astype(BF16)
                kl = ql
            else:
                ql = (qh * el).astype(BF16)
                kl = (kh * el).astype(BF16)
            att = att + masks_ref[1 + lvl] * nt(ql, _block_diag_rows(kl, pack, dk))
        return b, att

    def chunk(ci, carry):
        rows = pl.ds(pl.multiple_of(ci * c2, c2), c2)
        if mode == "gla":
            z_all = jnp.dot(glr_ref[rows, :].astype(BF16), wgk_ref[...],
                            preferred_element_type=F32) + bgk_ref[...]
        for g in range(packs):
            ks = slice(g * wk, (g + 1) * wk)
            vs = slice(g * wv, (g + 1) * wv)
            vc = v_ref[rows, vs]
            og = og_ref[rows, vs]
            if mode == "gla":
                qc = q_ref[rows, ks] * scale
                kc = k_ref[rows, ks]
                z = z_all[:, ks]
                gc = (jnp.minimum(z, 0.0) - jnp.log1p(jnp.exp(-jnp.abs(z)))) * (LOG2E / GLA_GATE_NORM)
                gate = og * jax.nn.sigmoid(og)
            else:
                qq = q_ref[rows, ks]
                qc = qq * jax.nn.sigmoid(qq) * scale
                lb = lb_all[:, ks]
                f = lb + (1.0 - lb) * jax.nn.sigmoid(k_ref[rows, ks])
                kc = 1.0 - f
                gc = jnp.log(f) * LOG2E
                gate = jax.nn.sigmoid(og)

            q0, q1 = qc[0:c], qc[c:c2]
            k0, k1 = kc[0:c], kc[c:c2]
            b0, att0 = half_terms(q0, k0, gc[0:c])
            b1, att1 = half_terms(q1, k1, gc[c:c2])
            e_b0 = jnp.exp2(b0)
            e_b1 = jnp.exp2(b1)
            e_rev0 = jnp.exp2(b0[c - 1:c] - b0)
            e_rev1 = jnp.exp2(b1[c - 1:c] - b1)
            e_l0 = e_b0[c - 1:c]
            e_l1 = e_b1[c - 1:c]

            k0_end = (k0 * e_rev0).astype(BF16)
            cross = nt((q1 * e_b1).astype(BF16), _block_diag_rows(k0_end, pack, dk))

            st = st_ref[g]
            q_in = jnp.concatenate([q0 * e_b0, q1 * (e_b1 * e_l0)], axis=0).astype(BF16)
            o = nt(q_in, st.astype(BF16))
            vb = vc.astype(BF16)
            v0 = _block_diag_rows(vb[0:c], pack, dv)
            v1 = _block_diag_rows(vb[c:c2], pack, dv)
            o0 = o[0:c] + jnp.dot(att0.astype(BF16), v0, preferred_element_type=F32)
            o1 = (o[c:c2] + jnp.dot(cross.astype(BF16), v0, preferred_element_type=F32)
                  + jnp.dot(att1.astype(BF16), v1, preferred_element_type=F32))
            o = jnp.concatenate([o0, o1], axis=0)

            k_dec = jnp.concatenate([k0 * (e_rev0 * e_l1), k1 * e_rev1], axis=0).astype(BF16)
            e_step = e_l0 * e_l1
            for p in range(pack):
                rs = slice(p * dv, (p + 1) * dv)
                cs = slice(p * dk, (p + 1) * dk)
                upd = lax.dot_general(vb[:, rs], k_dec[:, cs], _TN, preferred_element_type=F32)
                st_ref[g, rs, cs] = st[rs, cs] * e_step[:, cs] + upd

            for p in range(pack):
                hs = slice(p * dv, (p + 1) * dv)
                o_ref[rows, g * wv + p * dv:g * wv + (p + 1) * dv] = (
                    _rms(o[:, hs], gn_ref[...]) * gate[:, hs]).astype(o_ref.dtype)
        return carry

    lax.fori_loop(0, n_chunks, chunk, 0, unroll=2)


def _gla_core(proj, *, mode, layer, batch, seq, heads, hps, pack, dk, dv, col_q, col_k, col_v,
              col_og, extra, gn):
    tl = GLA_TOKENS_PER_STEP
    nl = seq // tl
    t = batch * seq
    wk = hps * dk
    wv = hps * dv
    mall = jnp.asarray(_gla_arg_matrix(), BF16)
    masks = jnp.asarray(_gla_masks(pack), F32)

    def rowblk(b, h, l):
        return b * nl + l

    in_specs = [
        pl.BlockSpec((tl, wk), lambda b, h, l: (rowblk(b, h, l), col_q // wk + h)),
        pl.BlockSpec((tl, wk), lambda b, h, l: (rowblk(b, h, l), col_k // wk + h)),
        pl.BlockSpec((tl, wv), lambda b, h, l: (rowblk(b, h, l), col_v // wv + h)),
        pl.BlockSpec((tl, wv), lambda b, h, l: (rowblk(b, h, l), col_og // wv + h)),
    ]
    args = [proj, proj, proj, proj]
    if mode == "gla":
        col_glr, wgk, bgk = extra
        in_specs += [
            pl.BlockSpec((tl, LANES), lambda b, h, l: (rowblk(b, h, l), col_glr // LANES)),
            pl.BlockSpec((LANES, wk), lambda b, h, l: (0, h)),
            pl.BlockSpec((1, wk), lambda b, h, l: (0, h)),
        ]
        args += [proj, wgk, bgk]
    else:
        (lbl,) = extra
        in_specs += [pl.BlockSpec((DEPTH, wk), lambda b, h, l: (0, h))]
        args += [lbl]
    in_specs += [
        pl.BlockSpec((1, dv), lambda b, h, l: (0, 0)),
        pl.BlockSpec(mall.shape, lambda b, h, l: (0, 0)),
        pl.BlockSpec(masks.shape, lambda b, h, l: (0, 0, 0)),
    ]
    args += [gn, mall, masks]
    return pl.pallas_call(
        functools.partial(_gla_core_kernel, mode=mode, layer=layer, scale=dk ** -0.5,
                          packs=hps // pack, pack=pack, dk=dk, dv=dv),
        grid=(batch, heads // hps, nl),
        in_specs=in_specs,
        out_specs=pl.BlockSpec((tl, wv), lambda b, h, l: (rowblk(b, h, l), h)),
        out_shape=jax.ShapeDtypeStruct((t, heads * dv), BF16),
        scratch_shapes=[pltpu.VMEM((hps // pack, pack * dv, pack * dk), F32)],
        compiler_params=_cparams(("parallel", "parallel", "arbitrary")),
        name=mode + "_core",
    )(*args)


N_PAIRS = SSM_HEADS // 2
PAIRS_PER_GROUP = N_PAIRS // SSM_GROUPS


def _ssd_kernel(z_ref, x_ref, bm_ref, cm_ref, dt_ref, cw_ref, cb_ref, dtb_ref, alog_ref, dsk_ref,
                nw_ref, tri3_ref, unperm_ref, o_ref,
                halo_ref, xs_ref, bs_ref, cs_ref, st_ref, y_ref, bt_ref, dtt_ref, wdt_ref, elb_ref):
    c = SSD_CHUNK
    nt_rows = c // 8
    hr = CONV_HALO_ROWS
    l = pl.program_id(1)

    @pl.when(l == 0)
    def _():
        st_ref[...] = jnp.zeros_like(st_ref)
        halo_ref[...] = jnp.zeros_like(halo_ref)

    first_sublane = lax.broadcasted_iota(jnp.int32, (8, LANES), 0) == 0
    for slab in range(SSM_CONV_DIM // LANES):
        cs = slice(slab * LANES, (slab + 1) * LANES)
        if slab < SSM_DINNER // LANES:
            cur = x_ref[:, cs]
        elif slab < (SSM_DINNER + SSM_GROUPS * SSM_STATE) // LANES:
            cur = bm_ref[:, slab * LANES - SSM_DINNER:(slab + 1) * LANES - SSM_DINNER]
        else:
            off = SSM_DINNER + SSM_GROUPS * SSM_STATE
            cur = cm_ref[:, slab * LANES - off:(slab + 1) * LANES - off]
        prev_tail = halo_ref[:, cs]
        wrapped = []
        for s in range(SSM_CONV - 1):
            tail = cur[c - hr + 8 * s:c - hr + 8 * (s + 1)]
            moved = pltpu.roll(tail, 1, 0)
            wrapped.append(jnp.where(first_sublane, prev_tail[8 * s + 7:8 * s + 8], moved))
        halo_ref[:, cs] = cur[c - hr:c]
        acc = cb_ref[:, cs] + cw_ref[SSM_CONV - 1:SSM_CONV, cs] * cur
        for s in range(1, SSM_CONV):
            shifted = jnp.concatenate(wrapped[SSM_CONV - 1 - s:] + [cur[0:c - 8 * s]], axis=0)
            acc = acc + cw_ref[SSM_CONV - 1 - s:SSM_CONV - s, cs] * shifted
        act = acc * jax.nn.sigmoid(acc)
        if slab < N_PAIRS:
            xs_ref[slab] = act
        elif slab < N_PAIRS + SSM_GROUPS:
            bs_ref[slab - N_PAIRS] = act
        else:
            cs_ref[slab - N_PAIRS - SSM_GROUPS] = act

    lane = lax.broadcasted_iota(jnp.int32, (1, LANES), 1)
    dtr = dt_ref[...] + dtb_ref[...]
    dt = jnp.maximum(dtr, 0.0) + jnp.log1p(jnp.exp(-jnp.abs(dtr)))
    a = jnp.where(lane < SSM_HEADS, -jnp.exp(alog_ref[...]), 0.0)
    la_hi, la_mid, la_lo = _split3(dt * (a * LOG2E))
    b = jnp.dot(tri3_ref[...], jnp.concatenate([la_hi, la_mid, la_lo], axis=0),
                preferred_element_type=F32)
    b_last = b[c - 1:c]
    wd = jnp.exp2(b_last - b) * dt
    bt = b.T
    bt_ref[...] = bt
    dtt_ref[...] = dt.T
    wdt_ref[...] = wd.T
    elb_ref[...] = jnp.broadcast_to(jnp.exp2(bt[:, c - 1:c]), (LANES, LANES))

    row = lax.broadcasted_iota(jnp.int32, (c, c), 0)
    col = lax.broadcasted_iota(jnp.int32, (c, c), 1)
    causal = (col % 8) * nt_rows + col // 8 <= (row % 8) * nt_rows + row // 8
    lo = lax.broadcasted_iota(jnp.int32, (c, LANES), 1) < SSM_HEADDIM

    def group(g, carry):
        bg = bs_ref[g]
        cg = cs_ref[g]
        cgb = cg.astype(BF16)
        cbm = lax.dot_general(cgb, bg.astype(BF16), _NT, preferred_element_type=F32)
        bgt = bg.T
        for j in range(PAIRS_PER_GROUP):
            p = g * PAIRS_PER_GROUP + j
            xp = xs_ref[p]
            sp = st_ref[p]
            x_lo = jnp.where(lo, xp, 0.0).astype(BF16)
            x_hi = jnp.where(lo, 0.0, xp).astype(BF16)
            x_bd = jnp.concatenate([x_lo, x_hi], axis=0)
            lhs_y = []
            lhs_s = []
            bhs = []
            for k in range(2):
                h = 2 * p + k
                bh = jnp.broadcast_to(bt_ref[pl.ds(h, 1), :], (c, c)).T
                rel = bh - bt_ref[pl.ds(h, 1), :]
                dec = jnp.exp2(jnp.where(causal, rel, MASKED_LOG2))
                lhs_y.append((cbm * dec * dtt_ref[pl.ds(h, 1), :]).astype(BF16))
                lhs_s.append((bgt * wdt_ref[pl.ds(h, 1), :]).astype(BF16))
                bhs.append(bh)
            y = jnp.exp2(jnp.where(lo, bhs[0], bhs[1])) * jnp.dot(
                cgb, sp.astype(BF16), preferred_element_type=F32)
            y = y + jnp.dot(jnp.concatenate(lhs_y, axis=1), x_bd, preferred_element_type=F32)
            su = jnp.dot(jnp.concatenate(lhs_s, axis=1), x_bd,
                         preferred_element_type=F32)
            el = jnp.where(lane < SSM_HEADDIM, elb_ref[pl.ds(2 * p, 1), :],
                           elb_ref[pl.ds(2 * p + 1, 1), :])
            st_ref[p] = sp * el + su
            y_ref[p] = y + dsk_ref[p] * xp
        return carry

    lax.fori_loop(0, SSM_GROUPS, group, 0, unroll=8)

    gw = PAIRS_PER_GROUP * LANES
    for g in range(SSM_GROUPS):
        cs = slice(g * gw, (g + 1) * gw)
        yg = jnp.concatenate([y_ref[g * PAIRS_PER_GROUP + j] for j in range(PAIRS_PER_GROUP)],
                             axis=1)
        zg = z_ref[:, cs]
        yg = yg * (zg * jax.nn.sigmoid(zg))
        og = _rms(yg, nw_ref[:, cs]).astype(BF16)
        o_ref[:, cs] = jnp.dot(unperm_ref[...], og, preferred_element_type=F32).astype(o_ref.dtype)


def _ssd_core(proj, conv_w, conv_b, dt_bias, a_log, d_skip, norm_w, *, batch, seq):
    c = SSD_CHUNK
    nl = seq // c
    t = batch * seq
    perm = _interleave_matrix()
    tok = perm.argmax(axis=1)
    tri = (tok[None, :] <= tok[:, None]).astype(np.float32)
    tri3 = jnp.asarray(np.concatenate([tri, tri, tri], axis=1), BF16)
    unperm = jnp.asarray(perm.T, BF16)
    pad = LANES - SSM_HEADS
    dtb = jnp.pad(dt_bias.astype(F32), (0, pad)).reshape(1, LANES)
    alog = jnp.pad(a_log.astype(F32), (0, pad)).reshape(1, LANES)
    dsk = jnp.repeat(d_skip.astype(F32), SSM_HEADDIM).reshape(N_PAIRS, 1, LANES)

    def rb(b, l):
        return b * nl + l

    full2 = lambda b, l: (0, 0)
    in_specs = [
        pl.BlockSpec((c, SSM_DINNER), lambda b, l: (rb(b, l), 0)),
        pl.BlockSpec((c, SSM_DINNER), lambda b, l: (rb(b, l), 1)),
        pl.BlockSpec((c, 1024), lambda b, l: (rb(b, l), 2 * SSM_DINNER // 1024)),
        pl.BlockSpec((c, 1024), lambda b, l: (rb(b, l), 2 * SSM_DINNER // 1024 + 1)),
        pl.BlockSpec((c, LANES), lambda b, l: (rb(b, l), (SSM_DINNER + SSM_CONV_DIM) // LANES)),
        pl.BlockSpec((SSM_CONV, SSM_CONV_DIM), full2),
        pl.BlockSpec((1, SSM_CONV_DIM), full2),
        pl.BlockSpec((1, LANES), full2),
        pl.BlockSpec((1, LANES), full2),
        pl.BlockSpec((N_PAIRS, 1, LANES), lambda b, l: (0, 0, 0)),
        pl.BlockSpec((1, SSM_DINNER), full2),
        pl.BlockSpec(tri3.shape, full2),
        pl.BlockSpec(unperm.shape, full2),
    ]
    scratch = [
        pltpu.VMEM((CONV_HALO_ROWS, SSM_CONV_DIM), F32),
        pltpu.VMEM((N_PAIRS, c, LANES), F32),
        pltpu.VMEM((SSM_GROUPS, c, SSM_STATE), F32),
        pltpu.VMEM((SSM_GROUPS, c, SSM_STATE), F32),
        pltpu.VMEM((N_PAIRS, SSM_STATE, LANES), F32),
        pltpu.VMEM((N_PAIRS, c, LANES), F32),
        pltpu.VMEM((LANES, c), F32),
        pltpu.VMEM((LANES, c), F32),
        pltpu.VMEM((LANES, c), F32),
        pltpu.VMEM((LANES, LANES), F32),
    ]
    return pl.pallas_call(
        _ssd_kernel,
        grid=(batch, nl),
        in_specs=in_specs,
        out_specs=pl.BlockSpec((c, SSM_DINNER), lambda b, l: (rb(b, l), 0)),
        out_shape=jax.ShapeDtypeStruct((t, SSM_DINNER), BF16),
        scratch_shapes=scratch,
        compiler_params=_cparams(("parallel", "arbitrary")),
        name="ssd_core",
    )(proj, proj, proj, proj, proj, conv_w.astype(F32), conv_b.astype(F32).reshape(1, -1),
      dtb, alog, dsk, norm_w.astype(F32).reshape(1, -1), tri3, unperm)


def _pad_last(w, n):
    return jnp.pad(w, [(0, 0)] * (w.ndim - 1) + [(0, n - w.shape[-1])])


def kernel(x, p, norm_mix, norm_mlp, norm_ple, norm_final, w_up, w_down, w_ple_proj, w_ple_gate,
           gla_w_in, gla_w_gk2, gla_b_gk, gla_gn, gla_w_out,
           hgrn_lb_logits, hgrn_w_in, hgrn_gn, hgrn_w_out,
           ssm_w_in, ssm_conv_w, ssm_conv_b, ssm_dt_bias, ssm_a_log, ssm_d, ssm_norm, ssm_w_out):
    batch, seq, d = x.shape
    t = batch * seq
    h = x.reshape(t, d)
    pf = p.reshape(DEPTH, t, PLE_DIM)
    row = lambda v: v.astype(F32).reshape(1, -1)

    gla_w_in_b = _pad_last(gla_w_in, GLA_IN_COLS).astype(BF16)
    hgrn_w_in_b = hgrn_w_in.astype(BF16)
    ssm_w_in_b = _pad_last(ssm_w_in, SSM_IN_COLS).astype(BF16)
    gla_w_out_b = gla_w_out.astype(BF16)
    hgrn_w_out_b = hgrn_w_out.astype(BF16)
    ssm_w_out_b = ssm_w_out.astype(BF16)
    w_up_b = w_up.astype(BF16)
    w_down_b = w_down.astype(BF16)
    w_gate_b = w_ple_gate.astype(BF16)
    w_proj_b = w_ple_proj.astype(BF16)

    for i in range(DEPTH):
        kind, j = i % N_MIXERS, i // N_MIXERS
        nw = row(norm_mix[i])
        if kind == 0:
            proj = _norm_matmul(h, nw, gla_w_in_b, j, tm=1024, tn=1280)
            kd = GLA_HEADS * GLA_DK
            vd = GLA_HEADS * GLA_DV
            wgk = jnp.pad(gla_w_gk2[j], ((0, LANES - GLA_GATE_RANK), (0, 0))).astype(BF16)
            o = _gla_core(proj, mode="gla", layer=i, batch=batch, seq=seq, heads=GLA_HEADS,
                          hps=GLA_HEADS, pack=1, dk=GLA_DK, dv=GLA_DV, col_q=0, col_k=kd, col_v=2 * kd,
                          col_og=2 * kd + vd,
                          extra=(2 * kd + 2 * vd, wgk, row(gla_b_gk[j])), gn=row(gla_gn[j]))
            h = _matmul_res(o, gla_w_out_b, j, h, tm=1024)
        elif kind == 1:
            proj = _norm_matmul(h, nw, hgrn_w_in_b, j, tm=1024, tn=2048)
            fd = HGRN_HEADS * HGRN_DK
            vd = HGRN_HEADS * HGRN_DV
            o = _gla_core(proj, mode="hgrn", layer=i, batch=batch, seq=seq, heads=HGRN_HEADS,
                          hps=HGRN_HEADS // 2, pack=2, dk=HGRN_DK, dv=HGRN_DV, col_q=0, col_k=fd,
                          col_v=2 * fd, col_og=2 * fd + vd,
                          extra=(hgrn_lb_logits.astype(F32),), gn=row(hgrn_gn[j]))
            h = _matmul_res(o, hgrn_w_out_b, j, h, tm=1024)
        else:
            proj = _norm_matmul(h, nw, ssm_w_in_b, j, tm=1024, tn=1792, interleave=True)
            o = _ssd_core(proj, ssm_conv_w[j], ssm_conv_b[j], ssm_dt_bias[j], ssm_a_log[j],
                          ssm_d[j], ssm_norm[j], batch=batch, seq=seq)
            h = _matmul_res(o, ssm_w_out_b, j, h, tm=512)
        h = _mlp(h, row(norm_mlp[i]), w_up_b, w_down_b, i, tm=1024, tf=1024)
        h = _ple(h, row(norm_ple[i]), w_gate_b, pf, w_proj_b, row(norm_final), i,
                 final_norm=(i == DEPTH - 1), tm=1024, tn=512)
    return h.reshape(batch, seq, d)
```

```python
import functools
import math

import numpy as np
import jax
import jax.numpy as jnp
from jax import lax
from jax.experimental import pallas as pl
from jax.experimental.pallas import tpu as pltpu

F32 = jnp.float32
BF16 = jnp.bfloat16

D_MODEL = 2048
DEPTH = 4
N_MIXERS = 3
PLE_DIM = 256
D_FF = 4 * D_MODEL
EPS = 1e-6

GLA_HEADS = 4
GLA_DK = 256
GLA_DV = 512
GLA_GATE_RANK = 16
GLA_GATE_NORM = 16.0
HGRN_HEADS = 16
HGRN_DK = 128
HGRN_DV = 128
SSM_DINNER = 4096
SSM_HEADDIM = 64
SSM_HEADS = 64
SSM_GROUPS = 8
SSM_STATE = 128
SSM_CONV = 4
SSM_CONV_DIM = SSM_DINNER + 2 * SSM_GROUPS * SSM_STATE

LANES = 128
VMEM_LIMIT_BYTES = 60 * 1024 * 1024

NORM_ROWS = 256
GLA_CHUNK = 64
GLA_TOKENS_PER_STEP = 512
SSD_CHUNK = 128
CONV_HALO_ROWS = 8 * (SSM_CONV - 1)

GLA_IN_COLS = 6400
SSM_IN_COLS = 10752
N_LEVELS = 6
LOG2E = math.log2(math.e)
MASKED_LOG2 = -1e30


def _cparams(sem):
    return pltpu.CompilerParams(dimension_semantics=sem, vmem_limit_bytes=VMEM_LIMIT_BYTES)


def _rms(x, w):
    ms = jnp.mean(x * x, axis=-1, keepdims=True)
    return x * lax.rsqrt(ms + EPS) * w


def _split3(x):
    hi = x.astype(BF16)
    r1 = x - hi.astype(F32)
    mid = r1.astype(BF16)
    lo = (r1 - mid.astype(F32)).astype(BF16)
    return hi, mid, lo


def _interleave_matrix():
    r = np.arange(SSD_CHUNK)
    p = np.zeros((SSD_CHUNK, SSD_CHUNK), np.float32)
    p[r, (r % 8) * (SSD_CHUNK // 8) + r // 8] = 1.0
    return p


def _norm_matmul_kernel(*refs, interleave):
    if interleave:
        h_ref, nw_ref, w_ref, pm_ref, o_ref, u_ref = refs
    else:
        h_ref, nw_ref, w_ref, o_ref, u_ref = refs
    j = pl.program_id(1)

    @pl.when(j == 0)
    def _():
        for r0 in range(0, h_ref.shape[0], NORM_ROWS):
            rs = slice(r0, r0 + NORM_ROWS)
            u = _rms(h_ref[rs, :], nw_ref[...]).astype(BF16)
            if interleave:
                u = jnp.concatenate(
                    [jnp.dot(pm_ref[...], u[g0:g0 + SSD_CHUNK], preferred_element_type=F32)
                     for g0 in range(0, NORM_ROWS, SSD_CHUNK)], axis=0).astype(BF16)
            u_ref[rs, :] = u
            o_ref[rs, :] = jnp.dot(u, w_ref[...], preferred_element_type=F32).astype(o_ref.dtype)

    @pl.when(j > 0)
    def _():
        o_ref[...] = jnp.dot(u_ref[...], w_ref[...], preferred_element_type=F32).astype(o_ref.dtype)


def _norm_matmul(h, nw, w, layer, *, tm, tn, interleave=False):
    t, d = h.shape
    n = w.shape[2]
    in_specs = [
        pl.BlockSpec((tm, d), lambda i, j: (i, 0)),
        pl.BlockSpec((1, d), lambda i, j: (0, 0)),
        pl.BlockSpec((None, d, tn), lambda i, j: (layer, 0, j)),
    ]
    args = [h, nw, w]
    if interleave:
        in_specs.append(pl.BlockSpec((SSD_CHUNK, SSD_CHUNK), lambda i, j: (0, 0)))
        args.append(jnp.asarray(_interleave_matrix(), BF16))
    return pl.pallas_call(
        functools.partial(_norm_matmul_kernel, interleave=interleave),
        grid=(t // tm, n // tn),
        in_specs=in_specs,
        out_specs=pl.BlockSpec((tm, tn), lambda i, j: (i, j)),
        out_shape=jax.ShapeDtypeStruct((t, n), F32),
        scratch_shapes=[pltpu.VMEM((tm, d), BF16)],
        compiler_params=_cparams(("parallel", "arbitrary")),
        name="norm_matmul",
    )(*args)


def _matmul_res_kernel(a_ref, w_ref, h_ref, o_ref):
    o_ref[...] = h_ref[...] + jnp.dot(a_ref[...], w_ref[...], preferred_element_type=F32)


def _matmul_res(a, w, layer, h, *, tm):
    t, k = a.shape
    n = w.shape[2]
    return pl.pallas_call(
        _matmul_res_kernel,
        grid=(t // tm,),
        in_specs=[
            pl.BlockSpec((tm, k), lambda i: (i, 0)),
            pl.BlockSpec((None, k, n), lambda i: (layer, 0, 0), pipeline_mode=pl.Buffered(1)),
            pl.BlockSpec((tm, n), lambda i: (i, 0)),
        ],
        out_specs=pl.BlockSpec((tm, n), lambda i: (i, 0)),
        out_shape=jax.ShapeDtypeStruct((t, n), F32),
        compiler_params=_cparams(("parallel",)),
        name="matmul_res",
    )(a, w, h)


def _mlp_kernel(h_ref, nw_ref, wu_ref, wd_ref, o_ref, u_ref):
    f = pl.program_id(1)

    def ff(u):
        a = jnp.dot(u, wu_ref[...], preferred_element_type=F32)
        a = jnp.square(jnp.maximum(a, 0.0)).astype(BF16)
        return jnp.dot(a, wd_ref[...], preferred_element_type=F32)

    @pl.when(f == 0)
    def _():
        for r0 in range(0, h_ref.shape[0], NORM_ROWS):
            rs = slice(r0, r0 + NORM_ROWS)
            x = h_ref[rs, :]
            u = _rms(x, nw_ref[...]).astype(BF16)
            u_ref[rs, :] = u
            o_ref[rs, :] = x + ff(u)

    @pl.when(f > 0)
    def _():
        o_ref[...] += ff(u_ref[...])


def _mlp(h, nw, wu, wd, layer, *, tm, tf):
    t, d = h.shape
    ff = wu.shape[2]
    return pl.pallas_call(
        _mlp_kernel,
        grid=(t // tm, ff // tf),
        in_specs=[
            pl.BlockSpec((tm, d), lambda i, f: (i, 0)),
            pl.BlockSpec((1, d), lambda i, f: (0, 0)),
            pl.BlockSpec((None, d, tf), lambda i, f: (layer, 0, f)),
            pl.BlockSpec((None, tf, d), lambda i, f: (layer, f, 0)),
        ],
        out_specs=pl.BlockSpec((tm, d), lambda i, f: (i, 0)),
        out_shape=jax.ShapeDtypeStruct((t, d), F32),
        scratch_shapes=[pltpu.VMEM((tm, d), BF16)],
        compiler_params=_cparams(("parallel", "arbitrary")),
        name="mlp",
    )(h, nw, wu, wd)


def _ple_kernel(h_ref, nw_ref, wg_ref, p_ref, wp_ref, nf_ref, o_ref, *, final_norm, tn):
    d = h_ref.shape[1]
    for r0 in range(0, h_ref.shape[0], NORM_ROWS):
        rs = slice(r0, r0 + NORM_ROWS)
        u = _rms(h_ref[rs, :], nw_ref[...]).astype(BF16)
        pb = p_ref[rs, :].astype(BF16)
        for n0 in range(0, d, tn):
            cs = slice(n0, n0 + tn)
            gate = jax.nn.sigmoid(jnp.dot(u, wg_ref[:, cs], preferred_element_type=F32))
            proj = jnp.dot(pb, wp_ref[:, cs], preferred_element_type=F32)
            o_ref[rs, cs] = h_ref[rs, cs] + gate * proj
        if final_norm:
            o_ref[rs, :] = _rms(o_ref[rs, :], nf_ref[...])


def _ple(h, nw, wg, p, wp, nf, layer, *, final_norm, tm, tn):
    t, d = h.shape
    pd = p.shape[2]
    return pl.pallas_call(
        functools.partial(_ple_kernel, final_norm=final_norm, tn=tn),
        grid=(t // tm,),
        in_specs=[
            pl.BlockSpec((tm, d), lambda i: (i, 0)),
            pl.BlockSpec((1, d), lambda i: (0, 0)),
            pl.BlockSpec((None, d, d), lambda i: (layer, 0, 0), pipeline_mode=pl.Buffered(1)),
            pl.BlockSpec((None, tm, pd), lambda i: (layer, i, 0)),
            pl.BlockSpec((None, pd, d), lambda i: (layer, 0, 0), pipeline_mode=pl.Buffered(1)),
            pl.BlockSpec((1, d), lambda i: (0, 0)),
        ],
        out_specs=pl.BlockSpec((tm, d), lambda i: (i, 0)),
        out_shape=jax.ShapeDtypeStruct((t, d), F32),
        compiler_params=_cparams(("parallel",)),
        name="ple",
    )(h, nw, wg, p, wp, nf)


MATMUL_LEVELS = (3, 4)


def _gla_arg_matrix():
    c = GLA_CHUNK
    t = np.arange(c)[:, None]
    r = np.arange(c)[None, :]
    blocks = [(r <= t)]
    for lvl in MATMUL_LEVELS:
        n = c >> (lvl + 1)
        ref = (t // (2 * n)) * (2 * n) + n - 1
        lower = t > ref
        blocks.append(np.where(lower, (r > ref) & (r <= t), (r > t) & (r <= ref)))
    m = np.concatenate(blocks, axis=0).astype(np.float32)
    return np.concatenate([m, m, m, np.zeros_like(m)], axis=1)


def _coarse_level_args(b, n):
    out = []
    for blk in range(GLA_CHUNK // (2 * n)):
        r0 = blk * 2 * n
        ref = b[r0 + n - 1:r0 + n]
        out.append(ref - b[r0:r0 + n])
        out.append(b[r0 + n:r0 + 2 * n] - ref)
    return jnp.concatenate(out, axis=0)


def _gla_masks(pack):
    c = GLA_CHUNK
    t = np.arange(c)[:, None]
    s = np.arange(c)[None, :]
    masks = np.zeros((1 + N_LEVELS, c, c), np.float32)
    masks[0] = (t == s)
    for lvl in range(N_LEVELS):
        n = c >> (lvl + 1)
        same = (t // (2 * n)) == (s // (2 * n))
        masks[1 + lvl] = same & ((t % (2 * n)) >= n) & ((s % (2 * n)) < n)
    return np.tile(masks, (1, 1, pack))


def _block_diag_rows(x, pack, width):
    if pack == 1:
        return x
    head = lax.broadcasted_iota(jnp.int32, x.shape, 1) // width
    return jnp.concatenate([jnp.where(head == p, x, jnp.zeros_like(x)) for p in range(pack)], axis=0)


_NT = (((1,), (1,)), ((), ()))
_TN = (((0,), (0,)), ((), ()))


def _gla_core_kernel(*refs, mode, layer, scale, packs, pack, dk, dv):
    wk = pack * dk
    wv = pack * dv
    if mode == "gla":
        (q_ref, k_ref, v_ref, og_ref, glr_ref, wgk_ref, bgk_ref, gn_ref, mall_ref, masks_ref,
         o_ref, st_ref) = refs
    else:
        (q_ref, k_ref, v_ref, og_ref, lbl_ref, gn_ref, mall_ref, masks_ref, o_ref, st_ref) = refs

    @pl.when(pl.program_id(2) == 0)
    def _():
        st_ref[...] = jnp.zeros_like(st_ref)

    if mode == "hgrn":
        lg = lbl_ref[...]
        e = jnp.exp(lg - jnp.max(lg, axis=0, keepdims=True))
        sm = e / jnp.sum(e, axis=0, keepdims=True)
        lb_all = jnp.zeros_like(sm[0:1])
        for r in range(1, layer + 1):
            lb_all = lb_all + sm[r:r + 1]

    c = GLA_CHUNK
    c2 = 2 * c
    n_chunks = q_ref.shape[0] // c2
    odd_row = lax.broadcasted_iota(jnp.int32, (c, wk), 0) % 2 == 1

    def nt(a, b):
        return lax.dot_general(a, b, _NT, preferred_element_type=F32)

    def half_terms(qh, kh, gh):
        g_hi, g_mid, g_lo = _split3(gh)
        gs = jnp.concatenate([g_hi, g_mid, g_lo, jnp.zeros_like(g_hi)], axis=0)
        pre = jnp.dot(mall_ref[...], gs, preferred_element_type=F32)
        b = pre[0:c]
        lvl_args = [_coarse_level_args(b, c >> (lvl + 1)) for lvl in range(MATMUL_LEVELS[0])]
        lvl_args += [pre[c:2 * c], pre[2 * c:3 * c], jnp.where(odd_row, gh, 0.0)]
        att = masks_ref[0] * nt(qh.astype(BF16), _block_diag_rows(kh.astype(BF16), pack, dk))
        for lvl in range(N_LEVELS):
            n = c >> (lvl + 1)
            el = jnp.exp2(lvl_args[lvl])
            if n % 8 == 0:
                mixed = jnp.concatenate(
                    [x[r0:r0 + n] for blk in range(c // (2 * n))
                     for x, r0 in ((kh, blk * 2 * n), (qh, blk * 2 * n + n))], axis=0)
                ql = (mixed * el).astype(BF16)
                kl = ql
            else:
                ql = (qh * el).astype(BF16)
                kl = (kh * el).astype(BF16)
            att = att + masks_ref[1 + lvl] * nt(ql, _block_diag_rows(kl, pack, dk))
        return b, att

    def chunk(ci, carry):
        rows = pl.ds(pl.multiple_of(ci * c2, c2), c2)
        if mode == "gla":
            z_all = jnp.dot(glr_ref[rows, :].astype(BF16), wgk_ref[...],
                            preferred_element_type=F32) + bgk_ref[...]
        for g in range(packs):
            ks = slice(g * wk, (g + 1) * wk)
            vs = slice(g * wv, (g + 1) * wv)
            vc = v_ref[rows, vs]
            og = og_ref[rows, vs]
            if mode == "gla":
                qc = q_ref[rows, ks] * scale
                kc = k_ref[rows, ks]
                z = z_all[:, ks]
                gc = (jnp.minimum(z, 0.0) - jnp.log1p(jnp.exp(-jnp.abs(z)))) * (LOG2E / GLA_GATE_NORM)
                gate = og * jax.nn.sigmoid(og)
            else:
                qq = q_ref[rows, ks]
                qc = qq * jax.nn.sigmoid(qq) * scale
                lb = lb_all[:, ks]
                f = lb + (1.0 - lb) * jax.nn.sigmoid(k_ref[rows, ks])
                kc = 1.0 - f
                gc = jnp.log(f) * LOG2E
                gate = jax.nn.sigmoid(og)

            q0, q1 = qc[0:c], qc[c:c2]
            k0, k1 = kc[0:c], kc[c:c2]
            b0, att0 = half_terms(q0, k0, gc[0:c])
            b1, att1 = half_terms(q1, k1, gc[c:c2])
            e_b0 = jnp.exp2(b0)
            e_b1 = jnp.exp2(b1)
            e_rev0 = jnp.exp2(b0[c - 1:c] - b0)
            e_rev1 = jnp.exp2(b1[c - 1:c] - b1)
            e_l0 = e_b0[c - 1:c]
            e_l1 = e_b1[c - 1:c]

            k0_end = (k0 * e_rev0).astype(BF16)
            cross = nt((q1 * e_b1).astype(BF16), _block_diag_rows(k0_end, pack, dk))

            st = st_ref[g]
            q_in = jnp.concatenate([q0 * e_b0, q1 * (e_b1 * e_l0)], axis=0).astype(BF16)
            o = nt(q_in, st.astype(BF16))
            vb = vc.astype(BF16)
            v0 = _block_diag_rows(vb[0:c], pack, dv)
            v1 = _block_diag_rows(vb[c:c2], pack, dv)
            o0 = o[0:c] + jnp.dot(att0.astype(BF16), v0, preferred_element_type=F32)
            o1 = (o[c:c2] + jnp.dot(cross.astype(BF16), v0, preferred_element_type=F32)
                  + jnp.dot(att1.astype(BF16), v1, preferred_element_type=F32))
            o = jnp.concatenate([o0, o1], axis=0)

            k_dec = jnp.concatenate([k0 * (e_rev0 * e_l1), k1 * e_rev1], axis=0).astype(BF16)
            e_step = e_l0 * e_l1
            for p in range(pack):
                rs = slice(p * dv, (p + 1) * dv)
                cs = slice(p * dk, (p + 1) * dk)
                upd = lax.dot_general(vb[:, rs], k_dec[:, cs], _TN, preferred_element_type=F32)
                st_ref[g, rs, cs] = st[rs, cs] * e_step[:, cs] + upd

            for p in range(pack):
                hs = slice(p * dv, (p + 1) * dv)
                o_ref[rows, g * wv + p * dv:g * wv + (p + 1) * dv] = (
                    _rms(o[:, hs], gn_ref[...]) * gate[:, hs]).astype(o_ref.dtype)
        return carry

    lax.fori_loop(0, n_chunks, chunk, 0, unroll=2)


def _gla_core(proj, *, mode, layer, batch, seq, heads, hps, pack, dk, dv, col_q, col_k, col_v,
              col_og, extra, gn, tl=GLA_TOKENS_PER_STEP):
    nl = seq // tl
    t = batch * seq
    wk = hps * dk
    wv = hps * dv
    mall = jnp.asarray(_gla_arg_matrix(), BF16)
    masks = jnp.asarray(_gla_masks(pack), F32)

    def rowblk(b, h, l):
        return b * nl + l

    in_specs = [
        pl.BlockSpec((tl, wk), lambda b, h, l: (rowblk(b, h, l), col_q // wk + h)),
        pl.BlockSpec((tl, wk), lambda b, h, l: (rowblk(b, h, l), col_k // wk + h)),
        pl.BlockSpec((tl, wv), lambda b, h, l: (rowblk(b, h, l), col_v // wv + h)),
        pl.BlockSpec((tl, wv), lambda b, h, l: (rowblk(b, h, l), col_og // wv + h)),
    ]
    args = [proj, proj, proj, proj]
    if mode == "gla":
        col_glr, wgk, bgk = extra
        in_specs += [
            pl.BlockSpec((tl, LANES), lambda b, h, l: (rowblk(b, h, l), col_glr // LANES)),
            pl.BlockSpec((LANES, wk), lambda b, h, l: (0, h)),
            pl.BlockSpec((1, wk), lambda b, h, l: (0, h)),
        ]
        args += [proj, wgk, bgk]
    else:
        (lbl,) = extra
        in_specs += [pl.BlockSpec((DEPTH, wk), lambda b, h, l: (0, h))]
        args += [lbl]
    in_specs += [
        pl.BlockSpec((1, dv), lambda b, h, l: (0, 0)),
        pl.BlockSpec(mall.shape, lambda b, h, l: (0, 0)),
        pl.BlockSpec(masks.shape, lambda b, h, l: (0, 0, 0)),
    ]
    args += [gn, mall, masks]
    return pl.pallas_call(
        functools.partial(_gla_core_kernel, mode=mode, layer=layer, scale=dk ** -0.5,
                          packs=hps // pack, pack=pack, dk=dk, dv=dv),
        grid=(batch, heads // hps, nl),
        in_specs=in_specs,
        out_specs=pl.BlockSpec((tl, wv), lambda b, h, l: (rowblk(b, h, l), h)),
        out_shape=jax.ShapeDtypeStruct((t, heads * dv), BF16),
        scratch_shapes=[pltpu.VMEM((hps // pack, pack * dv, pack * dk), F32)],
        compiler_params=_cparams(("parallel", "parallel", "arbitrary")),
        name=mode + "_core",
    )(*args)


N_PAIRS = SSM_HEADS // 2
PAIRS_PER_GROUP = N_PAIRS // SSM_GROUPS


def _ssd_kernel(z_ref, x_ref, bm_ref, cm_ref, dt_ref, cw_ref, cb_ref, dtb_ref, alog_ref, dsk_ref,
                nw_ref, tri3_ref, unperm_ref, o_ref,
                halo_ref, xs_ref, bs_ref, cs_ref, st_ref, y_ref, bt_ref, dtt_ref, wdt_ref, elb_ref):
    c = SSD_CHUNK
    nt_rows = c // 8
    hr = CONV_HALO_ROWS
    l = pl.program_id(1)

    @pl.when(l == 0)
    def _():
        st_ref[...] = jnp.zeros_like(st_ref)
        halo_ref[...] = jnp.zeros_like(halo_ref)

    first_sublane = lax.broadcasted_iota(jnp.int32, (8, LANES), 0) == 0
    for slab in range(SSM_CONV_DIM // LANES):
        cs = slice(slab * LANES, (slab + 1) * LANES)
        if slab < SSM_DINNER // LANES:
            cur = x_ref[:, cs]
        elif slab < (SSM_DINNER + SSM_GROUPS * SSM_STATE) // LANES:
            cur = bm_ref[:, slab * LANES - SSM_DINNER:(slab + 1) * LANES - SSM_DINNER]
        else:
            off = SSM_DINNER + SSM_GROUPS * SSM_STATE
            cur = cm_ref[:, slab * LANES - off:(slab + 1) * LANES - off]
        prev_tail = halo_ref[:, cs]
        wrapped = []
        for s in range(SSM_CONV - 1):
            tail = cur[c - hr + 8 * s:c - hr + 8 * (s + 1)]
            moved = pltpu.roll(tail, 1, 0)
            wrapped.append(jnp.where(first_sublane, prev_tail[8 * s + 7:8 * s + 8], moved))
        halo_ref[:, cs] = cur[c - hr:c]
        acc = cb_ref[:, cs] + cw_ref[SSM_CONV - 1:SSM_CONV, cs] * cur
        for s in range(1, SSM_CONV):
            shifted = jnp.concatenate(wrapped[SSM_CONV - 1 - s:] + [cur[0:c - 8 * s]], axis=0)
            acc = acc + cw_ref[SSM_CONV - 1 - s:SSM_CONV - s, cs] * shifted
        act = acc * jax.nn.sigmoid(acc)
        if slab < N_PAIRS:
            xs_ref[slab] = act
        elif slab < N_PAIRS + SSM_GROUPS:
            bs_ref[slab - N_PAIRS] = act
        else:
            cs_ref[slab - N_PAIRS - SSM_GROUPS] = act

    lane = lax.broadcasted_iota(jnp.int32, (1, LANES), 1)
    dtr = dt_ref[...] + dtb_ref[...]
    dt = jnp.maximum(dtr, 0.0) + jnp.log1p(jnp.exp(-jnp.abs(dtr)))
    a = jnp.where(lane < SSM_HEADS, -jnp.exp(alog_ref[...]), 0.0)
    la_hi, la_mid, la_lo = _split3(dt * (a * LOG2E))
    b = jnp.dot(tri3_ref[...], jnp.concatenate([la_hi, la_mid, la_lo], axis=0),
                preferred_element_type=F32)
    b_last = b[c - 1:c]
    wd = jnp.exp2(b_last - b) * dt
    bt = b.T
    bt_ref[...] = bt
    dtt_ref[...] = (b - jnp.log2(dt)).T
    wdt_ref[...] = wd.T
    elb_ref[...] = jnp.broadcast_to(jnp.exp2(bt[:, c - 1:c]), (LANES, LANES))

    row = lax.broadcasted_iota(jnp.int32, (c, c), 0)
    col = lax.broadcasted_iota(jnp.int32, (c, c), 1)
    causal = (col % 8) * nt_rows + col // 8 <= (row % 8) * nt_rows + row // 8
    lo = lax.broadcasted_iota(jnp.int32, (c, LANES), 1) < SSM_HEADDIM

    def group(g, carry):
        bg = bs_ref[g]
        cg = cs_ref[g]
        cgb = cg.astype(BF16)
        cbm = lax.dot_general(cgb, bg.astype(BF16), _NT, preferred_element_type=F32)
        bgt = bg.T
        for j in range(PAIRS_PER_GROUP):
            p = g * PAIRS_PER_GROUP + j
            xp = xs_ref[p]
            sp = st_ref[p]
            x_lo = jnp.where(lo, xp, 0.0).astype(BF16)
            x_hi = jnp.where(lo, 0.0, xp).astype(BF16)
            x_bd = jnp.concatenate([x_lo, x_hi], axis=0)
            lhs_y = []
            lhs_s = []
            bhs = []
            for k in range(2):
                h = 2 * p + k
                bh = jnp.broadcast_to(bt_ref[pl.ds(h, 1), :], (c, c)).T
                rel = bh - dtt_ref[pl.ds(h, 1), :]
                dec = jnp.exp2(jnp.where(causal, rel, MASKED_LOG2))
                lhs_y.append((cbm * dec).astype(BF16))
                lhs_s.append((bgt * wdt_ref[pl.ds(h, 1), :]).astype(BF16))
                bhs.append(bh)
            y = jnp.exp2(jnp.where(lo, bhs[0], bhs[1])) * jnp.dot(
                cgb, sp.astype(BF16), preferred_element_type=F32)
            y = y + jnp.dot(jnp.concatenate(lhs_y, axis=1), x_bd, preferred_element_type=F32)
            su = jnp.dot(jnp.concatenate(lhs_s, axis=1), x_bd,
                         preferred_element_type=F32)
            el = jnp.where(lane < SSM_HEADDIM, elb_ref[pl.ds(2 * p, 1), :],
                           elb_ref[pl.ds(2 * p + 1, 1), :])
            st_ref[p] = sp * el + su
            y_ref[p] = y + dsk_ref[p] * xp
        return carry

    lax.fori_loop(0, SSM_GROUPS, group, 0, unroll=8)

    gw = PAIRS_PER_GROUP * LANES
    for g in range(SSM_GROUPS):
        cs = slice(g * gw, (g + 1) * gw)
        yg = jnp.concatenate([y_ref[g * PAIRS_PER_GROUP + j] for j in range(PAIRS_PER_GROUP)],
                             axis=1)
        zg = z_ref[:, cs]
        yg = yg * (zg * jax.nn.sigmoid(zg))
        og = _rms(yg, nw_ref[:, cs]).astype(BF16)
        o_ref[:, cs] = jnp.dot(unperm_ref[...], og, preferred_element_type=F32).astype(o_ref.dtype)


def _ssd_core(proj, conv_w, conv_b, dt_bias, a_log, d_skip, norm_w, *, batch, seq):
    c = SSD_CHUNK
    nl = seq // c
    t = batch * seq
    perm = _interleave_matrix()
    tok = perm.argmax(axis=1)
    tri = (tok[None, :] <= tok[:, None]).astype(np.float32)
    tri3 = jnp.asarray(np.concatenate([tri, tri, tri], axis=1), BF16)
    unperm = jnp.asarray(perm.T, BF16)
    pad = LANES - SSM_HEADS
    dtb = jnp.pad(dt_bias.astype(F32), (0, pad)).reshape(1, LANES)
    alog = jnp.pad(a_log.astype(F32), (0, pad)).reshape(1, LANES)
    dsk = jnp.repeat(d_skip.astype(F32), SSM_HEADDIM).reshape(N_PAIRS, 1, LANES)

    def rb(b, l):
        return b * nl + l

    full2 = lambda b, l: (0, 0)
    in_specs = [
        pl.BlockSpec((c, SSM_DINNER), lambda b, l: (rb(b, l), 0)),
        pl.BlockSpec((c, SSM_DINNER), lambda b, l: (rb(b, l), 1)),
        pl.BlockSpec((c, 1024), lambda b, l: (rb(b, l), 2 * SSM_DINNER // 1024)),
        pl.BlockSpec((c, 1024), lambda b, l: (rb(b, l), 2 * SSM_DINNER // 1024 + 1)),
        pl.BlockSpec((c, LANES), lambda b, l: (rb(b, l), (SSM_DINNER + SSM_CONV_DIM) // LANES)),
        pl.BlockSpec((SSM_CONV, SSM_CONV_DIM), full2),
        pl.BlockSpec((1, SSM_CONV_DIM), full2),
        pl.BlockSpec((1, LANES), full2),
        pl.BlockSpec((1, LANES), full2),
        pl.BlockSpec((N_PAIRS, 1, LANES), lambda b, l: (0, 0, 0)),
        pl.BlockSpec((1, SSM_DINNER), full2),
        pl.BlockSpec(tri3.shape, full2),
        pl.BlockSpec(unperm.shape, full2),
    ]
    scratch = [
        pltpu.VMEM((CONV_HALO_ROWS, SSM_CONV_DIM), F32),
        pltpu.VMEM((N_PAIRS, c, LANES), F32),
        pltpu.VMEM((SSM_GROUPS, c, SSM_STATE), F32),
        pltpu.VMEM((SSM_GROUPS, c, SSM_STATE), F32),
        pltpu.VMEM((N_PAIRS, SSM_STATE, LANES), F32),
        pltpu.VMEM((N_PAIRS, c, LANES), F32),
        pltpu.VMEM((LANES, c), F32),
        pltpu.VMEM((LANES, c), F32),
        pltpu.VMEM((LANES, c), F32),
        pltpu.VMEM((LANES, LANES), F32),
    ]
    return pl.pallas_call(
        _ssd_kernel,
        grid=(batch, nl),
        in_specs=in_specs,
        out_specs=pl.BlockSpec((c, SSM_DINNER), lambda b, l: (rb(b, l), 0)),
        out_shape=jax.ShapeDtypeStruct((t, SSM_DINNER), BF16),
        scratch_shapes=scratch,
        compiler_params=_cparams(("parallel", "arbitrary")),
        name="ssd_core",
    )(proj, proj, proj, proj, proj, conv_w.astype(F32), conv_b.astype(F32).reshape(1, -1),
      dtb, alog, dsk, norm_w.astype(F32).reshape(1, -1), tri3, unperm)


def _pad_last(w, n):
    return jnp.pad(w, [(0, 0)] * (w.ndim - 1) + [(0, n - w.shape[-1])])


def kernel(x, p, norm_mix, norm_mlp, norm_ple, norm_final, w_up, w_down, w_ple_proj, w_ple_gate,
           gla_w_in, gla_w_gk2, gla_b_gk, gla_gn, gla_w_out,
           hgrn_lb_logits, hgrn_w_in, hgrn_gn, hgrn_w_out,
           ssm_w_in, ssm_conv_w, ssm_conv_b, ssm_dt_bias, ssm_a_log, ssm_d, ssm_norm, ssm_w_out):
    batch, seq, d = x.shape
    t = batch * seq
    h = x.reshape(t, d)
    pf = p.reshape(DEPTH, t, PLE_DIM)
    row = lambda v: v.astype(F32).reshape(1, -1)

    gla_w_in_b = _pad_last(gla_w_in, GLA_IN_COLS).astype(BF16)
    hgrn_w_in_b = hgrn_w_in.astype(BF16)
    ssm_w_in_b = _pad_last(ssm_w_in, SSM_IN_COLS).astype(BF16)
    gla_w_out_b = gla_w_out.astype(BF16)
    hgrn_w_out_b = hgrn_w_out.astype(BF16)
    ssm_w_out_b = ssm_w_out.astype(BF16)
    w_up_b = w_up.astype(BF16)
    w_down_b = w_down.astype(BF16)
    w_gate_b = w_ple_gate.astype(BF16)
    w_proj_b = w_ple_proj.astype(BF16)

    for i in range(DEPTH):
        kind, j = i % N_MIXERS, i // N_MIXERS
        nw = row(norm_mix[i])
        if kind == 0:
            proj = _norm_matmul(h, nw, gla_w_in_b, j, tm=1024, tn=1280)
            kd = GLA_HEADS * GLA_DK
            vd = GLA_HEADS * GLA_DV
            wgk = jnp.pad(gla_w_gk2[j], ((0, LANES - GLA_GATE_RANK), (0, 0))).astype(BF16)
            o = _gla_core(proj, mode="gla", layer=i, batch=batch, seq=seq, heads=GLA_HEADS,
                          hps=GLA_HEADS, pack=1, dk=GLA_DK, dv=GLA_DV, col_q=0, col_k=kd, col_v=2 * kd,
                          col_og=2 * kd + vd,
                          extra=(2 * kd + 2 * vd, wgk, row(gla_b_gk[j])), gn=row(gla_gn[j]))
            h = _matmul_res(o, gla_w_out_b, j, h, tm=1024)
        elif kind == 1:
            proj = _norm_matmul(h, nw, hgrn_w_in_b, j, tm=1024, tn=2048)
            fd = HGRN_HEADS * HGRN_DK
            vd = HGRN_HEADS * HGRN_DV
            o = _gla_core(proj, mode="hgrn", layer=i, batch=batch, seq=seq, heads=HGRN_HEADS,
                          hps=HGRN_HEADS // 2, pack=2, dk=HGRN_DK, dv=HGRN_DV, col_q=0, col_k=fd,
                          col_v=2 * fd, col_og=2 * fd + vd,
                          extra=(hgrn_lb_logits.astype(F32),), gn=row(hgrn_gn[j]),
                          tl=2 * GLA_TOKENS_PER_STEP)
            h = _matmul_res(o, hgrn_w_out_b, j, h, tm=1024)
        else:
            proj = _norm_matmul(h, nw, ssm_w_in_b, j, tm=1024, tn=1792, interleave=True)
            o = _ssd_core(proj, ssm_conv_w[j], ssm_conv_b[j], ssm_dt_bias[j], ssm_a_log[j],
                          ssm_d[j], ssm_norm[j], batch=batch, seq=seq)
            h = _matmul_res(o, ssm_w_out_b, j, h, tm=512)
        h = _mlp(h, row(norm_mlp[i]), w_up_b, w_down_b, i, tm=1024, tf=1024)
        h = _ple(h, row(norm_ple[i]), w_gate_b, pf, w_proj_b, row(norm_final), i,
                 final_norm=(i == DEPTH - 1), tm=1024, tn=512)
    return h.reshape(batch, seq, d)
```

```python
import functools
import math

import numpy as np
import jax
import jax.numpy as jnp
from jax import lax
from jax.experimental import pallas as pl
from jax.experimental.pallas import tpu as pltpu

F32 = jnp.float32
BF16 = jnp.bfloat16

D_MODEL = 2048
DEPTH = 4
N_MIXERS = 3
PLE_DIM = 256
D_FF = 4 * D_MODEL
EPS = 1e-6

GLA_HEADS = 4
GLA_DK = 256
GLA_DV = 512
GLA_GATE_RANK = 16
GLA_GATE_NORM = 16.0
HGRN_HEADS = 16
HGRN_DK = 128
HGRN_DV = 128
SSM_DINNER = 4096
SSM_HEADDIM = 64
SSM_HEADS = 64
SSM_GROUPS = 8
SSM_STATE = 128
SSM_CONV = 4
SSM_CONV_DIM = SSM_DINNER + 2 * SSM_GROUPS * SSM_STATE

LANES = 128
VMEM_LIMIT_BYTES = 60 * 1024 * 1024

NORM_ROWS = 256
GLA_CHUNK = 64
GLA_TOKENS_PER_STEP = 512
SSD_CHUNK = 128
CONV_HALO_ROWS = 8 * (SSM_CONV - 1)

GLA_MAIN_COLS = 6144
SSM_IN_COLS = 10752
N_LEVELS = 6
LOG2E = math.log2(math.e)
MASKED_LOG2 = -1e30


def _cparams(sem):
    return pltpu.CompilerParams(dimension_semantics=sem, vmem_limit_bytes=VMEM_LIMIT_BYTES)


def _rms(x, w):
    ms = jnp.mean(x * x, axis=-1, keepdims=True)
    return x * lax.rsqrt(ms + EPS) * w


def _split3(x):
    hi = x.astype(BF16)
    r1 = x - hi.astype(F32)
    mid = r1.astype(BF16)
    lo = (r1 - mid.astype(F32)).astype(BF16)
    return hi, mid, lo


def _interleave_matrix():
    r = np.arange(SSD_CHUNK)
    p = np.zeros((SSD_CHUNK, SSD_CHUNK), np.float32)
    p[r, (r % 8) * (SSD_CHUNK // 8) + r // 8] = 1.0
    return p


def _norm_matmul_kernel(*refs, interleave, narrow):
    pm_ref = wx_ref = ox_ref = None
    if interleave:
        h_ref, nw_ref, w_ref, pm_ref, o_ref, u_ref = refs
    elif narrow:
        h_ref, nw_ref, w_ref, wx_ref, o_ref, ox_ref, u_ref = refs
    else:
        h_ref, nw_ref, w_ref, o_ref, u_ref = refs
    j = pl.program_id(1)

    @pl.when(j == 0)
    def _():
        for r0 in range(0, h_ref.shape[0], NORM_ROWS):
            rs = slice(r0, r0 + NORM_ROWS)
            u = _rms(h_ref[rs, :], nw_ref[...]).astype(BF16)
            if interleave:
                u = jnp.concatenate(
                    [jnp.dot(pm_ref[...], u[g0:g0 + SSD_CHUNK], preferred_element_type=F32)
                     for g0 in range(0, NORM_ROWS, SSD_CHUNK)], axis=0).astype(BF16)
            u_ref[rs, :] = u
            o_ref[rs, :] = jnp.dot(u, w_ref[...], preferred_element_type=F32).astype(o_ref.dtype)
            if narrow:
                ox_ref[rs, :] = jnp.dot(u, wx_ref[...], preferred_element_type=F32)

    @pl.when(j > 0)
    def _():
        o_ref[...] = jnp.dot(u_ref[...], w_ref[...], preferred_element_type=F32).astype(o_ref.dtype)


def _norm_matmul(h, nw, w, layer, *, tm, tn, interleave=False, w_narrow=None):
    t, d = h.shape
    n = w.shape[2]
    in_specs = [
        pl.BlockSpec((tm, d), lambda i, j: (i, 0)),
        pl.BlockSpec((1, d), lambda i, j: (0, 0)),
        pl.BlockSpec((None, d, tn), lambda i, j: (layer, 0, j)),
    ]
    args = [h, nw, w]
    out_specs = pl.BlockSpec((tm, tn), lambda i, j: (i, j))
    out_shape = jax.ShapeDtypeStruct((t, n), F32)
    if interleave:
        in_specs.append(pl.BlockSpec((SSD_CHUNK, SSD_CHUNK), lambda i, j: (0, 0)))
        args.append(jnp.asarray(_interleave_matrix(), BF16))
    if w_narrow is not None:
        assert not interleave
        in_specs.append(pl.BlockSpec((None, d, LANES), lambda i, j: (layer, 0, 0)))
        args.append(w_narrow)
        out_specs = [out_specs, pl.BlockSpec((tm, LANES), lambda i, j: (i, 0))]
        out_shape = [out_shape, jax.ShapeDtypeStruct((t, LANES), F32)]
    return pl.pallas_call(
        functools.partial(_norm_matmul_kernel, interleave=interleave, narrow=w_narrow is not None),
        grid=(t // tm, n // tn),
        in_specs=in_specs,
        out_specs=out_specs,
        out_shape=out_shape,
        scratch_shapes=[pltpu.VMEM((tm, d), BF16)],
        compiler_params=_cparams(("parallel", "arbitrary")),
        name="norm_matmul",
    )(*args)


def _matmul_res_kernel(a_ref, w_ref, h_ref, o_ref):
    o_ref[...] = h_ref[...] + jnp.dot(a_ref[...], w_ref[...], preferred_element_type=F32)


def _matmul_res(a, w, layer, h, *, tm):
    t, k = a.shape
    n = w.shape[2]
    return pl.pallas_call(
        _matmul_res_kernel,
        grid=(t // tm,),
        in_specs=[
            pl.BlockSpec((tm, k), lambda i: (i, 0)),
            pl.BlockSpec((None, k, n), lambda i: (layer, 0, 0), pipeline_mode=pl.Buffered(1)),
            pl.BlockSpec((tm, n), lambda i: (i, 0)),
        ],
        out_specs=pl.BlockSpec((tm, n), lambda i: (i, 0)),
        out_shape=jax.ShapeDtypeStruct((t, n), F32),
        compiler_params=_cparams(("parallel",)),
        name="matmul_res",
    )(a, w, h)


def _mlp_kernel(h_ref, nw_ref, wu_ref, wd_ref, o_ref, u_ref):
    f = pl.program_id(1)

    def ff(u):
        a = jnp.dot(u, wu_ref[...], preferred_element_type=F32)
        a = jnp.square(jnp.maximum(a, 0.0)).astype(BF16)
        return jnp.dot(a, wd_ref[...], preferred_element_type=F32)

    @pl.when(f == 0)
    def _():
        for r0 in range(0, h_ref.shape[0], NORM_ROWS):
            rs = slice(r0, r0 + NORM_ROWS)
            x = h_ref[rs, :]
            u = _rms(x, nw_ref[...]).astype(BF16)
            u_ref[rs, :] = u
            o_ref[rs, :] = x + ff(u)

    @pl.when(f > 0)
    def _():
        o_ref[...] += ff(u_ref[...])


def _mlp(h, nw, wu, wd, layer, *, tm, tf):
    t, d = h.shape
    ff = wu.shape[2]
    return pl.pallas_call(
        _mlp_kernel,
        grid=(t // tm, ff // tf),
        in_specs=[
            pl.BlockSpec((tm, d), lambda i, f: (i, 0)),
            pl.BlockSpec((1, d), lambda i, f: (0, 0)),
            pl.BlockSpec((None, d, tf), lambda i, f: (layer, 0, f)),
            pl.BlockSpec((None, tf, d), lambda i, f: (layer, f, 0)),
        ],
        out_specs=pl.BlockSpec((tm, d), lambda i, f: (i, 0)),
        out_shape=jax.ShapeDtypeStruct((t, d), F32),
        scratch_shapes=[pltpu.VMEM((tm, d), BF16)],
        compiler_params=_cparams(("parallel", "arbitrary")),
        name="mlp",
    )(h, nw, wu, wd)


def _ple_kernel(h_ref, nw_ref, wg_ref, p_ref, wp_ref, nf_ref, o_ref, *, final_norm, tn):
    d = h_ref.shape[1]
    for r0 in range(0, h_ref.shape[0], NORM_ROWS):
        rs = slice(r0, r0 + NORM_ROWS)
        u = _rms(h_ref[rs, :], nw_ref[...]).astype(BF16)
        pb = p_ref[rs, :].astype(BF16)
        for n0 in range(0, d, tn):
            cs = slice(n0, n0 + tn)
            gate = jax.nn.sigmoid(jnp.dot(u, wg_ref[:, cs], preferred_element_type=F32))
            proj = jnp.dot(pb, wp_ref[:, cs], preferred_element_type=F32)
            o_ref[rs, cs] = h_ref[rs, cs] + gate * proj
        if final_norm:
            o_ref[rs, :] = _rms(o_ref[rs, :], nf_ref[...])


def _ple(h, nw, wg, p, wp, nf, layer, *, final_norm, tm, tn):
    t, d = h.shape
    pd = p.shape[2]
    return pl.pallas_call(
        functools.partial(_ple_kernel, final_norm=final_norm, tn=tn),
        grid=(t // tm,),
        in_specs=[
            pl.BlockSpec((tm, d), lambda i: (i, 0)),
            pl.BlockSpec((1, d), lambda i: (0, 0)),
            pl.BlockSpec((None, d, d), lambda i: (layer, 0, 0), pipeline_mode=pl.Buffered(1)),
            pl.BlockSpec((None, tm, pd), lambda i: (layer, i, 0)),
            pl.BlockSpec((None, pd, d), lambda i: (layer, 0, 0), pipeline_mode=pl.Buffered(1)),
            pl.BlockSpec((1, d), lambda i: (0, 0)),
        ],
        out_specs=pl.BlockSpec((tm, d), lambda i: (i, 0)),
        out_shape=jax.ShapeDtypeStruct((t, d), F32),
        compiler_params=_cparams(("parallel",)),
        name="ple",
    )(h, nw, wg, p, wp, nf)


MATMUL_LEVELS = (3, 4)


def _gla_arg_matrix():
    c = GLA_CHUNK
    t = np.arange(c)[:, None]
    r = np.arange(c)[None, :]
    blocks = [(r <= t)]
    for lvl in MATMUL_LEVELS:
        n = c >> (lvl + 1)
        ref = (t // (2 * n)) * (2 * n) + n - 1
        lower = t > ref
        blocks.append(np.where(lower, (r > ref) & (r <= t), (r > t) & (r <= ref)))
    m = np.concatenate(blocks, axis=0).astype(np.float32)
    return np.concatenate([m, m, m, np.zeros_like(m)], axis=1)


def _coarse_level_args(b, n):
    out = []
    for blk in range(GLA_CHUNK // (2 * n)):
        r0 = blk * 2 * n
        ref = b[r0 + n - 1:r0 + n]
        out.append(ref - b[r0:r0 + n])
        out.append(b[r0 + n:r0 + 2 * n] - ref)
    return jnp.concatenate(out, axis=0)


def _gla_masks(pack):
    c = GLA_CHUNK
    t = np.arange(c)[:, None]
    s = np.arange(c)[None, :]
    masks = np.zeros((1 + N_LEVELS, c, c), np.float32)
    masks[0] = (t == s)
    for lvl in range(N_LEVELS):
        n = c >> (lvl + 1)
        same = (t // (2 * n)) == (s // (2 * n))
        masks[1 + lvl] = same & ((t % (2 * n)) >= n) & ((s % (2 * n)) < n)
    return np.tile(masks, (1, 1, pack))


def _block_diag_rows(x, pack, width):
    if pack == 1:
        return x
    head = lax.broadcasted_iota(jnp.int32, x.shape, 1) // width
    return jnp.concatenate([jnp.where(head == p, x, jnp.zeros_like(x)) for p in range(pack)], axis=0)


_NT = (((1,), (1,)), ((), ()))
_TN = (((0,), (0,)), ((), ()))


def _gla_core_kernel(*refs, mode, layer, scale, packs, pack, dk, dv):
    wk = pack * dk
    wv = pack * dv
    if mode == "gla":
        (q_ref, k_ref, v_ref, og_ref, glr_ref, wgk_ref, bgk_ref, gn_ref, mall_ref, masks_ref,
         o_ref, st_ref) = refs
    else:
        (q_ref, k_ref, v_ref, og_ref, lbl_ref, gn_ref, mall_ref, masks_ref, o_ref, st_ref) = refs

    @pl.when(pl.program_id(2) == 0)
    def _():
        st_ref[...] = jnp.zeros_like(st_ref)

    if mode == "hgrn":
        lg = lbl_ref[...]
        e = jnp.exp(lg - jnp.max(lg, axis=0, keepdims=True))
        sm = e / jnp.sum(e, axis=0, keepdims=True)
        lb_all = jnp.zeros_like(sm[0:1])
        for r in range(1, layer + 1):
            lb_all = lb_all + sm[r:r + 1]

    c = GLA_CHUNK
    c2 = 2 * c
    n_chunks = q_ref.shape[0] // c2
    odd_row = lax.broadcasted_iota(jnp.int32, (c, wk), 0) % 2 == 1

    def nt(a, b):
        return lax.dot_general(a, b, _NT, preferred_element_type=F32)

    def half_terms(qh, kh, gh):
        g_hi, g_mid, g_lo = _split3(gh)
        gs = jnp.concatenate([g_hi, g_mid, g_lo, jnp.zeros_like(g_hi)], axis=0)
        pre = jnp.dot(mall_ref[...], gs, preferred_element_type=F32)
        b = pre[0:c]
        lvl_args = [_coarse_level_args(b, c >> (lvl + 1)) for lvl in range(MATMUL_LEVELS[0])]
        lvl_args += [pre[c:2 * c], pre[2 * c:3 * c], jnp.where(odd_row, gh, 0.0)]
        att = masks_ref[0] * nt(qh.astype(BF16), _block_diag_rows(kh.astype(BF16), pack, dk))
        for lvl in range(N_LEVELS):
            n = c >> (lvl + 1)
            el = jnp.exp2(lvl_args[lvl])
            if n % 8 == 0:
                mixed = jnp.concatenate(
                    [x[r0:r0 + n] for blk in range(c // (2 * n))
                     for x, r0 in ((kh, blk * 2 * n), (qh, blk * 2 * n + n))], axis=0)
                ql = (mixed * el).astype(BF16)
                kl = ql
            else:
                ql = (qh * el).astype(BF16)
                kl = (kh * el).astype(BF16)
            att = att + masks_ref[1 + lvl] * nt(ql, _block_diag_rows(kl, pack, dk))
        return b, att

    def chunk(ci, carry):
        rows = pl.ds(pl.multiple_of(ci * c2, c2), c2)
        if mode == "gla":
            z_all = jnp.dot(glr_ref[rows, :].astype(BF16), wgk_ref[...],
                            preferred_element_type=F32) + bgk_ref[...]
        for g in range(packs):
            ks = slice(g * wk, (g + 1) * wk)
            vs = slice(g * wv, (g + 1) * wv)
            vc = v_ref[rows, vs]
            og = og_ref[rows, vs]
            if mode == "gla":
                qc = q_ref[rows, ks] * scale
                kc = k_ref[rows, ks]
                z = z_all[:, ks]
                gc = (jnp.minimum(z, 0.0) - jnp.log1p(jnp.exp(-jnp.abs(z)))) * (LOG2E / GLA_GATE_NORM)
                gate = og * jax.nn.sigmoid(og)
            else:
                qq = q_ref[rows, ks]
                qc = qq * jax.nn.sigmoid(qq) * scale
                lb = lb_all[:, ks]
                f = lb + (1.0 - lb) * jax.nn.sigmoid(k_ref[rows, ks])
                kc = 1.0 - f
                gc = jnp.log(f) * LOG2E
                gate = jax.nn.sigmoid(og)

            q0, q1 = qc[0:c], qc[c:c2]
            k0, k1 = kc[0:c], kc[c:c2]
            b0, att0 = half_terms(q0, k0, gc[0:c])
            b1, att1 = half_terms(q1, k1, gc[c:c2])
            e_b0 = jnp.exp2(b0)
            e_b1 = jnp.exp2(b1)
            e_rev0 = jnp.exp2(b0[c - 1:c] - b0)
            e_rev1 = jnp.exp2(b1[c - 1:c] - b1)
            e_l0 = e_b0[c - 1:c]
            e_l1 = e_b1[c - 1:c]

            k0_end = (k0 * e_rev0).astype(BF16)
            cross = nt((q1 * e_b1).astype(BF16), _block_diag_rows(k0_end, pack, dk))

            st = st_ref[g]
            q_in = jnp.concatenate([q0 * e_b0, q1 * (e_b1 * e_l0)], axis=0).astype(BF16)
            o = nt(q_in, st.astype(BF16))
            vb = vc.astype(BF16)
            v0 = _block_diag_rows(vb[0:c], pack, dv)
            v1 = _block_diag_rows(vb[c:c2], pack, dv)
            o0 = o[0:c] + jnp.dot(att0.astype(BF16), v0, preferred_element_type=F32)
            o1 = (o[c:c2] + jnp.dot(cross.astype(BF16), v0, preferred_element_type=F32)
                  + jnp.dot(att1.astype(BF16), v1, preferred_element_type=F32))
            o = jnp.concatenate([o0, o1], axis=0)

            k_dec = jnp.concatenate([k0 * (e_rev0 * e_l1), k1 * e_rev1], axis=0).astype(BF16)
            e_step = e_l0 * e_l1
            for p in range(pack):
                rs = slice(p * dv, (p + 1) * dv)
                cs = slice(p * dk, (p + 1) * dk)
                upd = lax.dot_general(vb[:, rs], k_dec[:, cs], _TN, preferred_element_type=F32)
                st_ref[g, rs, cs] = st[rs, cs] * e_step[:, cs] + upd

            for p in range(pack):
                hs = slice(p * dv, (p + 1) * dv)
                o_ref[rows, g * wv + p * dv:g * wv + (p + 1) * dv] = (
                    _rms(o[:, hs], gn_ref[...]) * gate[:, hs]).astype(o_ref.dtype)
        return carry

    lax.fori_loop(0, n_chunks, chunk, 0, unroll=2)


def _gla_core(proj, *, mode, layer, batch, seq, heads, hps, pack, dk, dv, col_q, col_k, col_v,
              col_og, extra, gn, tl=GLA_TOKENS_PER_STEP):
    nl = seq // tl
    t = batch * seq
    wk = hps * dk
    wv = hps * dv
    mall = jnp.asarray(_gla_arg_matrix(), BF16)
    masks = jnp.asarray(_gla_masks(pack), F32)

    def rowblk(b, h, l):
        return b * nl + l

    in_specs = [
        pl.BlockSpec((tl, wk), lambda b, h, l: (rowblk(b, h, l), col_q // wk + h)),
        pl.BlockSpec((tl, wk), lambda b, h, l: (rowblk(b, h, l), col_k // wk + h)),
        pl.BlockSpec((tl, wv), lambda b, h, l: (rowblk(b, h, l), col_v // wv + h)),
        pl.BlockSpec((tl, wv), lambda b, h, l: (rowblk(b, h, l), col_og // wv + h)),
    ]
    args = [proj, proj, proj, proj]
    if mode == "gla":
        glr, wgk, bgk = extra
        in_specs += [
            pl.BlockSpec((tl, LANES), lambda b, h, l: (rowblk(b, h, l), 0)),
            pl.BlockSpec((LANES, wk), lambda b, h, l: (0, h)),
            pl.BlockSpec((1, wk), lambda b, h, l: (0, h)),
        ]
        args += [glr, wgk, bgk]
    else:
        (lbl,) = extra
        in_specs += [pl.BlockSpec((DEPTH, wk), lambda b, h, l: (0, h))]
        args += [lbl]
    in_specs += [
        pl.BlockSpec((1, dv), lambda b, h, l: (0, 0)),
        pl.BlockSpec(mall.shape, lambda b, h, l: (0, 0)),
        pl.BlockSpec(masks.shape, lambda b, h, l: (0, 0, 0)),
    ]
    args += [gn, mall, masks]
    return pl.pallas_call(
        functools.partial(_gla_core_kernel, mode=mode, layer=layer, scale=dk ** -0.5,
                          packs=hps // pack, pack=pack, dk=dk, dv=dv),
        grid=(batch, heads // hps, nl),
        in_specs=in_specs,
        out_specs=pl.BlockSpec((tl, wv), lambda b, h, l: (rowblk(b, h, l), h)),
        out_shape=jax.ShapeDtypeStruct((t, heads * dv), BF16),
        scratch_shapes=[pltpu.VMEM((hps // pack, pack * dv, pack * dk), F32)],
        compiler_params=_cparams(("parallel", "parallel", "arbitrary")),
        name=mode + "_core",
    )(*args)


N_PAIRS = SSM_HEADS // 2
PAIRS_PER_GROUP = N_PAIRS // SSM_GROUPS


def _ssd_kernel(z_ref, x_ref, bm_ref, cm_ref, dt_ref, cw_ref, cb_ref, dtb_ref, alog_ref, dsk_ref,
                nw_ref, tri3_ref, unperm_ref, o_ref,
                halo_ref, xs_ref, bs_ref, cs_ref, st_ref, y_ref, bt_ref, dtt_ref, wdt_ref, elb_ref):
    c = SSD_CHUNK
    nt_rows = c // 8
    hr = CONV_HALO_ROWS
    l = pl.program_id(1)

    @pl.when(l == 0)
    def _():
        st_ref[...] = jnp.zeros_like(st_ref)
        halo_ref[...] = jnp.zeros_like(halo_ref)

    first_sublane = lax.broadcasted_iota(jnp.int32, (8, LANES), 0) == 0
    for slab in range(SSM_CONV_DIM // LANES):
        cs = slice(slab * LANES, (slab + 1) * LANES)
        if slab < SSM_DINNER // LANES:
            cur = x_ref[:, cs]
        elif slab < (SSM_DINNER + SSM_GROUPS * SSM_STATE) // LANES:
            cur = bm_ref[:, slab * LANES - SSM_DINNER:(slab + 1) * LANES - SSM_DINNER]
        else:
            off = SSM_DINNER + SSM_GROUPS * SSM_STATE
            cur = cm_ref[:, slab * LANES - off:(slab + 1) * LANES - off]
        prev_tail = halo_ref[:, cs]
        wrapped = []
        for s in range(SSM_CONV - 1):
            tail = cur[c - hr + 8 * s:c - hr + 8 * (s + 1)]
            moved = pltpu.roll(tail, 1, 0)
            wrapped.append(jnp.where(first_sublane, prev_tail[8 * s + 7:8 * s + 8], moved))
        halo_ref[:, cs] = cur[c - hr:c]
        acc = cb_ref[:, cs] + cw_ref[SSM_CONV - 1:SSM_CONV, cs] * cur
        for s in range(1, SSM_CONV):
            shifted = jnp.concatenate(wrapped[SSM_CONV - 1 - s:] + [cur[0:c - 8 * s]], axis=0)
            acc = acc + cw_ref[SSM_CONV - 1 - s:SSM_CONV - s, cs] * shifted
        act = acc * jax.nn.sigmoid(acc)
        if slab < N_PAIRS:
            xs_ref[slab] = act
        elif slab < N_PAIRS + SSM_GROUPS:
            bs_ref[slab - N_PAIRS] = act
        else:
            cs_ref[slab - N_PAIRS - SSM_GROUPS] = act

    lane = lax.broadcasted_iota(jnp.int32, (1, LANES), 1)
    dtr = dt_ref[...] + dtb_ref[...]
    dt = jnp.maximum(dtr, 0.0) + jnp.log1p(jnp.exp(-jnp.abs(dtr)))
    a = jnp.where(lane < SSM_HEADS, -jnp.exp(alog_ref[...]), 0.0)
    la_hi, la_mid, la_lo = _split3(dt * (a * LOG2E))
    b = jnp.dot(tri3_ref[...], jnp.concatenate([la_hi, la_mid, la_lo], axis=0),
                preferred_element_type=F32)
    b_last = b[c - 1:c]
    wd = jnp.exp2(b_last - b) * dt
    bt = b.T
    bt_ref[...] = bt
    dtt_ref[...] = (b - jnp.log2(dt)).T
    wdt_ref[...] = wd.T
    elb_ref[...] = jnp.broadcast_to(jnp.exp2(bt[:, c - 1:c]), (LANES, LANES))

    row = lax.broadcasted_iota(jnp.int32, (c, c), 0)
    col = lax.broadcasted_iota(jnp.int32, (c, c), 1)
    causal = (col % 8) * nt_rows + col // 8 <= (row % 8) * nt_rows + row // 8
    lo = lax.broadcasted_iota(jnp.int32, (c, LANES), 1) < SSM_HEADDIM

    def group(g, carry):
        bg = bs_ref[g]
        cg = cs_ref[g]
        cgb = cg.astype(BF16)
        cbm = lax.dot_general(cgb, bg.astype(BF16), _NT, preferred_element_type=F32)
        bgt = bg.T
        for j in range(PAIRS_PER_GROUP):
            p = g * PAIRS_PER_GROUP + j
            xp = xs_ref[p]
            sp = st_ref[p]
            x_lo = jnp.where(lo, xp, 0.0).astype(BF16)
            x_hi = jnp.where(lo, 0.0, xp).astype(BF16)
            x_bd = jnp.concatenate([x_lo, x_hi], axis=0)
            lhs_y = []
            lhs_s = []
            bhs = []
            for k in range(2):
                h = 2 * p + k
                bh = jnp.broadcast_to(bt_ref[pl.ds(h, 1), :], (c, c)).T
                rel = bh - dtt_ref[pl.ds(h, 1), :]
                dec = jnp.exp2(jnp.where(causal, rel, MASKED_LOG2))
                lhs_y.append((cbm * dec).astype(BF16))
                lhs_s.append((bgt * wdt_ref[pl.ds(h, 1), :]).astype(BF16))
                bhs.append(bh)
            y = jnp.exp2(jnp.where(lo, bhs[0], bhs[1])) * jnp.dot(
                cgb, sp.astype(BF16), preferred_element_type=F32)
            y = y + jnp.dot(jnp.concatenate(lhs_y, axis=1), x_bd, preferred_element_type=F32)
            su = jnp.dot(jnp.concatenate(lhs_s, axis=1), x_bd,
                         preferred_element_type=F32)
            el = jnp.where(lane < SSM_HEADDIM, elb_ref[pl.ds(2 * p, 1), :],
                           elb_ref[pl.ds(2 * p + 1, 1), :])
            st_ref[p] = sp * el + su
            y_ref[p] = y + dsk_ref[p] * xp
        return carry

    lax.fori_loop(0, SSM_GROUPS, group, 0, unroll=8)

    gw = PAIRS_PER_GROUP * LANES
    for g in range(SSM_GROUPS):
        cs = slice(g * gw, (g + 1) * gw)
        yg = jnp.concatenate([y_ref[g * PAIRS_PER_GROUP + j] for j in range(PAIRS_PER_GROUP)],
                             axis=1)
        zg = z_ref[:, cs]
        yg = yg * (zg * jax.nn.sigmoid(zg))
        og = _rms(yg, nw_ref[:, cs]).astype(BF16)
        o_ref[:, cs] = jnp.dot(unperm_ref[...], og, preferred_element_type=F32).astype(o_ref.dtype)


def _ssd_core(proj, conv_w, conv_b, dt_bias, a_log, d_skip, norm_w, *, batch, seq):
    c = SSD_CHUNK
    nl = seq // c
    t = batch * seq
    perm = _interleave_matrix()
    tok = perm.argmax(axis=1)
    tri = (tok[None, :] <= tok[:, None]).astype(np.float32)
    tri3 = jnp.asarray(np.concatenate([tri, tri, tri], axis=1), BF16)
    unperm = jnp.asarray(perm.T, BF16)
    pad = LANES - SSM_HEADS
    dtb = jnp.pad(dt_bias.astype(F32), (0, pad)).reshape(1, LANES)
    alog = jnp.pad(a_log.astype(F32), (0, pad)).reshape(1, LANES)
    dsk = jnp.repeat(d_skip.astype(F32), SSM_HEADDIM).reshape(N_PAIRS, 1, LANES)

    def rb(b, l):
        return b * nl + l

    full2 = lambda b, l: (0, 0)
    in_specs = [
        pl.BlockSpec((c, SSM_DINNER), lambda b, l: (rb(b, l), 0)),
        pl.BlockSpec((c, SSM_DINNER), lambda b, l: (rb(b, l), 1)),
        pl.BlockSpec((c, 1024), lambda b, l: (rb(b, l), 2 * SSM_DINNER // 1024)),
        pl.BlockSpec((c, 1024), lambda b, l: (rb(b, l), 2 * SSM_DINNER // 1024 + 1)),
        pl.BlockSpec((c, LANES), lambda b, l: (rb(b, l), (SSM_DINNER + SSM_CONV_DIM) // LANES)),
        pl.BlockSpec((SSM_CONV, SSM_CONV_DIM), full2),
        pl.BlockSpec((1, SSM_CONV_DIM), full2),
        pl.BlockSpec((1, LANES), full2),
        pl.BlockSpec((1, LANES), full2),
        pl.BlockSpec((N_PAIRS, 1, LANES), lambda b, l: (0, 0, 0)),
        pl.BlockSpec((1, SSM_DINNER), full2),
        pl.BlockSpec(tri3.shape, full2),
        pl.BlockSpec(unperm.shape, full2),
    ]
    scratch = [
        pltpu.VMEM((CONV_HALO_ROWS, SSM_CONV_DIM), F32),
        pltpu.VMEM((N_PAIRS, c, LANES), F32),
        pltpu.VMEM((SSM_GROUPS, c, SSM_STATE), F32),
        pltpu.VMEM((SSM_GROUPS, c, SSM_STATE), F32),
        pltpu.VMEM((N_PAIRS, SSM_STATE, LANES), F32),
        pltpu.VMEM((N_PAIRS, c, LANES), F32),
        pltpu.VMEM((LANES, c), F32),
        pltpu.VMEM((LANES, c), F32),
        pltpu.VMEM((LANES, c), F32),
        pltpu.VMEM((LANES, LANES), F32),
    ]
    return pl.pallas_call(
        _ssd_kernel,
        grid=(batch, nl),
        in_specs=in_specs,
        out_specs=pl.BlockSpec((c, SSM_DINNER), lambda b, l: (rb(b, l), 0)),
        out_shape=jax.ShapeDtypeStruct((t, SSM_DINNER), BF16),
        scratch_shapes=scratch,
        compiler_params=_cparams(("parallel", "arbitrary")),
        name="ssd_core",
    )(proj, proj, proj, proj, proj, conv_w.astype(F32), conv_b.astype(F32).reshape(1, -1),
      dtb, alog, dsk, norm_w.astype(F32).reshape(1, -1), tri3, unperm)


def _pad_last(w, n):
    return jnp.pad(w, [(0, 0)] * (w.ndim - 1) + [(0, n - w.shape[-1])])


def kernel(x, p, norm_mix, norm_mlp, norm_ple, norm_final, w_up, w_down, w_ple_proj, w_ple_gate,
           gla_w_in, gla_w_gk2, gla_b_gk, gla_gn, gla_w_out,
           hgrn_lb_logits, hgrn_w_in, hgrn_gn, hgrn_w_out,
           ssm_w_in, ssm_conv_w, ssm_conv_b, ssm_dt_bias, ssm_a_log, ssm_d, ssm_norm, ssm_w_out):
    batch, seq, d = x.shape
    t = batch * seq
    h = x.reshape(t, d)
    pf = p.reshape(DEPTH, t, PLE_DIM)
    row = lambda v: v.astype(F32).reshape(1, -1)

    gla_w_in_b = gla_w_in[..., :GLA_MAIN_COLS].astype(BF16)
    gla_w_lr_b = _pad_last(gla_w_in[..., GLA_MAIN_COLS:], LANES).astype(BF16)
    hgrn_w_in_b = hgrn_w_in.astype(BF16)
    ssm_w_in_b = _pad_last(ssm_w_in, SSM_IN_COLS).astype(BF16)
    gla_w_out_b = gla_w_out.astype(BF16)
    hgrn_w_out_b = hgrn_w_out.astype(BF16)
    ssm_w_out_b = ssm_w_out.astype(BF16)
    w_up_b = w_up.astype(BF16)
    w_down_b = w_down.astype(BF16)
    w_gate_b = w_ple_gate.astype(BF16)
    w_proj_b = w_ple_proj.astype(BF16)

    for i in range(DEPTH):
        kind, j = i % N_MIXERS, i // N_MIXERS
        nw = row(norm_mix[i])
        if kind == 0:
            proj, glr = _norm_matmul(h, nw, gla_w_in_b, j, tm=1024, tn=2048, w_narrow=gla_w_lr_b)
            kd = GLA_HEADS * GLA_DK
            vd = GLA_HEADS * GLA_DV
            wgk = jnp.pad(gla_w_gk2[j], ((0, LANES - GLA_GATE_RANK), (0, 0))).astype(BF16)
            o = _gla_core(proj, mode="gla", layer=i, batch=batch, seq=seq, heads=GLA_HEADS,
                          hps=GLA_HEADS, pack=1, dk=GLA_DK, dv=GLA_DV, col_q=0, col_k=kd, col_v=2 * kd,
                          col_og=2 * kd + vd,
                          extra=(glr, wgk, row(gla_b_gk[j])), gn=row(gla_gn[j]))
            h = _matmul_res(o, gla_w_out_b, j, h, tm=1024)
        elif kind == 1:
            proj = _norm_matmul(h, nw, hgrn_w_in_b, j, tm=1024, tn=2048)
            fd = HGRN_HEADS * HGRN_DK
            vd = HGRN_HEADS * HGRN_DV
            o = _gla_core(proj, mode="hgrn", layer=i, batch=batch, seq=seq, heads=HGRN_HEADS,
                          hps=HGRN_HEADS // 2, pack=2, dk=HGRN_DK, dv=HGRN_DV, col_q=0, col_k=fd,
                          col_v=2 * fd, col_og=2 * fd + vd,
                          extra=(hgrn_lb_logits.astype(F32),), gn=row(hgrn_gn[j]),
                          tl=2 * GLA_TOKENS_PER_STEP)
            h = _matmul_res(o, hgrn_w_out_b, j, h, tm=1024)
        else:
            proj = _norm_matmul(h, nw, ssm_w_in_b, j, tm=1024, tn=1792, interleave=True)
            o = _ssd_core(proj, ssm_conv_w[j], ssm_conv_b[j], ssm_dt_bias[j], ssm_a_log[j],
                          ssm_d[j], ssm_norm[j], batch=batch, seq=seq)
            h = _matmul_res(o, ssm_w_out_b, j, h, tm=512)
        h = _mlp(h, row(norm_mlp[i]), w_up_b, w_down_b, i, tm=1024, tf=1024)
        h = _ple(h, row(norm_ple[i]), w_gate_b, pf, w_proj_b, row(norm_final), i,
                 final_norm=(i == DEPTH - 1), tm=1024, tn=512)
    return h.reshape(batch, seq, d)
```

```python
import functools
import math

import numpy as np
import jax
import jax.numpy as jnp
from jax import lax
from jax.experimental import pallas as pl
from jax.experimental.pallas import tpu as pltpu

F32 = jnp.float32
BF16 = jnp.bfloat16

D_MODEL = 2048
DEPTH = 4
N_MIXERS = 3
PLE_DIM = 256
D_FF = 4 * D_MODEL
EPS = 1e-6

GLA_HEADS = 4
GLA_DK = 256
GLA_DV = 512
GLA_GATE_RANK = 16
GLA_GATE_NORM = 16.0
HGRN_HEADS = 16
HGRN_DK = 128
HGRN_DV = 128
SSM_DINNER = 4096
SSM_HEADDIM = 64
SSM_HEADS = 64
SSM_GROUPS = 8
SSM_STATE = 128
SSM_CONV = 4
SSM_CONV_DIM = SSM_DINNER + 2 * SSM_GROUPS * SSM_STATE

LANES = 128
VMEM_LIMIT_BYTES = 60 * 1024 * 1024

NORM_ROWS = 256
GLA_CHUNK = 64
GLA_TOKENS_PER_STEP = 512
SSD_CHUNK = 128
CONV_HALO_ROWS = 8 * (SSM_CONV - 1)

GLA_MAIN_COLS = 6144
SSM_MAIN_COLS = SSM_DINNER + SSM_CONV_DIM
N_LEVELS = 6
LOG2E = math.log2(math.e)
MASKED_LOG2 = -1e30


def _cparams(sem):
    return pltpu.CompilerParams(dimension_semantics=sem, vmem_limit_bytes=VMEM_LIMIT_BYTES)


def _rms(x, w):
    ms = jnp.mean(x * x, axis=-1, keepdims=True)
    return x * lax.rsqrt(ms + EPS) * w


def _split3(x):
    hi = x.astype(BF16)
    r1 = x - hi.astype(F32)
    mid = r1.astype(BF16)
    lo = (r1 - mid.astype(F32)).astype(BF16)
    return hi, mid, lo


def _interleave_matrix():
    r = np.arange(SSD_CHUNK)
    p = np.zeros((SSD_CHUNK, SSD_CHUNK), np.float32)
    p[r, (r % 8) * (SSD_CHUNK // 8) + r // 8] = 1.0
    return p


def _norm_matmul_kernel(*refs, interleave, narrow):
    refs = list(refs)
    h_ref, nw_ref, w_ref = refs[:3]
    u_ref = refs.pop()
    ox_ref = refs.pop() if narrow else None
    o_ref = refs.pop()
    wx_ref = refs.pop() if narrow else None
    pm_ref = refs.pop() if interleave else None
    j = pl.program_id(1)

    @pl.when(j == 0)
    def _():
        for r0 in range(0, h_ref.shape[0], NORM_ROWS):
            rs = slice(r0, r0 + NORM_ROWS)
            u = _rms(h_ref[rs, :], nw_ref[...]).astype(BF16)
            if interleave:
                u = jnp.concatenate(
                    [jnp.dot(pm_ref[...], u[g0:g0 + SSD_CHUNK], preferred_element_type=F32)
                     for g0 in range(0, NORM_ROWS, SSD_CHUNK)], axis=0).astype(BF16)
            u_ref[rs, :] = u
            o_ref[rs, :] = jnp.dot(u, w_ref[...], preferred_element_type=F32).astype(o_ref.dtype)
            if narrow:
                ox_ref[rs, :] = jnp.dot(u, wx_ref[...], preferred_element_type=F32)

    @pl.when(j > 0)
    def _():
        o_ref[...] = jnp.dot(u_ref[...], w_ref[...], preferred_element_type=F32).astype(o_ref.dtype)


def _norm_matmul(h, nw, w, layer, *, tm, tn, interleave=False, w_narrow=None):
    t, d = h.shape
    n = w.shape[2]
    in_specs = [
        pl.BlockSpec((tm, d), lambda i, j: (i, 0)),
        pl.BlockSpec((1, d), lambda i, j: (0, 0)),
        pl.BlockSpec((None, d, tn), lambda i, j: (layer, 0, j)),
    ]
    args = [h, nw, w]
    out_specs = pl.BlockSpec((tm, tn), lambda i, j: (i, j))
    out_shape = jax.ShapeDtypeStruct((t, n), F32)
    if interleave:
        in_specs.append(pl.BlockSpec((SSD_CHUNK, SSD_CHUNK), lambda i, j: (0, 0)))
        args.append(jnp.asarray(_interleave_matrix(), BF16))
    if w_narrow is not None:
        in_specs.append(pl.BlockSpec((None, d, LANES), lambda i, j: (layer, 0, 0)))
        args.append(w_narrow)
        out_specs = [out_specs, pl.BlockSpec((tm, LANES), lambda i, j: (i, 0))]
        out_shape = [out_shape, jax.ShapeDtypeStruct((t, LANES), F32)]
    return pl.pallas_call(
        functools.partial(_norm_matmul_kernel, interleave=interleave, narrow=w_narrow is not None),
        grid=(t // tm, n // tn),
        in_specs=in_specs,
        out_specs=out_specs,
        out_shape=out_shape,
        scratch_shapes=[pltpu.VMEM((tm, d), BF16)],
        compiler_params=_cparams(("parallel", "arbitrary")),
        name="norm_matmul",
    )(*args)


def _matmul_res_kernel(a_ref, w_ref, h_ref, o_ref):
    o_ref[...] = h_ref[...] + jnp.dot(a_ref[...], w_ref[...], preferred_element_type=F32)


def _matmul_res(a, w, layer, h, *, tm):
    t, k = a.shape
    n = w.shape[2]
    return pl.pallas_call(
        _matmul_res_kernel,
        grid=(t // tm,),
        in_specs=[
            pl.BlockSpec((tm, k), lambda i: (i, 0)),
            pl.BlockSpec((None, k, n), lambda i: (layer, 0, 0), pipeline_mode=pl.Buffered(1)),
            pl.BlockSpec((tm, n), lambda i: (i, 0)),
        ],
        out_specs=pl.BlockSpec((tm, n), lambda i: (i, 0)),
        out_shape=jax.ShapeDtypeStruct((t, n), F32),
        compiler_params=_cparams(("parallel",)),
        name="matmul_res",
    )(a, w, h)


def _mlp_kernel(h_ref, nw_ref, wu_ref, wd_ref, o_ref, u_ref):
    f = pl.program_id(1)

    def ff(u):
        a = jnp.dot(u, wu_ref[...], preferred_element_type=F32)
        a = jnp.square(jnp.maximum(a, 0.0)).astype(BF16)
        return jnp.dot(a, wd_ref[...], preferred_element_type=F32)

    @pl.when(f == 0)
    def _():
        for r0 in range(0, h_ref.shape[0], NORM_ROWS):
            rs = slice(r0, r0 + NORM_ROWS)
            x = h_ref[rs, :]
            u = _rms(x, nw_ref[...]).astype(BF16)
            u_ref[rs, :] = u
            o_ref[rs, :] = x + ff(u)

    @pl.when(f > 0)
    def _():
        o_ref[...] += ff(u_ref[...])


def _mlp(h, nw, wu, wd, layer, *, tm, tf):
    t, d = h.shape
    ff = wu.shape[2]
    return pl.pallas_call(
        _mlp_kernel,
        grid=(t // tm, ff // tf),
        in_specs=[
            pl.BlockSpec((tm, d), lambda i, f: (i, 0)),
            pl.BlockSpec((1, d), lambda i, f: (0, 0)),
            pl.BlockSpec((None, d, tf), lambda i, f: (layer, 0, f)),
            pl.BlockSpec((None, tf, d), lambda i, f: (layer, f, 0)),
        ],
        out_specs=pl.BlockSpec((tm, d), lambda i, f: (i, 0)),
        out_shape=jax.ShapeDtypeStruct((t, d), F32),
        scratch_shapes=[pltpu.VMEM((tm, d), BF16)],
        compiler_params=_cparams(("parallel", "arbitrary")),
        name="mlp",
    )(h, nw, wu, wd)


def _ple_kernel(h_ref, nw_ref, wg_ref, p_ref, wp_ref, nf_ref, o_ref, *, final_norm, tn):
    d = h_ref.shape[1]
    for r0 in range(0, h_ref.shape[0], NORM_ROWS):
        rs = slice(r0, r0 + NORM_ROWS)
        u = _rms(h_ref[rs, :], nw_ref[...]).astype(BF16)
        pb = p_ref[rs, :].astype(BF16)
        for n0 in range(0, d, tn):
            cs = slice(n0, n0 + tn)
            gate = jax.nn.sigmoid(jnp.dot(u, wg_ref[:, cs], preferred_element_type=F32))
            proj = jnp.dot(pb, wp_ref[:, cs], preferred_element_type=F32)
            o_ref[rs, cs] = h_ref[rs, cs] + gate * proj
        if final_norm:
            o_ref[rs, :] = _rms(o_ref[rs, :], nf_ref[...])


def _ple(h, nw, wg, p, wp, nf, layer, *, final_norm, tm, tn):
    t, d = h.shape
    pd = p.shape[2]
    return pl.pallas_call(
        functools.partial(_ple_kernel, final_norm=final_norm, tn=tn),
        grid=(t // tm,),
        in_specs=[
            pl.BlockSpec((tm, d), lambda i: (i, 0)),
            pl.BlockSpec((1, d), lambda i: (0, 0)),
            pl.BlockSpec((None, d, d), lambda i: (layer, 0, 0), pipeline_mode=pl.Buffered(1)),
            pl.BlockSpec((None, tm, pd), lambda i: (layer, i, 0)),
            pl.BlockSpec((None, pd, d), lambda i: (layer, 0, 0), pipeline_mode=pl.Buffered(1)),
            pl.BlockSpec((1, d), lambda i: (0, 0)),
        ],
        out_specs=pl.BlockSpec((tm, d), lambda i: (i, 0)),
        out_shape=jax.ShapeDtypeStruct((t, d), F32),
        compiler_params=_cparams(("parallel",)),
        name="ple",
    )(h, nw, wg, p, wp, nf)


MATMUL_LEVELS = (3, 4)


def _gla_arg_matrix():
    c = GLA_CHUNK
    t = np.arange(c)[:, None]
    r = np.arange(c)[None, :]
    blocks = [(r <= t)]
    for lvl in MATMUL_LEVELS:
        n = c >> (lvl + 1)
        ref = (t // (2 * n)) * (2 * n) + n - 1
        lower = t > ref
        blocks.append(np.where(lower, (r > ref) & (r <= t), (r > t) & (r <= ref)))
    m = np.concatenate(blocks, axis=0).astype(np.float32)
    return np.concatenate([m, m, m, np.zeros_like(m)], axis=1)


def _coarse_level_args(b, n):
    out = []
    for blk in range(GLA_CHUNK // (2 * n)):
        r0 = blk * 2 * n
        ref = b[r0 + n - 1:r0 + n]
        out.append(ref - b[r0:r0 + n])
        out.append(b[r0 + n:r0 + 2 * n] - ref)
    return jnp.concatenate(out, axis=0)


def _gla_masks(pack):
    c = GLA_CHUNK
    t = np.arange(c)[:, None]
    s = np.arange(c)[None, :]
    masks = np.zeros((1 + N_LEVELS, c, c), np.float32)
    masks[0] = (t == s)
    for lvl in range(N_LEVELS):
        n = c >> (lvl + 1)
        same = (t // (2 * n)) == (s // (2 * n))
        masks[1 + lvl] = same & ((t % (2 * n)) >= n) & ((s % (2 * n)) < n)
    return np.tile(masks, (1, 1, pack))


def _block_diag_rows(x, pack, width):
    if pack == 1:
        return x
    head = lax.broadcasted_iota(jnp.int32, x.shape, 1) // width
    return jnp.concatenate([jnp.where(head == p, x, jnp.zeros_like(x)) for p in range(pack)], axis=0)


_NT = (((1,), (1,)), ((), ()))
_TN = (((0,), (0,)), ((), ()))


def _gla_core_kernel(*refs, mode, layer, scale, packs, pack, dk, dv):
    wk = pack * dk
    wv = pack * dv
    if mode == "gla":
        (q_ref, k_ref, v_ref, og_ref, glr_ref, wgk_ref, bgk_ref, gn_ref, mall_ref, masks_ref,
         o_ref, st_ref) = refs
    else:
        (q_ref, k_ref, v_ref, og_ref, lbl_ref, gn_ref, mall_ref, masks_ref, o_ref, st_ref) = refs

    @pl.when(pl.program_id(2) == 0)
    def _():
        st_ref[...] = jnp.zeros_like(st_ref)

    if mode == "hgrn":
        lg = lbl_ref[...]
        e = jnp.exp(lg - jnp.max(lg, axis=0, keepdims=True))
        sm = e / jnp.sum(e, axis=0, keepdims=True)
        lb_all = jnp.zeros_like(sm[0:1])
        for r in range(1, layer + 1):
            lb_all = lb_all + sm[r:r + 1]

    c = GLA_CHUNK
    c2 = 2 * c
    n_chunks = q_ref.shape[0] // c2
    odd_row = lax.broadcasted_iota(jnp.int32, (c, wk), 0) % 2 == 1

    def nt(a, b):
        return lax.dot_general(a, b, _NT, preferred_element_type=F32)

    def half_terms(qh, kh, gh):
        g_hi, g_mid, g_lo = _split3(gh)
        gs = jnp.concatenate([g_hi, g_mid, g_lo, jnp.zeros_like(g_hi)], axis=0)
        pre = jnp.dot(mall_ref[...], gs, preferred_element_type=F32)
        b = pre[0:c]
        lvl_args = [_coarse_level_args(b, c >> (lvl + 1)) for lvl in range(MATMUL_LEVELS[0])]
        lvl_args += [pre[c:2 * c], pre[2 * c:3 * c], jnp.where(odd_row, gh, 0.0)]
        att = masks_ref[0] * nt(qh.astype(BF16), _block_diag_rows(kh.astype(BF16), pack, dk))
        for lvl in range(N_LEVELS):
            n = c >> (lvl + 1)
            el = jnp.exp2(lvl_args[lvl])
            if n % 8 == 0:
                mixed = jnp.concatenate(
                    [x[r0:r0 + n] for blk in range(c // (2 * n))
                     for x, r0 in ((kh, blk * 2 * n), (qh, blk * 2 * n + n))], axis=0)
                ql = (mixed * el).astype(BF16)
                kl = ql
            else:
                ql = (qh * el).astype(BF16)
                kl = (kh * el).astype(BF16)
            att = att + masks_ref[1 + lvl] * nt(ql, _block_diag_rows(kl, pack, dk))
        return b, att

    def chunk(ci, carry):
        rows = pl.ds(pl.multiple_of(ci * c2, c2), c2)
        if mode == "gla":
            z_all = jnp.dot(glr_ref[rows, :].astype(BF16), wgk_ref[...],
                            preferred_element_type=F32) + bgk_ref[...]
        for g in range(packs):
            ks = slice(g * wk, (g + 1) * wk)
            vs = slice(g * wv, (g + 1) * wv)
            vc = v_ref[rows, vs]
            og = og_ref[rows, vs]
            if mode == "gla":
                qc = q_ref[rows, ks] * scale
                kc = k_ref[rows, ks]
                z = z_all[:, ks]
                gc = (jnp.minimum(z, 0.0) - jnp.log1p(jnp.exp(-jnp.abs(z)))) * (LOG2E / GLA_GATE_NORM)
                gate = og * jax.nn.sigmoid(og)
            else:
                qq = q_ref[rows, ks]
                qc = qq * jax.nn.sigmoid(qq) * scale
                lb = lb_all[:, ks]
                f = lb + (1.0 - lb) * jax.nn.sigmoid(k_ref[rows, ks])
                kc = 1.0 - f
                gc = jnp.log(f) * LOG2E
                gate = jax.nn.sigmoid(og)

            q0, q1 = qc[0:c], qc[c:c2]
            k0, k1 = kc[0:c], kc[c:c2]
            b0, att0 = half_terms(q0, k0, gc[0:c])
            b1, att1 = half_terms(q1, k1, gc[c:c2])
            e_b0 = jnp.exp2(b0)
            e_b1 = jnp.exp2(b1)
            e_rev0 = jnp.exp2(b0[c - 1:c] - b0)
            e_rev1 = jnp.exp2(b1[c - 1:c] - b1)
            e_l0 = e_b0[c - 1:c]
            e_l1 = e_b1[c - 1:c]

            k0_end = (k0 * e_rev0).astype(BF16)
            cross = nt((q1 * e_b1).astype(BF16), _block_diag_rows(k0_end, pack, dk))

            st = st_ref[g]
            q_in = jnp.concatenate([q0 * e_b0, q1 * (e_b1 * e_l0)], axis=0).astype(BF16)
            o = nt(q_in, st.astype(BF16))
            vb = vc.astype(BF16)
            v0 = _block_diag_rows(vb[0:c], pack, dv)
            v1 = _block_diag_rows(vb[c:c2], pack, dv)
            o0 = o[0:c] + jnp.dot(att0.astype(BF16), v0, preferred_element_type=F32)
            o1 = (o[c:c2] + jnp.dot(cross.astype(BF16), v0, preferred_element_type=F32)
                  + jnp.dot(att1.astype(BF16), v1, preferred_element_type=F32))
            o = jnp.concatenate([o0, o1], axis=0)

            k_dec = jnp.concatenate([k0 * (e_rev0 * e_l1), k1 * e_rev1], axis=0).astype(BF16)
            e_step = e_l0 * e_l1
            for p in range(pack):
                rs = slice(p * dv, (p + 1) * dv)
                cs = slice(p * dk, (p + 1) * dk)
                upd = lax.dot_general(vb[:, rs], k_dec[:, cs], _TN, preferred_element_type=F32)
                st_ref[g, rs, cs] = st[rs, cs] * e_step[:, cs] + upd

            for p in range(pack):
                hs = slice(p * dv, (p + 1) * dv)
                o_ref[rows, g * wv + p * dv:g * wv + (p + 1) * dv] = (
                    _rms(o[:, hs], gn_ref[...]) * gate[:, hs]).astype(o_ref.dtype)
        return carry

    lax.fori_loop(0, n_chunks, chunk, 0, unroll=2)


def _gla_core(proj, *, mode, layer, batch, seq, heads, hps, pack, dk, dv, col_q, col_k, col_v,
              col_og, extra, gn, tl=GLA_TOKENS_PER_STEP):
    nl = seq // tl
    t = batch * seq
    wk = hps * dk
    wv = hps * dv
    mall = jnp.asarray(_gla_arg_matrix(), BF16)
    masks = jnp.asarray(_gla_masks(pack), F32)

    def rowblk(b, h, l):
        return b * nl + l

    in_specs = [
        pl.BlockSpec((tl, wk), lambda b, h, l: (rowblk(b, h, l), col_q // wk + h)),
        pl.BlockSpec((tl, wk), lambda b, h, l: (rowblk(b, h, l), col_k // wk + h)),
        pl.BlockSpec((tl, wv), lambda b, h, l: (rowblk(b, h, l), col_v // wv + h)),
        pl.BlockSpec((tl, wv), lambda b, h, l: (rowblk(b, h, l), col_og // wv + h)),
    ]
    args = [proj, proj, proj, proj]
    if mode == "gla":
        glr, wgk, bgk = extra
        in_specs += [
            pl.BlockSpec((tl, LANES), lambda b, h, l: (rowblk(b, h, l), 0)),
            pl.BlockSpec((LANES, wk), lambda b, h, l: (0, h)),
            pl.BlockSpec((1, wk), lambda b, h, l: (0, h)),
        ]
        args += [glr, wgk, bgk]
    else:
        (lbl,) = extra
        in_specs += [pl.BlockSpec((DEPTH, wk), lambda b, h, l: (0, h))]
        args += [lbl]
    in_specs += [
        pl.BlockSpec((1, dv), lambda b, h, l: (0, 0)),
        pl.BlockSpec(mall.shape, lambda b, h, l: (0, 0)),
        pl.BlockSpec(masks.shape, lambda b, h, l: (0, 0, 0)),
    ]
    args += [gn, mall, masks]
    return pl.pallas_call(
        functools.partial(_gla_core_kernel, mode=mode, layer=layer, scale=dk ** -0.5,
                          packs=hps // pack, pack=pack, dk=dk, dv=dv),
        grid=(batch, heads // hps, nl),
        in_specs=in_specs,
        out_specs=pl.BlockSpec((tl, wv), lambda b, h, l: (rowblk(b, h, l), h)),
        out_shape=jax.ShapeDtypeStruct((t, heads * dv), BF16),
        scratch_shapes=[pltpu.VMEM((hps // pack, pack * dv, pack * dk), F32)],
        compiler_params=_cparams(("parallel", "parallel", "arbitrary")),
        name=mode + "_core",
    )(*args)


N_PAIRS = SSM_HEADS // 2
PAIRS_PER_GROUP = N_PAIRS // SSM_GROUPS


def _ssd_kernel(z_ref, x_ref, bm_ref, cm_ref, dt_ref, cw_ref, cb_ref, dtb_ref, alog_ref, dsk_ref,
                nw_ref, tri3_ref, unperm_ref, o_ref,
                halo_ref, xs_ref, bs_ref, cs_ref, st_ref, y_ref, bt_ref, dtt_ref, wdt_ref, elb_ref):
    c = SSD_CHUNK
    nt_rows = c // 8
    hr = CONV_HALO_ROWS
    l = pl.program_id(1)

    @pl.when(l == 0)
    def _():
        st_ref[...] = jnp.zeros_like(st_ref)
        halo_ref[...] = jnp.zeros_like(halo_ref)

    first_sublane = lax.broadcasted_iota(jnp.int32, (8, LANES), 0) == 0
    for slab in range(SSM_CONV_DIM // LANES):
        cs = slice(slab * LANES, (slab + 1) * LANES)
        if slab < SSM_DINNER // LANES:
            cur = x_ref[:, cs]
        elif slab < (SSM_DINNER + SSM_GROUPS * SSM_STATE) // LANES:
            cur = bm_ref[:, slab * LANES - SSM_DINNER:(slab + 1) * LANES - SSM_DINNER]
        else:
            off = SSM_DINNER + SSM_GROUPS * SSM_STATE
            cur = cm_ref[:, slab * LANES - off:(slab + 1) * LANES - off]
        prev_tail = halo_ref[:, cs]
        wrapped = []
        for s in range(SSM_CONV - 1):
            tail = cur[c - hr + 8 * s:c - hr + 8 * (s + 1)]
            moved = pltpu.roll(tail, 1, 0)
            wrapped.append(jnp.where(first_sublane, prev_tail[8 * s + 7:8 * s + 8], moved))
        halo_ref[:, cs] = cur[c - hr:c]
        acc = cb_ref[:, cs] + cw_ref[SSM_CONV - 1:SSM_CONV, cs] * cur
        for s in range(1, SSM_CONV):
            shifted = jnp.concatenate(wrapped[SSM_CONV - 1 - s:] + [cur[0:c - 8 * s]], axis=0)
            acc = acc + cw_ref[SSM_CONV - 1 - s:SSM_CONV - s, cs] * shifted
        act = acc * jax.nn.sigmoid(acc)
        if slab < N_PAIRS:
            xs_ref[slab] = act
        elif slab < N_PAIRS + SSM_GROUPS:
            bs_ref[slab - N_PAIRS] = act
        else:
            cs_ref[slab - N_PAIRS - SSM_GROUPS] = act

    lane = lax.broadcasted_iota(jnp.int32, (1, LANES), 1)
    dtr = dt_ref[...] + dtb_ref[...]
    dt = jnp.maximum(dtr, 0.0) + jnp.log1p(jnp.exp(-jnp.abs(dtr)))
    a = jnp.where(lane < SSM_HEADS, -jnp.exp(alog_ref[...]), 0.0)
    la_hi, la_mid, la_lo = _split3(dt * (a * LOG2E))
    b = jnp.dot(tri3_ref[...], jnp.concatenate([la_hi, la_mid, la_lo], axis=0),
                preferred_element_type=F32)
    b_last = b[c - 1:c]
    wd = jnp.exp2(b_last - b) * dt
    bt = b.T
    bt_ref[...] = bt
    dtt_ref[...] = (b - jnp.log2(dt)).T
    wdt_ref[...] = wd.T
    elb_ref[...] = jnp.broadcast_to(jnp.exp2(bt[:, c - 1:c]), (LANES, LANES))

    row = lax.broadcasted_iota(jnp.int32, (c, c), 0)
    col = lax.broadcasted_iota(jnp.int32, (c, c), 1)
    causal = (col % 8) * nt_rows + col // 8 <= (row % 8) * nt_rows + row // 8
    lo = lax.broadcasted_iota(jnp.int32, (c, LANES), 1) < SSM_HEADDIM

    def group(g, carry):
        bg = bs_ref[g]
        cg = cs_ref[g]
        cgb = cg.astype(BF16)
        cbm = lax.dot_general(cgb, bg.astype(BF16), _NT, preferred_element_type=F32)
        bgt = bg.T
        for j in range(PAIRS_PER_GROUP):
            p = g * PAIRS_PER_GROUP + j
            xp = xs_ref[p]
            sp = st_ref[p]
            x_lo = jnp.where(lo, xp, 0.0).astype(BF16)
            x_hi = jnp.where(lo, 0.0, xp).astype(BF16)
            x_bd = jnp.concatenate([x_lo, x_hi], axis=0)
            lhs_y = []
            lhs_s = []
            bhs = []
            for k in range(2):
                h = 2 * p + k
                bh = jnp.broadcast_to(bt_ref[pl.ds(h, 1), :], (c, c)).T
                rel = bh - dtt_ref[pl.ds(h, 1), :]
                dec = jnp.exp2(jnp.where(causal, rel, MASKED_LOG2))
                lhs_y.append((cbm * dec).astype(BF16))
                lhs_s.append((bgt * wdt_ref[pl.ds(h, 1), :]).astype(BF16))
                bhs.append(bh)
            y = jnp.exp2(jnp.where(lo, bhs[0], bhs[1])) * jnp.dot(
                cgb, sp.astype(BF16), preferred_element_type=F32)
            y = y + jnp.dot(jnp.concatenate(lhs_y, axis=1), x_bd, preferred_element_type=F32)
            su = jnp.dot(jnp.concatenate(lhs_s, axis=1), x_bd,
                         preferred_element_type=F32)
            el = jnp.where(lane < SSM_HEADDIM, elb_ref[pl.ds(2 * p, 1), :],
                           elb_ref[pl.ds(2 * p + 1, 1), :])
            st_ref[p] = sp * el + su
            y_ref[p] = y + dsk_ref[p] * xp
        return carry

    lax.fori_loop(0, SSM_GROUPS, group, 0, unroll=8)

    gw = PAIRS_PER_GROUP * LANES
    for g in range(SSM_GROUPS):
        cs = slice(g * gw, (g + 1) * gw)
        yg = jnp.concatenate([y_ref[g * PAIRS_PER_GROUP + j] for j in range(PAIRS_PER_GROUP)],
                             axis=1)
        zg = z_ref[:, cs]
        yg = yg * (zg * jax.nn.sigmoid(zg))
        og = _rms(yg, nw_ref[:, cs]).astype(BF16)
        o_ref[:, cs] = jnp.dot(unperm_ref[...], og, preferred_element_type=F32).astype(o_ref.dtype)


def _ssd_core(proj, dt_raw, conv_w, conv_b, dt_bias, a_log, d_skip, norm_w, *, batch, seq):
    c = SSD_CHUNK
    nl = seq // c
    t = batch * seq
    perm = _interleave_matrix()
    tok = perm.argmax(axis=1)
    tri = (tok[None, :] <= tok[:, None]).astype(np.float32)
    tri3 = jnp.asarray(np.concatenate([tri, tri, tri], axis=1), BF16)
    unperm = jnp.asarray(perm.T, BF16)
    pad = LANES - SSM_HEADS
    dtb = jnp.pad(dt_bias.astype(F32), (0, pad)).reshape(1, LANES)
    alog = jnp.pad(a_log.astype(F32), (0, pad)).reshape(1, LANES)
    dsk = jnp.repeat(d_skip.astype(F32), SSM_HEADDIM).reshape(N_PAIRS, 1, LANES)

    def rb(b, l):
        return b * nl + l

    full2 = lambda b, l: (0, 0)
    in_specs = [
        pl.BlockSpec((c, SSM_DINNER), lambda b, l: (rb(b, l), 0)),
        pl.BlockSpec((c, SSM_DINNER), lambda b, l: (rb(b, l), 1)),
        pl.BlockSpec((c, 1024), lambda b, l: (rb(b, l), 2 * SSM_DINNER // 1024)),
        pl.BlockSpec((c, 1024), lambda b, l: (rb(b, l), 2 * SSM_DINNER // 1024 + 1)),
        pl.BlockSpec((c, LANES), lambda b, l: (rb(b, l), 0)),
        pl.BlockSpec((SSM_CONV, SSM_CONV_DIM), full2),
        pl.BlockSpec((1, SSM_CONV_DIM), full2),
        pl.BlockSpec((1, LANES), full2),
        pl.BlockSpec((1, LANES), full2),
        pl.BlockSpec((N_PAIRS, 1, LANES), lambda b, l: (0, 0, 0)),
        pl.BlockSpec((1, SSM_DINNER), full2),
        pl.BlockSpec(tri3.shape, full2),
        pl.BlockSpec(unperm.shape, full2),
    ]
    scratch = [
        pltpu.VMEM((CONV_HALO_ROWS, SSM_CONV_DIM), F32),
        pltpu.VMEM((N_PAIRS, c, LANES), F32),
        pltpu.VMEM((SSM_GROUPS, c, SSM_STATE), F32),
        pltpu.VMEM((SSM_GROUPS, c, SSM_STATE), F32),
        pltpu.VMEM((N_PAIRS, SSM_STATE, LANES), F32),
        pltpu.VMEM((N_PAIRS, c, LANES), F32),
        pltpu.VMEM((LANES, c), F32),
        pltpu.VMEM((LANES, c), F32),
        pltpu.VMEM((LANES, c), F32),
        pltpu.VMEM((LANES, LANES), F32),
    ]
    return pl.pallas_call(
        _ssd_kernel,
        grid=(batch, nl),
        in_specs=in_specs,
        out_specs=pl.BlockSpec((c, SSM_DINNER), lambda b, l: (rb(b, l), 0)),
        out_shape=jax.ShapeDtypeStruct((t, SSM_DINNER), BF16),
        scratch_shapes=scratch,
        compiler_params=_cparams(("parallel", "arbitrary")),
        name="ssd_core",
    )(proj, proj, proj, proj, dt_raw, conv_w.astype(F32), conv_b.astype(F32).reshape(1, -1),
      dtb, alog, dsk, norm_w.astype(F32).reshape(1, -1), tri3, unperm)


def _pad_last(w, n):
    return jnp.pad(w, [(0, 0)] * (w.ndim - 1) + [(0, n - w.shape[-1])])


def kernel(x, p, norm_mix, norm_mlp, norm_ple, norm_final, w_up, w_down, w_ple_proj, w_ple_gate,
           gla_w_in, gla_w_gk2, gla_b_gk, gla_gn, gla_w_out,
           hgrn_lb_logits, hgrn_w_in, hgrn_gn, hgrn_w_out,
           ssm_w_in, ssm_conv_w, ssm_conv_b, ssm_dt_bias, ssm_a_log, ssm_d, ssm_norm, ssm_w_out):
    batch, seq, d = x.shape
    t = batch * seq
    h = x.reshape(t, d)
    pf = p.reshape(DEPTH, t, PLE_DIM)
    row = lambda v: v.astype(F32).reshape(1, -1)

    gla_w_in_b = gla_w_in[..., :GLA_MAIN_COLS].astype(BF16)
    gla_w_lr_b = _pad_last(gla_w_in[..., GLA_MAIN_COLS:], LANES).astype(BF16)
    hgrn_w_in_b = hgrn_w_in.astype(BF16)
    ssm_w_in_b = ssm_w_in[..., :SSM_MAIN_COLS].astype(BF16)
    ssm_w_dt_b = _pad_last(ssm_w_in[..., SSM_MAIN_COLS:], LANES).astype(BF16)
    gla_w_out_b = gla_w_out.astype(BF16)
    hgrn_w_out_b = hgrn_w_out.astype(BF16)
    ssm_w_out_b = ssm_w_out.astype(BF16)
    w_up_b = w_up.astype(BF16)
    w_down_b = w_down.astype(BF16)
    w_gate_b = w_ple_gate.astype(BF16)
    w_proj_b = w_ple_proj.astype(BF16)

    for i in range(DEPTH):
        kind, j = i % N_MIXERS, i // N_MIXERS
        nw = row(norm_mix[i])
        if kind == 0:
            proj, glr = _norm_matmul(h, nw, gla_w_in_b, j, tm=1024, tn=2048, w_narrow=gla_w_lr_b)
            kd = GLA_HEADS * GLA_DK
            vd = GLA_HEADS * GLA_DV
            wgk = jnp.pad(gla_w_gk2[j], ((0, LANES - GLA_GATE_RANK), (0, 0))).astype(BF16)
            o = _gla_core(proj, mode="gla", layer=i, batch=batch, seq=seq, heads=GLA_HEADS,
                          hps=GLA_HEADS, pack=1, dk=GLA_DK, dv=GLA_DV, col_q=0, col_k=kd, col_v=2 * kd,
                          col_og=2 * kd + vd,
                          extra=(glr, wgk, row(gla_b_gk[j])), gn=row(gla_gn[j]))
            h = _matmul_res(o, gla_w_out_b, j, h, tm=1024)
        elif kind == 1:
            proj = _norm_matmul(h, nw, hgrn_w_in_b, j, tm=1024, tn=2048)
            fd = HGRN_HEADS * HGRN_DK
            vd = HGRN_HEADS * HGRN_DV
            o = _gla_core(proj, mode="hgrn", layer=i, batch=batch, seq=seq, heads=HGRN_HEADS,
                          hps=HGRN_HEADS // 2, pack=2, dk=HGRN_DK, dv=HGRN_DV, col_q=0, col_k=fd,
                          col_v=2 * fd, col_og=2 * fd + vd,
                          extra=(hgrn_lb_logits.astype(F32),), gn=row(hgrn_gn[j]),
                          tl=2 * GLA_TOKENS_PER_STEP)
            h = _matmul_res(o, hgrn_w_out_b, j, h, tm=1024)
        else:
            proj, dt_raw = _norm_matmul(h, nw, ssm_w_in_b, j, tm=1024, tn=2048, interleave=True,
                                        w_narrow=ssm_w_dt_b)
            o = _ssd_core(proj, dt_raw, ssm_conv_w[j], ssm_conv_b[j], ssm_dt_bias[j], ssm_a_log[j],
                          ssm_d[j], ssm_norm[j], batch=batch, seq=seq)
            h = _matmul_res(o, ssm_w_out_b, j, h, tm=512)
        h = _mlp(h, row(norm_mlp[i]), w_up_b, w_down_b, i, tm=1024, tf=1024)
        h = _ple(h, row(norm_ple[i]), w_gate_b, pf, w_proj_b, row(norm_final), i,
                 final_norm=(i == DEPTH - 1), tm=1024, tn=512)
    return h.reshape(batch, seq, d)
```

```python
import functools
import math

import numpy as np
import jax
import jax.numpy as jnp
from jax import lax
from jax.experimental import pallas as pl
from jax.experimental.pallas import tpu as pltpu

F32 = jnp.float32
BF16 = jnp.bfloat16

D_MODEL = 2048
DEPTH = 4
N_MIXERS = 3
PLE_DIM = 256
D_FF = 4 * D_MODEL
EPS = 1e-6

GLA_HEADS = 4
GLA_DK = 256
GLA_DV = 512
GLA_GATE_RANK = 16
GLA_GATE_NORM = 16.0
HGRN_HEADS = 16
HGRN_DK = 128
HGRN_DV = 128
SSM_DINNER = 4096
SSM_HEADDIM = 64
SSM_HEADS = 64
SSM_GROUPS = 8
SSM_STATE = 128
SSM_CONV = 4
SSM_CONV_DIM = SSM_DINNER + 2 * SSM_GROUPS * SSM_STATE

LANES = 128
VMEM_LIMIT_BYTES = 60 * 1024 * 1024

NORM_ROWS = 256
GLA_CHUNK = 64
GLA_TOKENS_PER_STEP = 512
SSD_CHUNK = 128
CONV_HALO_ROWS = 8 * (SSM_CONV - 1)

GLA_MAIN_COLS = 6144
SSM_MAIN_COLS = SSM_DINNER + SSM_CONV_DIM
N_LEVELS = 6
LOG2E = math.log2(math.e)
MASKED_LOG2 = -1e30


def _cparams(sem):
    return pltpu.CompilerParams(dimension_semantics=sem, vmem_limit_bytes=VMEM_LIMIT_BYTES)


def _rms(x, w):
    ms = jnp.mean(x * x, axis=-1, keepdims=True)
    return x * lax.rsqrt(ms + EPS) * w


def _split3(x):
    hi = x.astype(BF16)
    r1 = x - hi.astype(F32)
    mid = r1.astype(BF16)
    lo = (r1 - mid.astype(F32)).astype(BF16)
    return hi, mid, lo


def _interleave_matrix():
    r = np.arange(SSD_CHUNK)
    p = np.zeros((SSD_CHUNK, SSD_CHUNK), np.float32)
    p[r, (r % 8) * (SSD_CHUNK // 8) + r // 8] = 1.0
    return p


def _norm_matmul_kernel(*refs, interleave, narrow):
    refs = list(refs)
    h_ref, nw_ref, w_ref = refs[:3]
    u_ref = refs.pop()
    ox_ref = refs.pop() if narrow else None
    o_ref = refs.pop()
    wx_ref = refs.pop() if narrow else None
    pm_ref = refs.pop() if interleave else None
    j = pl.program_id(1)

    @pl.when(j == 0)
    def _():
        for r0 in range(0, h_ref.shape[0], NORM_ROWS):
            rs = slice(r0, r0 + NORM_ROWS)
            u = _rms(h_ref[rs, :], nw_ref[...]).astype(BF16)
            if interleave:
                u = jnp.concatenate(
                    [jnp.dot(pm_ref[...], u[g0:g0 + SSD_CHUNK], preferred_element_type=F32)
                     for g0 in range(0, NORM_ROWS, SSD_CHUNK)], axis=0).astype(BF16)
            u_ref[rs, :] = u
            o_ref[rs, :] = jnp.dot(u, w_ref[...], preferred_element_type=F32).astype(o_ref.dtype)
            if narrow:
                ox_ref[rs, :] = jnp.dot(u, wx_ref[...], preferred_element_type=F32)

    @pl.when(j > 0)
    def _():
        o_ref[...] = jnp.dot(u_ref[...], w_ref[...], preferred_element_type=F32).astype(o_ref.dtype)


def _norm_matmul(h, nw, w, layer, *, tm, tn, interleave=False, w_narrow=None):
    t, d = h.shape
    n = w.shape[2]
    in_specs = [
        pl.BlockSpec((tm, d), lambda i, j: (i, 0)),
        pl.BlockSpec((1, d), lambda i, j: (0, 0)),
        pl.BlockSpec((None, d, tn), lambda i, j: (layer, 0, j)),
    ]
    args = [h, nw, w]
    out_specs = pl.BlockSpec((tm, tn), lambda i, j: (i, j))
    out_shape = jax.ShapeDtypeStruct((t, n), F32)
    if interleave:
        in_specs.append(pl.BlockSpec((SSD_CHUNK, SSD_CHUNK), lambda i, j: (0, 0)))
        args.append(jnp.asarray(_interleave_matrix(), BF16))
    if w_narrow is not None:
        in_specs.append(pl.BlockSpec((None, d, LANES), lambda i, j: (layer, 0, 0)))
        args.append(w_narrow)
        out_specs = [out_specs, pl.BlockSpec((tm, LANES), lambda i, j: (i, 0))]
        out_shape = [out_shape, jax.ShapeDtypeStruct((t, LANES), F32)]
    return pl.pallas_call(
        functools.partial(_norm_matmul_kernel, interleave=interleave, narrow=w_narrow is not None),
        grid=(t // tm, n // tn),
        in_specs=in_specs,
        out_specs=out_specs,
        out_shape=out_shape,
        scratch_shapes=[pltpu.VMEM((tm, d), BF16)],
        compiler_params=_cparams(("parallel", "arbitrary")),
        name="norm_matmul",
    )(*args)


def _matmul_res_kernel(a_ref, w_ref, h_ref, o_ref):
    o_ref[...] = h_ref[...] + jnp.dot(a_ref[...], w_ref[...], preferred_element_type=F32)


def _matmul_res(a, w, layer, h, *, tm):
    t, k = a.shape
    n = w.shape[2]
    return pl.pallas_call(
        _matmul_res_kernel,
        grid=(t // tm,),
        in_specs=[
            pl.BlockSpec((tm, k), lambda i: (i, 0)),
            pl.BlockSpec((None, k, n), lambda i: (layer, 0, 0), pipeline_mode=pl.Buffered(1)),
            pl.BlockSpec((tm, n), lambda i: (i, 0)),
        ],
        out_specs=pl.BlockSpec((tm, n), lambda i: (i, 0)),
        out_shape=jax.ShapeDtypeStruct((t, n), F32),
        compiler_params=_cparams(("parallel",)),
        name="matmul_res",
    )(a, w, h)


def _mlp_kernel(h_ref, nw_ref, wu_ref, wd_ref, o_ref, u_ref):
    f = pl.program_id(1)

    def ff(u):
        a = jnp.dot(u, wu_ref[...], preferred_element_type=F32)
        a = jnp.square(jnp.maximum(a, 0.0)).astype(BF16)
        return jnp.dot(a, wd_ref[...], preferred_element_type=F32)

    @pl.when(f == 0)
    def _():
        for r0 in range(0, h_ref.shape[0], NORM_ROWS):
            rs = slice(r0, r0 + NORM_ROWS)
            x = h_ref[rs, :]
            u = _rms(x, nw_ref[...]).astype(BF16)
            u_ref[rs, :] = u
            o_ref[rs, :] = x + ff(u)

    @pl.when(f > 0)
    def _():
        o_ref[...] += ff(u_ref[...])


def _mlp(h, nw, wu, wd, layer, *, tm, tf):
    t, d = h.shape
    ff = wu.shape[2]
    return pl.pallas_call(
        _mlp_kernel,
        grid=(t // tm, ff // tf),
        in_specs=[
            pl.BlockSpec((tm, d), lambda i, f: (i, 0)),
            pl.BlockSpec((1, d), lambda i, f: (0, 0)),
            pl.BlockSpec((None, d, tf), lambda i, f: (layer, 0, f)),
            pl.BlockSpec((None, tf, d), lambda i, f: (layer, f, 0)),
        ],
        out_specs=pl.BlockSpec((tm, d), lambda i, f: (i, 0)),
        out_shape=jax.ShapeDtypeStruct((t, d), F32),
        scratch_shapes=[pltpu.VMEM((tm, d), BF16)],
        compiler_params=_cparams(("parallel", "arbitrary")),
        name="mlp",
    )(h, nw, wu, wd)


def _ple_kernel(h_ref, nw_ref, wg_ref, p_ref, wp_ref, nf_ref, o_ref, *, final_norm, tn):
    d = h_ref.shape[1]
    for r0 in range(0, h_ref.shape[0], NORM_ROWS):
        rs = slice(r0, r0 + NORM_ROWS)
        u = _rms(h_ref[rs, :], nw_ref[...]).astype(BF16)
        pb = p_ref[rs, :].astype(BF16)
        for n0 in range(0, d, tn):
            cs = slice(n0, n0 + tn)
            gate = jax.nn.sigmoid(jnp.dot(u, wg_ref[:, cs], preferred_element_type=F32))
            proj = jnp.dot(pb, wp_ref[:, cs], preferred_element_type=F32)
            o_ref[rs, cs] = h_ref[rs, cs] + gate * proj
        if final_norm:
            o_ref[rs, :] = _rms(o_ref[rs, :], nf_ref[...])


def _ple(h, nw, wg, p, wp, nf, layer, *, final_norm, tm, tn):
    t, d = h.shape
    pd = p.shape[2]
    return pl.pallas_call(
        functools.partial(_ple_kernel, final_norm=final_norm, tn=tn),
        grid=(t // tm,),
        in_specs=[
            pl.BlockSpec((tm, d), lambda i: (i, 0)),
            pl.BlockSpec((1, d), lambda i: (0, 0)),
            pl.BlockSpec((None, d, d), lambda i: (layer, 0, 0), pipeline_mode=pl.Buffered(1)),
            pl.BlockSpec((None, tm, pd), lambda i: (layer, i, 0)),
            pl.BlockSpec((None, pd, d), lambda i: (layer, 0, 0), pipeline_mode=pl.Buffered(1)),
            pl.BlockSpec((1, d), lambda i: (0, 0)),
        ],
        out_specs=pl.BlockSpec((tm, d), lambda i: (i, 0)),
        out_shape=jax.ShapeDtypeStruct((t, d), F32),
        compiler_params=_cparams(("parallel",)),
        name="ple",
    )(h, nw, wg, p, wp, nf)


MATMUL_LEVELS = (3, 4)


def _gla_arg_matrix():
    c = GLA_CHUNK
    t = np.arange(c)[:, None]
    r = np.arange(c)[None, :]
    blocks = [(r <= t)]
    for lvl in MATMUL_LEVELS:
        n = c >> (lvl + 1)
        ref = (t // (2 * n)) * (2 * n) + n - 1
        lower = t > ref
        blocks.append(np.where(lower, (r > ref) & (r <= t), (r > t) & (r <= ref)))
    m = np.concatenate(blocks, axis=0).astype(np.float32)
    return np.concatenate([m, m, m, np.zeros_like(m)], axis=1)


def _coarse_level_args(b, n):
    out = []
    for blk in range(GLA_CHUNK // (2 * n)):
        r0 = blk * 2 * n
        ref = b[r0 + n - 1:r0 + n]
        out.append(ref - b[r0:r0 + n])
        out.append(b[r0 + n:r0 + 2 * n] - ref)
    return jnp.concatenate(out, axis=0)


def _gla_masks(pack):
    c = GLA_CHUNK
    t = np.arange(c)[:, None]
    s = np.arange(c)[None, :]
    masks = np.zeros((1 + N_LEVELS, c, c), np.float32)
    masks[0] = (t == s)
    for lvl in range(N_LEVELS):
        n = c >> (lvl + 1)
        same = (t // (2 * n)) == (s // (2 * n))
        masks[1 + lvl] = same & ((t % (2 * n)) >= n) & ((s % (2 * n)) < n)
    return np.tile(masks, (1, 1, pack))


def _block_diag_rows(x, pack, width):
    if pack == 1:
        return x
    head = lax.broadcasted_iota(jnp.int32, x.shape, 1) // width
    return jnp.concatenate([jnp.where(head == p, x, jnp.zeros_like(x)) for p in range(pack)], axis=0)


_NT = (((1,), (1,)), ((), ()))
_TN = (((0,), (0,)), ((), ()))


def _gla_core_kernel(*refs, mode, layer, scale, packs, pack, dk, dv):
    wk = pack * dk
    wv = pack * dv
    if mode == "gla":
        (q_ref, k_ref, v_ref, og_ref, glr_ref, wgk_ref, bgk_ref, gn_ref, mall_ref, masks_ref,
         o_ref, st_ref) = refs
    else:
        (q_ref, k_ref, v_ref, og_ref, lbl_ref, gn_ref, mall_ref, masks_ref, o_ref, st_ref) = refs

    @pl.when(pl.program_id(2) == 0)
    def _():
        st_ref[...] = jnp.zeros_like(st_ref)

    if mode == "hgrn":
        lg = lbl_ref[...]
        e = jnp.exp(lg - jnp.max(lg, axis=0, keepdims=True))
        sm = e / jnp.sum(e, axis=0, keepdims=True)
        lb_all = jnp.zeros_like(sm[0:1])
        for r in range(1, layer + 1):
            lb_all = lb_all + sm[r:r + 1]

    c = GLA_CHUNK
    c2 = 2 * c
    n_chunks = q_ref.shape[0] // c2
    odd_row = lax.broadcasted_iota(jnp.int32, (c, wk), 0) % 2 == 1

    def nt(a, b):
        return lax.dot_general(a, b, _NT, preferred_element_type=F32)

    def half_terms(qh, kh, gh):
        g_hi, g_mid, g_lo = _split3(gh)
        gs = jnp.concatenate([g_hi, g_mid, g_lo, jnp.zeros_like(g_hi)], axis=0)
        pre = jnp.dot(mall_ref[...], gs, preferred_element_type=F32)
        b = pre[0:c]
        lvl_args = [_coarse_level_args(b, c >> (lvl + 1)) for lvl in range(MATMUL_LEVELS[0])]
        lvl_args += [pre[c:2 * c], pre[2 * c:3 * c], jnp.where(odd_row, gh, 0.0)]
        att = masks_ref[0] * nt(qh.astype(BF16), _block_diag_rows(kh.astype(BF16), pack, dk))
        for lvl in range(N_LEVELS):
            n = c >> (lvl + 1)
            el = jnp.exp2(lvl_args[lvl])
            if n % 8 == 0:
                mixed = jnp.concatenate(
                    [x[r0:r0 + n] for blk in range(c // (2 * n))
                     for x, r0 in ((kh, blk * 2 * n), (qh, blk * 2 * n + n))], axis=0)
                ql = (mixed * el).astype(BF16)
                kl = ql
            else:
                ql = (qh * el).astype(BF16)
                kl = (kh * el).astype(BF16)
            att = att + masks_ref[1 + lvl] * nt(ql, _block_diag_rows(kl, pack, dk))
        return b, att

    def chunk(ci, carry):
        rows = pl.ds(pl.multiple_of(ci * c2, c2), c2)
        if mode == "gla":
            z_all = jnp.dot(glr_ref[rows, :].astype(BF16), wgk_ref[...],
                            preferred_element_type=F32) + bgk_ref[...]
        for g in range(packs):
            ks = slice(g * wk, (g + 1) * wk)
            vs = slice(g * wv, (g + 1) * wv)
            vc = v_ref[rows, vs]
            og = og_ref[rows, vs]
            if mode == "gla":
                qc = q_ref[rows, ks] * scale
                kc = k_ref[rows, ks]
                z = z_all[:, ks]
                gc = (jnp.minimum(z, 0.0) - jnp.log1p(jnp.exp(-jnp.abs(z)))) * (LOG2E / GLA_GATE_NORM)
                gate = og * jax.nn.sigmoid(og)
            else:
                qq = q_ref[rows, ks]
                qc = qq * jax.nn.sigmoid(qq) * scale
                lb = lb_all[:, ks]
                f = lb + (1.0 - lb) * jax.nn.sigmoid(k_ref[rows, ks])
                kc = 1.0 - f
                gc = jnp.log(f) * LOG2E
                gate = jax.nn.sigmoid(og)

            q0, q1 = qc[0:c], qc[c:c2]
            k0, k1 = kc[0:c], kc[c:c2]
            b0, att0 = half_terms(q0, k0, gc[0:c])
            b1, att1 = half_terms(q1, k1, gc[c:c2])
            e_b0 = jnp.exp2(b0)
            e_b1 = jnp.exp2(b1)
            e_rev0 = jnp.exp2(b0[c - 1:c] - b0)
            e_rev1 = jnp.exp2(b1[c - 1:c] - b1)
            e_l0 = e_b0[c - 1:c]
            e_l1 = e_b1[c - 1:c]

            k0_end = (k0 * e_rev0).astype(BF16)
            cross = nt((q1 * e_b1).astype(BF16), _block_diag_rows(k0_end, pack, dk))

            st = st_ref[g]
            q_in = jnp.concatenate([q0 * e_b0, q1 * (e_b1 * e_l0)], axis=0).astype(BF16)
            o = nt(q_in, st.astype(BF16))
            vb = vc.astype(BF16)
            v0 = _block_diag_rows(vb[0:c], pack, dv)
            v1 = _block_diag_rows(vb[c:c2], pack, dv)
            o0 = o[0:c] + jnp.dot(att0.astype(BF16), v0, preferred_element_type=F32)
            o1 = (o[c:c2] + jnp.dot(cross.astype(BF16), v0, preferred_element_type=F32)
                  + jnp.dot(att1.astype(BF16), v1, preferred_element_type=F32))
            o = jnp.concatenate([o0, o1], axis=0)

            k_dec = jnp.concatenate([k0 * (e_rev0 * e_l1), k1 * e_rev1], axis=0).astype(BF16)
            e_step = e_l0 * e_l1
            for p in range(pack):
                rs = slice(p * dv, (p + 1) * dv)
                cs = slice(p * dk, (p + 1) * dk)
                upd = lax.dot_general(vb[:, rs], k_dec[:, cs], _TN, preferred_element_type=F32)
                st_ref[g, rs, cs] = st[rs, cs] * e_step[:, cs] + upd

            for p in range(pack):
                hs = slice(p * dv, (p + 1) * dv)
                o_ref[rows, g * wv + p * dv:g * wv + (p + 1) * dv] = (
                    _rms(o[:, hs], gn_ref[...]) * gate[:, hs]).astype(o_ref.dtype)
        return carry

    lax.fori_loop(0, n_chunks, chunk, 0, unroll=4 if pack > 1 else 2)


def _gla_core(proj, *, mode, layer, batch, seq, heads, hps, pack, dk, dv, col_q, col_k, col_v,
              col_og, extra, gn, tl=GLA_TOKENS_PER_STEP):
    nl = seq // tl
    t = batch * seq
    wk = hps * dk
    wv = hps * dv
    mall = jnp.asarray(_gla_arg_matrix(), BF16)
    masks = jnp.asarray(_gla_masks(pack), F32)

    def rowblk(b, h, l):
        return b * nl + l

    in_specs = [
        pl.BlockSpec((tl, wk), lambda b, h, l: (rowblk(b, h, l), col_q // wk + h)),
        pl.BlockSpec((tl, wk), lambda b, h, l: (rowblk(b, h, l), col_k // wk + h)),
        pl.BlockSpec((tl, wv), lambda b, h, l: (rowblk(b, h, l), col_v // wv + h)),
        pl.BlockSpec((tl, wv), lambda b, h, l: (rowblk(b, h, l), col_og // wv + h)),
    ]
    args = [proj, proj, proj, proj]
    if mode == "gla":
        glr, wgk, bgk = extra
        in_specs += [
            pl.BlockSpec((tl, LANES), lambda b, h, l: (rowblk(b, h, l), 0)),
            pl.BlockSpec((LANES, wk), lambda b, h, l: (0, h)),
            pl.BlockSpec((1, wk), lambda b, h, l: (0, h)),
        ]
        args += [glr, wgk, bgk]
    else:
        (lbl,) = extra
        in_specs += [pl.BlockSpec((DEPTH, wk), lambda b, h, l: (0, h))]
        args += [lbl]
    in_specs += [
        pl.BlockSpec((1, dv), lambda b, h, l: (0, 0)),
        pl.BlockSpec(mall.shape, lambda b, h, l: (0, 0)),
        pl.BlockSpec(masks.shape, lambda b, h, l: (0, 0, 0)),
    ]
    args += [gn, mall, masks]
    return pl.pallas_call(
        functools.partial(_gla_core_kernel, mode=mode, layer=layer, scale=dk ** -0.5,
                          packs=hps // pack, pack=pack, dk=dk, dv=dv),
        grid=(batch, heads // hps, nl),
        in_specs=in_specs,
        out_specs=pl.BlockSpec((tl, wv), lambda b, h, l: (rowblk(b, h, l), h)),
        out_shape=jax.ShapeDtypeStruct((t, heads * dv), BF16),
        scratch_shapes=[pltpu.VMEM((hps // pack, pack * dv, pack * dk), F32)],
        compiler_params=_cparams(("parallel", "parallel", "arbitrary")),
        name=mode + "_core",
    )(*args)


N_PAIRS = SSM_HEADS // 2
PAIRS_PER_GROUP = N_PAIRS // SSM_GROUPS


def _ssd_kernel(z_ref, x_ref, bm_ref, cm_ref, dt_ref, cw_ref, cb_ref, dtb_ref, alog_ref, dsk_ref,
                nw_ref, tri3_ref, unperm_ref, o_ref,
                halo_ref, xs_ref, bs_ref, cs_ref, st_ref, y_ref, bt_ref, dtt_ref, wdt_ref, elb_ref):
    c = SSD_CHUNK
    nt_rows = c // 8
    hr = CONV_HALO_ROWS
    l = pl.program_id(1)

    @pl.when(l == 0)
    def _():
        st_ref[...] = jnp.zeros_like(st_ref)
        halo_ref[...] = jnp.zeros_like(halo_ref)

    first_sublane = lax.broadcasted_iota(jnp.int32, (8, LANES), 0) == 0
    for slab in range(SSM_CONV_DIM // LANES):
        cs = slice(slab * LANES, (slab + 1) * LANES)
        if slab < SSM_DINNER // LANES:
            cur = x_ref[:, cs]
        elif slab < (SSM_DINNER + SSM_GROUPS * SSM_STATE) // LANES:
            cur = bm_ref[:, slab * LANES - SSM_DINNER:(slab + 1) * LANES - SSM_DINNER]
        else:
            off = SSM_DINNER + SSM_GROUPS * SSM_STATE
            cur = cm_ref[:, slab * LANES - off:(slab + 1) * LANES - off]
        prev_tail = halo_ref[:, cs]
        wrapped = []
        for s in range(SSM_CONV - 1):
            tail = cur[c - hr + 8 * s:c - hr + 8 * (s + 1)]
            moved = pltpu.roll(tail, 1, 0)
            wrapped.append(jnp.where(first_sublane, prev_tail[8 * s + 7:8 * s + 8], moved))
        halo_ref[:, cs] = cur[c - hr:c]
        acc = cb_ref[:, cs] + cw_ref[SSM_CONV - 1:SSM_CONV, cs] * cur
        for s in range(1, SSM_CONV):
            shifted = jnp.concatenate(wrapped[SSM_CONV - 1 - s:] + [cur[0:c - 8 * s]], axis=0)
            acc = acc + cw_ref[SSM_CONV - 1 - s:SSM_CONV - s, cs] * shifted
        act = acc * jax.nn.sigmoid(acc)
        if slab < N_PAIRS:
            xs_ref[slab] = act
        elif slab < N_PAIRS + SSM_GROUPS:
            bs_ref[slab - N_PAIRS] = act
        else:
            cs_ref[slab - N_PAIRS - SSM_GROUPS] = act

    lane = lax.broadcasted_iota(jnp.int32, (1, LANES), 1)
    dtr = dt_ref[...] + dtb_ref[...]
    dt = jnp.maximum(dtr, 0.0) + jnp.log1p(jnp.exp(-jnp.abs(dtr)))
    a = jnp.where(lane < SSM_HEADS, -jnp.exp(alog_ref[...]), 0.0)
    la_hi, la_mid, la_lo = _split3(dt * (a * LOG2E))
    b = jnp.dot(tri3_ref[...], jnp.concatenate([la_hi, la_mid, la_lo], axis=0),
                preferred_element_type=F32)
    b_last = b[c - 1:c]
    wd = jnp.exp2(b_last - b) * dt
    bt = b.T
    bt_ref[...] = bt
    dtt_ref[...] = (b - jnp.log2(dt)).T
    wdt_ref[...] = wd.T
    elb_ref[...] = jnp.broadcast_to(jnp.exp2(bt[:, c - 1:c]), (LANES, LANES))

    row = lax.broadcasted_iota(jnp.int32, (c, c), 0)
    col = lax.broadcasted_iota(jnp.int32, (c, c), 1)
    causal = (col % 8) * nt_rows + col // 8 <= (row % 8) * nt_rows + row // 8
    lo = lax.broadcasted_iota(jnp.int32, (c, LANES), 1) < SSM_HEADDIM

    def group(g, carry):
        bg = bs_ref[g]
        cg = cs_ref[g]
        cgb = cg.astype(BF16)
        cbm = lax.dot_general(cgb, bg.astype(BF16), _NT, preferred_element_type=F32)
        bgt = bg.T
        for j in range(PAIRS_PER_GROUP):
            p = g * PAIRS_PER_GROUP + j
            xp = xs_ref[p]
            sp = st_ref[p]
            x_lo = jnp.where(lo, xp, 0.0).astype(BF16)
            x_hi = jnp.where(lo, 0.0, xp).astype(BF16)
            x_bd = jnp.concatenate([x_lo, x_hi], axis=0)
            lhs_y = []
            lhs_s = []
            bhs = []
            for k in range(2):
                h = 2 * p + k
                bh = jnp.broadcast_to(bt_ref[pl.ds(h, 1), :], (c, c)).T
                rel = bh - dtt_ref[pl.ds(h, 1), :]
                dec = jnp.exp2(jnp.where(causal, rel, MASKED_LOG2))
                lhs_y.append((cbm * dec).astype(BF16))
                lhs_s.append((bgt * wdt_ref[pl.ds(h, 1), :]).astype(BF16))
                bhs.append(bh)
            y = jnp.exp2(jnp.where(lo, bhs[0], bhs[1])) * jnp.dot(
                cgb, sp.astype(BF16), preferred_element_type=F32)
            y = y + jnp.dot(jnp.concatenate(lhs_y, axis=1), x_bd, preferred_element_type=F32)
            su = jnp.dot(jnp.concatenate(lhs_s, axis=1), x_bd,
                         preferred_element_type=F32)
            el = jnp.where(lane < SSM_HEADDIM, elb_ref[pl.ds(2 * p, 1), :],
                           elb_ref[pl.ds(2 * p + 1, 1), :])
            st_ref[p] = sp * el + su
            y_ref[p] = y + dsk_ref[p] * xp
        return carry

    lax.fori_loop(0, SSM_GROUPS, group, 0, unroll=8)

    gw = PAIRS_PER_GROUP * LANES
    for g in range(SSM_GROUPS):
        cs = slice(g * gw, (g + 1) * gw)
        yg = jnp.concatenate([y_ref[g * PAIRS_PER_GROUP + j] for j in range(PAIRS_PER_GROUP)],
                             axis=1)
        zg = z_ref[:, cs]
        yg = yg * (zg * jax.nn.sigmoid(zg))
        og = _rms(yg, nw_ref[:, cs]).astype(BF16)
        o_ref[:, cs] = jnp.dot(unperm_ref[...], og, preferred_element_type=F32).astype(o_ref.dtype)


def _ssd_core(proj, dt_raw, conv_w, conv_b, dt_bias, a_log, d_skip, norm_w, *, batch, seq):
    c = SSD_CHUNK
    nl = seq // c
    t = batch * seq
    perm = _interleave_matrix()
    tok = perm.argmax(axis=1)
    tri = (tok[None, :] <= tok[:, None]).astype(np.float32)
    tri3 = jnp.asarray(np.concatenate([tri, tri, tri], axis=1), BF16)
    unperm = jnp.asarray(perm.T, BF16)
    pad = LANES - SSM_HEADS
    dtb = jnp.pad(dt_bias.astype(F32), (0, pad)).reshape(1, LANES)
    alog = jnp.pad(a_log.astype(F32), (0, pad)).reshape(1, LANES)
    dsk = jnp.repeat(d_skip.astype(F32), SSM_HEADDIM).reshape(N_PAIRS, 1, LANES)

    def rb(b, l):
        return b * nl + l

    full2 = lambda b, l: (0, 0)
    in_specs = [
        pl.BlockSpec((c, SSM_DINNER), lambda b, l: (rb(b, l), 0)),
        pl.BlockSpec((c, SSM_DINNER), lambda b, l: (rb(b, l), 1)),
        pl.BlockSpec((c, 1024), lambda b, l: (rb(b, l), 2 * SSM_DINNER // 1024)),
        pl.BlockSpec((c, 1024), lambda b, l: (rb(b, l), 2 * SSM_DINNER // 1024 + 1)),
        pl.BlockSpec((c, LANES), lambda b, l: (rb(b, l), 0)),
        pl.BlockSpec((SSM_CONV, SSM_CONV_DIM), full2),
        pl.BlockSpec((1, SSM_CONV_DIM), full2),
        pl.BlockSpec((1, LANES), full2),
        pl.BlockSpec((1, LANES), full2),
        pl.BlockSpec((N_PAIRS, 1, LANES), lambda b, l: (0, 0, 0)),
        pl.BlockSpec((1, SSM_DINNER), full2),
        pl.BlockSpec(tri3.shape, full2),
        pl.BlockSpec(unperm.shape, full2),
    ]
    scratch = [
        pltpu.VMEM((CONV_HALO_ROWS, SSM_CONV_DIM), F32),
        pltpu.VMEM((N_PAIRS, c, LANES), F32),
        pltpu.VMEM((SSM_GROUPS, c, SSM_STATE), F32),
        pltpu.VMEM((SSM_GROUPS, c, SSM_STATE), F32),
        pltpu.VMEM((N_PAIRS, SSM_STATE, LANES), F32),
        pltpu.VMEM((N_PAIRS, c, LANES), F32),
        pltpu.VMEM((LANES, c), F32),
        pltpu.VMEM((LANES, c), F32),
        pltpu.VMEM((LANES, c), F32),
        pltpu.VMEM((LANES, LANES), F32),
    ]
    return pl.pallas_call(
        _ssd_kernel,
        grid=(batch, nl),
        in_specs=in_specs,
        out_specs=pl.BlockSpec((c, SSM_DINNER), lambda b, l: (rb(b, l), 0)),
        out_shape=jax.ShapeDtypeStruct((t, SSM_DINNER), BF16),
        scratch_shapes=scratch,
        compiler_params=_cparams(("parallel", "arbitrary")),
        name="ssd_core",
    )(proj, proj, proj, proj, dt_raw, conv_w.astype(F32), conv_b.astype(F32).reshape(1, -1),
      dtb, alog, dsk, norm_w.astype(F32).reshape(1, -1), tri3, unperm)


def _pad_last(w, n):
    return jnp.pad(w, [(0, 0)] * (w.ndim - 1) + [(0, n - w.shape[-1])])


def kernel(x, p, norm_mix, norm_mlp, norm_ple, norm_final, w_up, w_down, w_ple_proj, w_ple_gate,
           gla_w_in, gla_w_gk2, gla_b_gk, gla_gn, gla_w_out,
           hgrn_lb_logits, hgrn_w_in, hgrn_gn, hgrn_w_out,
           ssm_w_in, ssm_conv_w, ssm_conv_b, ssm_dt_bias, ssm_a_log, ssm_d, ssm_norm, ssm_w_out):
    batch, seq, d = x.shape
    t = batch * seq
    h = x.reshape(t, d)
    pf = p.reshape(DEPTH, t, PLE_DIM)
    row = lambda v: v.astype(F32).reshape(1, -1)

    gla_w_in_b = gla_w_in[..., :GLA_MAIN_COLS].astype(BF16)
    gla_w_lr_b = _pad_last(gla_w_in[..., GLA_MAIN_COLS:], LANES).astype(BF16)
    hgrn_w_in_b = hgrn_w_in.astype(BF16)
    ssm_w_in_b = ssm_w_in[..., :SSM_MAIN_COLS].astype(BF16)
    ssm_w_dt_b = _pad_last(ssm_w_in[..., SSM_MAIN_COLS:], LANES).astype(BF16)
    gla_w_out_b = gla_w_out.astype(BF16)
    hgrn_w_out_b = hgrn_w_out.astype(BF16)
    ssm_w_out_b = ssm_w_out.astype(BF16)
    w_up_b = w_up.astype(BF16)
    w_down_b = w_down.astype(BF16)
    w_gate_b = w_ple_gate.astype(BF16)
    w_proj_b = w_ple_proj.astype(BF16)

    for i in range(DEPTH):
        kind, j = i % N_MIXERS, i // N_MIXERS
        nw = row(norm_mix[i])
        if kind == 0:
            proj, glr = _norm_matmul(h, nw, gla_w_in_b, j, tm=1024, tn=2048, w_narrow=gla_w_lr_b)
            kd = GLA_HEADS * GLA_DK
            vd = GLA_HEADS * GLA_DV
            wgk = jnp.pad(gla_w_gk2[j], ((0, LANES - GLA_GATE_RANK), (0, 0))).astype(BF16)
            o = _gla_core(proj, mode="gla", layer=i, batch=batch, seq=seq, heads=GLA_HEADS,
                          hps=GLA_HEADS, pack=1, dk=GLA_DK, dv=GLA_DV, col_q=0, col_k=kd, col_v=2 * kd,
                          col_og=2 * kd + vd,
                          extra=(glr, wgk, row(gla_b_gk[j])), gn=row(gla_gn[j]))
            h = _matmul_res(o, gla_w_out_b, j, h, tm=1024)
        elif kind == 1:
            proj = _norm_matmul(h, nw, hgrn_w_in_b, j, tm=1024, tn=2048)
            fd = HGRN_HEADS * HGRN_DK
            vd = HGRN_HEADS * HGRN_DV
            o = _gla_core(proj, mode="hgrn", layer=i, batch=batch, seq=seq, heads=HGRN_HEADS,
                          hps=HGRN_HEADS // 2, pack=2, dk=HGRN_DK, dv=HGRN_DV, col_q=0, col_k=fd,
                          col_v=2 * fd, col_og=2 * fd + vd,
                          extra=(hgrn_lb_logits.astype(F32),), gn=row(hgrn_gn[j]),
                          tl=2 * GLA_TOKENS_PER_STEP)
            h = _matmul_res(o, hgrn_w_out_b, j, h, tm=1024)
        else:
            proj, dt_raw = _norm_matmul(h, nw, ssm_w_in_b, j, tm=1024, tn=2048, interleave=True,
                                        w_narrow=ssm_w_dt_b)
            o = _ssd_core(proj, dt_raw, ssm_conv_w[j], ssm_conv_b[j], ssm_dt_bias[j], ssm_a_log[j],
                          ssm_d[j], ssm_norm[j], batch=batch, seq=seq)
            h = _matmul_res(o, ssm_w_out_b, j, h, tm=512)
        h = _mlp(h, row(norm_mlp[i]), w_up_b, w_down_b, i, tm=1024, tf=1024)
        h = _ple(h, row(norm_ple[i]), w_gate_b, pf, w_proj_b, row(norm_final), i,
                 final_norm=(i == DEPTH - 1), tm=1024, tn=512)
    return h.reshape(batch, seq, d)
```
